```python
import math
import jax
import jax.numpy as jnp
from jax import lax
import numpy as np

D_MODEL = 2048
BATCH = 16
SEQ = 2048
DEPTH = 2
DEC_BATCH = 2
DEC_SEQ = 16384
PAST_LEN = 128

GRID_W = 64
N_EVEN = (DEPTH + 1) // 2
N_ODD = DEPTH // 2
EPS = 1e-6
NA_HEADS = 8
NA_HEAD_DIM = 128
NA_WIN_ROWS = 8
NA_WIN_COLS = 16
NA_COL_BLOCK = 16
NA_KEY_COLS = 32
NA_W = NA_HEADS * NA_HEAD_DIM
HG_HEADS = 8
HG_KEY_DIM = 128
HG_VAL_DIM = 128
HG_CHUNK = 32
HG_K = HG_HEADS * HG_KEY_DIM
HG_V = HG_HEADS * HG_VAL_DIM
EVEN_IN = 3 * NA_W + 3 * HG_K + 2 * HG_V
EVEN_MIX = NA_W + HG_V
MLA_HEADS = 16
MLA_Q_RANK = 768
MLA_KV_RANK = 512
MLA_NOPE = 128
MLA_ROPE = 64
MLA_V = 128
MLA_QK = MLA_NOPE + MLA_ROPE
ODD_IN = MLA_Q_RANK + MLA_KV_RANK + MLA_ROPE
ROPE_BASE = 10000.0
ATTN_Q_BLOCK = 128
PEER_HEADS = 8
PEER_N_KEYS = 128
PEER_N_EXPERTS = PEER_N_KEYS * PEER_N_KEYS
PEER_TOPK = 16
PEER_D_KEY = 256
PEER_TOKEN_BLOCK = 128

kernel_name = 'hybrid_natten_hgrn2_mla_peer_encoder'


def _rmsnorm(x, gain):
    xf = x.astype(jnp.float32)
    y = xf * lax.rsqrt(jnp.mean(xf * xf, axis=-1, keepdims=True) + EPS)
    return (y * gain.astype(jnp.float32)).astype(x.dtype)


def _neighbourhood_attention(q, k, v, rpb):
    B, S, H, Dh = q.shape
    rows = S // GRID_W
    wr = min(NA_WIN_ROWS, rows)
    n_cb = GRID_W // NA_COL_BLOCK
    qcol = np.arange(GRID_W).reshape(n_cb, NA_COL_BLOCK)
    qc0 = np.clip(qcol - NA_WIN_COLS // 2, 0, GRID_W - NA_WIN_COLS)
    kc0 = np.clip(np.arange(n_cb) * NA_COL_BLOCK - NA_WIN_COLS // 2, 0, GRID_W - NA_KEY_COLS)
    kcol = kc0[:, None] + np.arange(NA_KEY_COLS)[None, :]
    col_ok = (kcol[:, None, :] >= qc0[:, :, None]) & (kcol[:, None, :] < qc0[:, :, None] + NA_WIN_COLS)
    dcol = np.clip(kcol[:, None, :] - qcol[:, :, None] + NA_WIN_COLS - 1, 0, 2 * NA_WIN_COLS - 2)
    col_bias = rpb.astype(jnp.float32)[:, :, dcol]
    mask = jnp.asarray(col_ok)[:, :, None, :]
    qg = q.reshape(B, rows, n_cb, NA_COL_BLOCK, H, Dh).transpose(1, 0, 2, 3, 4, 5)
    kg = k.reshape(B, rows, GRID_W, H, Dh)[:, :, kcol]
    vg = v.reshape(B, rows, GRID_W, H, Dh)[:, :, kcol]
    scale = NA_HEAD_DIM ** -0.5

    def row_block(args):
        r, q_r = args
        r0 = jnp.clip(r - wr // 2, 0, rows - wr)
        k_r = lax.dynamic_slice_in_dim(kg, r0, wr, axis=1)
        v_r = lax.dynamic_slice_in_dim(vg, r0, wr, axis=1)
        s = jnp.einsum('bjqhd,bwjkhd->bhjqwk', q_r, k_r).astype(jnp.float32) * scale
        dr = r0 + jnp.arange(wr) - r + (NA_WIN_ROWS - 1)
        bias = jnp.take(col_bias, dr, axis=1).transpose(0, 2, 3, 1, 4)
        s = jnp.where(mask, s + bias, -jnp.inf)
        p = jax.nn.softmax(s.reshape(s.shape[:4] + (wr * NA_KEY_COLS,)), axis=-1).reshape(s.shape)
        return jnp.einsum('bhjqwk,bwjkhd->bjqhd', p.astype(v.dtype), v_r)

    out = lax.map(row_block, (jnp.arange(rows), qg))
    return out.transpose(1, 0, 2, 3, 4, 5).reshape(B, S, H * Dh)


def _hgrn2_scan(q, f_logit, lb, i):
    B, S, H, K = q.shape
    V = i.shape[-1]
    C = HG_CHUNK
    N = S // C
    lbh = lb.astype(jnp.float32).reshape(H, K)
    f = lbh + (1.0 - lbh) * jax.nn.sigmoid(f_logit.astype(jnp.float32))
    logf = jnp.log(f)
    kk = 1.0 - f

    def chunks(a):
        return a.astype(jnp.float32).reshape(B, N, C, H, a.shape[-1]).transpose(1, 0, 3, 2, 4)

    causal = jnp.tril(jnp.ones((C, C), dtype=bool))[:, :, None]

    def step(state, inp):
        qc, lc, kc, vc = inp
        b = jnp.cumsum(lc, axis=2)
        o_inter = jnp.einsum('bhtk,bhkv->bhtv', qc * jnp.exp(b), state)
        decay = jnp.exp(jnp.where(causal, b[:, :, :, None, :] - b[:, :, None, :, :], -jnp.inf))
        a = jnp.einsum('bhtsk,bhsk->bhts', qc[:, :, :, None, :] * decay, kc)
        o_intra = jnp.einsum('bhts,bhsv->bhtv', a, vc)
        b_last = b[:, :, -1:, :]
        new_state = jnp.exp(b_last[:, :, 0, :])[..., None] * state + jnp.einsum('bhsk,bhsv->bhkv', kc * jnp.exp(b_last - b), vc)
        return new_state, o_inter + o_intra

    state0 = jnp.zeros((B, H, K, V), jnp.float32)
    _, o = lax.scan(step, state0, (chunks(q), chunks(logf), chunks(kk), chunks(i)))
    return o.transpose(1, 0, 3, 2, 4).reshape(B, S, H, V)


def _even_mixer(x, norm, w_in, na_q_gain, na_k_gain, na_rpb, lb, hg_out_gain, w_out):
    B, S, _ = x.shape
    h = _rmsnorm(x, norm)
    p = h @ w_in
    sizes = [NA_W, NA_W, NA_W, HG_K, HG_K, HG_K, HG_V, HG_V]
    cuts = [int(c) for c in np.cumsum(sizes)[:-1]]
    qa, ka, va, qb, f_fw, f_bw, ib, gb = jnp.split(p, cuts, axis=-1)
    qa = _rmsnorm(qa.reshape(B, S, NA_HEADS, NA_HEAD_DIM), na_q_gain)
    ka = _rmsnorm(ka.reshape(B, S, NA_HEADS, NA_HEAD_DIM), na_k_gain)
    na = _neighbourhood_attention(qa, ka, va.reshape(B, S, NA_HEADS, NA_HEAD_DIM), na_rpb)
    qb = jax.nn.silu(qb.reshape(B, S, HG_HEADS, HG_KEY_DIM))
    f_fw = f_fw.reshape(B, S, HG_HEADS, HG_KEY_DIM)
    f_bw = f_bw.reshape(B, S, HG_HEADS, HG_KEY_DIM)
    ib = ib.reshape(B, S, HG_HEADS, HG_VAL_DIM)
    o = _hgrn2_scan(qb, f_fw, lb[0], ib) + _hgrn2_scan(qb[:, ::-1], f_bw[:, ::-1], lb[1], ib[:, ::-1])[:, ::-1]
    gate = jax.nn.silu(gb.reshape(B, S, HG_HEADS, HG_VAL_DIM).astype(jnp.float32))
    hg = (_rmsnorm(o, hg_out_gain) * gate).astype(x.dtype).reshape(B, S, HG_V)
    return jnp.concatenate([na, hg], axis=-1) @ w_out


def _rope_tail(t):
    S = t.shape[1]
    half = MLA_ROPE // 2
    pos = jnp.arange(S, dtype=jnp.float32)
    inv = 1.0 / (ROPE_BASE ** (jnp.arange(0, MLA_ROPE, 2, dtype=jnp.float32) / MLA_ROPE))
    ang = pos[:, None] * inv[None, :]
    cos = jnp.cos(ang)[None, :, None, :]
    sin = jnp.sin(ang)[None, :, None, :]
    tf = t.astype(jnp.float32)
    x1 = tf[..., MLA_NOPE:MLA_NOPE + half]
    x2 = tf[..., MLA_NOPE + half:]
    rot = jnp.concatenate([x1 * cos - x2 * sin, x2 * cos + x1 * sin], axis=-1)
    return jnp.concatenate([tf[..., :MLA_NOPE], rot], axis=-1).astype(t.dtype)


def _mla(x, norm, w_in, q_norm, kv_norm, w_uq, w_ukv, q_gain, k_gain, w_out):
    B, S, _ = x.shape
    h = _rmsnorm(x, norm)
    c_q, c_kv, k_rope = jnp.split(h @ w_in, [MLA_Q_RANK, MLA_Q_RANK + MLA_KV_RANK], axis=-1)
    q = (_rmsnorm(c_q, q_norm) @ w_uq).reshape(B, S, MLA_HEADS, MLA_QK)
    kv = (_rmsnorm(c_kv, kv_norm) @ w_ukv).reshape(B, S, MLA_HEADS, MLA_NOPE + MLA_V)
    k_nope, v = jnp.split(kv, [MLA_NOPE], axis=-1)
    k = jnp.concatenate([k_nope, jnp.broadcast_to(k_rope[:, :, None, :], (B, S, MLA_HEADS, MLA_ROPE))], axis=-1)
    q = _rope_tail(_rmsnorm(q, q_gain))
    k = _rope_tail(_rmsnorm(k, k_gain))
    nb = S // ATTN_Q_BLOCK
    qb = q.reshape(B, nb, ATTN_Q_BLOCK, MLA_HEADS, MLA_QK).transpose(1, 0, 2, 3, 4)
    scale = MLA_QK ** -0.5

    def block(q_blk):
        s = jnp.einsum('bqhd,bkhd->bhqk', q_blk, k).astype(jnp.float32) * scale
        p = jax.nn.softmax(s, axis=-1).astype(v.dtype)
        return jnp.einsum('bhqk,bkhd->bqhd', p, v)

    o = lax.map(block, qb).transpose(1, 0, 2, 3, 4).reshape(B, S, MLA_HEADS * MLA_V)
    return o @ w_out


def _peer(h, w_query, sub_keys, u, v):
    B, S, D = h.shape
    T = B * S
    x = h.reshape(T, D)
    qry = (x @ w_query).reshape(T, PEER_HEADS, 2, PEER_D_KEY // 2)
    s = jnp.einsum('thpd,hpnd->thpn', qry, sub_keys).astype(jnp.float32)
    s1, i1 = lax.top_k(s[:, :, 0], PEER_TOPK)
    s2, i2 = lax.top_k(s[:, :, 1], PEER_TOPK)
    cand = (s1[..., :, None] + s2[..., None, :]).reshape(T, PEER_HEADS, PEER_TOPK * PEER_TOPK)
    cidx = (i1[..., :, None] * PEER_N_KEYS + i2[..., None, :]).reshape(T, PEER_HEADS, PEER_TOPK * PEER_TOPK)
    top_s, pos = lax.top_k(cand, PEER_TOPK)
    idx = jnp.take_along_axis(cidx, pos, axis=-1)
    gate = jax.nn.softmax(top_s, axis=-1).astype(h.dtype)
    nb = T // PEER_TOKEN_BLOCK
    E = PEER_HEADS * PEER_TOPK

    def blk(args):
        xb, ib, gb = args
        ue = jnp.take(u, ib, axis=0)
        ve = jnp.take(v, ib, axis=0)
        a = jax.nn.gelu(jnp.einsum('td,ted->te', xb, ue), approximate=False)
        return jnp.einsum('te,ted->td', gb * a, ve)

    out = lax.map(blk, (x.reshape(nb, PEER_TOKEN_BLOCK, D), idx.reshape(nb, PEER_TOKEN_BLOCK, E), gate.reshape(nb, PEER_TOKEN_BLOCK, E)))
    return out.reshape(B, S, D)


def setup_inputs(seed: int = 0) -> dict:
    key = jax.random.key(seed)
    ks = iter(jax.random.split(key, 32))

    def nrm(shape, scale):
        return jax.random.normal(next(ks), shape, jnp.float32) * scale

    def gain(shape):
        return 1.0 + 0.02 * jax.random.normal(next(ks), shape, jnp.float32)

    return {
        'x_prompt': nrm((BATCH, SEQ, D_MODEL), 1.0),
        'x_sample': nrm((DEC_BATCH, DEC_SEQ, D_MODEL), 1.0),
        'ev_norm': gain((N_EVEN, D_MODEL)),
        'ev_w_in': nrm((N_EVEN, D_MODEL, EVEN_IN), D_MODEL ** -0.5),
        'ev_na_q_gain': gain((N_EVEN, NA_HEAD_DIM)),
        'ev_na_k_gain': gain((N_EVEN, NA_HEAD_DIM)),
        'ev_na_rpb': nrm((N_EVEN, NA_HEADS, 2 * NA_WIN_ROWS - 1, 2 * NA_WIN_COLS - 1), 0.02),
        'ev_hg_lb_logits': nrm((2, N_EVEN + 1, HG_K), 0.5),
        'ev_hg_out_gain': gain((N_EVEN, HG_VAL_DIM)),
        'ev_w_out': nrm((N_EVEN, EVEN_MIX, D_MODEL), EVEN_MIX ** -0.5),
        'od_norm': gain((N_ODD, D_MODEL)),
        'od_w_in': nrm((N_ODD, D_MODEL, ODD_IN), D_MODEL ** -0.5),
        'od_q_norm': gain((N_ODD, MLA_Q_RANK)),
        'od_kv_norm': gain((N_ODD, MLA_KV_RANK)),
        'od_w_uq': nrm((N_ODD, MLA_Q_RANK, MLA_HEADS * MLA_QK), MLA_Q_RANK ** -0.5),
        'od_w_ukv': nrm((N_ODD, MLA_KV_RANK, MLA_HEADS * (MLA_NOPE + MLA_V)), MLA_KV_RANK ** -0.5),
        'od_q_gain': gain((N_ODD, MLA_QK)),
        'od_k_gain': gain((N_ODD, MLA_QK)),
        'od_w_out': nrm((N_ODD, MLA_HEADS * MLA_V, D_MODEL), (MLA_HEADS * MLA_V) ** -0.5),
        'ff_norm': gain((DEPTH, D_MODEL)),
        'peer_w_query': nrm((DEPTH, D_MODEL, PEER_HEADS * PEER_D_KEY), D_MODEL ** -0.5),
        'peer_sub_keys': nrm((DEPTH, PEER_HEADS, 2, PEER_N_KEYS, PEER_D_KEY // 2), (PEER_D_KEY // 2) ** -0.5),
        'peer_u': nrm((DEPTH, PEER_N_EXPERTS, D_MODEL), D_MODEL ** -0.5),
        'peer_v': nrm((DEPTH, PEER_N_EXPERTS, D_MODEL), (PEER_HEADS * PEER_TOPK) ** -0.5),
    }


def reference(x_prompt, x_sample, ev_norm, ev_w_in, ev_na_q_gain, ev_na_k_gain, ev_na_rpb, ev_hg_lb_logits, ev_hg_out_gain, ev_w_out, od_norm, od_w_in, od_q_norm, od_kv_norm, od_w_uq, od_w_ukv, od_q_gain, od_k_gain, od_w_out, ff_norm, peer_w_query, peer_sub_keys, peer_u, peer_v):
    lb_all = jnp.cumsum(jax.nn.softmax(ev_hg_lb_logits.astype(jnp.float32), axis=1), axis=1)

    def run(x):
        for layer in range(DEPTH):
            j = layer // 2
            if layer % 2 == 0:
                x = x + _even_mixer(x, ev_norm[j], ev_w_in[j], ev_na_q_gain[j], ev_na_k_gain[j], ev_na_rpb[j], lb_all[:, j], ev_hg_out_gain[j], ev_w_out[j])
            else:
                x = x + _mla(x, od_norm[j], od_w_in[j], od_q_norm[j], od_kv_norm[j], od_w_uq[j], od_w_ukv[j], od_q_gain[j], od_k_gain[j], od_w_out[j])
            x = x + _peer(_rmsnorm(x, ff_norm[layer]), peer_w_query[layer], peer_sub_keys[layer], peer_u[layer], peer_v[layer])
        return x

    y_prompt = run(x_prompt)
    y_sample = run(x_sample)
    return (y_prompt, y_sample)
```

```python
import functools

import numpy as np
import jax
import jax.numpy as jnp
from jax import lax
from jax.experimental import pallas as pl
from jax.experimental.pallas import tpu as pltpu

F32 = jnp.float32
BF16 = jnp.bfloat16

EPS = 1e-6
LANES = 128
VMEM_LIMIT_BYTES = 56 * 1024 * 1024
NEG_BIG = -1e30

GRID_W = 64
NA_HEADS = 8
NA_HEAD_DIM = 128
NA_WIN_ROWS = 8
NA_WIN_COLS = 16
NA_ROW_BLOCK = 8
NA_KEY_ROWS = 16
HG_HEADS = 8
HG_DIM = 128
HG_CHUNK = 32
MLA_HEADS = 16
MLA_Q_RANK = 768
MLA_KV_RANK = 512
MLA_NOPE = 128
MLA_ROPE = 64
MLA_QK = MLA_NOPE + MLA_ROPE
MLA_HEAD_PAD = 256
ROPE_BASE = 10000.0
PEER_HEADS = 8
PEER_N_KEYS = 128
PEER_TOPK = 16
PEER_HALF = 128


def _cparams(sem):
    return pltpu.CompilerParams(dimension_semantics=sem, vmem_limit_bytes=VMEM_LIMIT_BYTES)


def _rms(x, gain):
    return x * lax.rsqrt(jnp.mean(x * x, axis=-1, keepdims=True) + EPS) * gain


def _dot_nt(a, b):
    return lax.dot_general(a, b, (((1,), (1,)), ((), ())), preferred_element_type=F32)


def _dot_tn(a, b):
    return lax.dot_general(a, b, (((0,), (0,)), ((), ())), preferred_element_type=F32)


def _mm_kernel(*refs, has_norm, has_res):
    it = iter(refs)
    x_ref = next(it)
    g_ref = next(it) if has_norm else None
    w_ref = next(it)
    r_ref = next(it) if has_res else None
    o_ref = next(it)
    if has_norm:
        hn_ref = next(it)

        @pl.when(pl.program_id(1) == 0)
        def _():
            hn_ref[...] = _rms(x_ref[...].astype(F32), g_ref[...]).astype(BF16)

        a = hn_ref[...]
    else:
        a = x_ref[...].astype(BF16)
    acc = jnp.dot(a, w_ref[...], preferred_element_type=F32)
    if has_res:
        acc = acc + r_ref[...]
    o_ref[...] = acc.astype(o_ref.dtype)


def _matmul(x, w, *, gain=None, residual=None, out_dtype=F32, x_col_block=0, tm=512, tn=1024):
    M = x.shape[0]
    K, N = w.shape
    tn = min(tn, N)
    assert M % tm == 0 and N % tn == 0 and (x_col_block + 1) * K <= x.shape[1]
    has_norm, has_res = gain is not None, residual is not None
    in_specs = [pl.BlockSpec((tm, K), lambda i, j: (i, x_col_block))]
    args = [x]
    if has_norm:
        in_specs.append(pl.BlockSpec((1, K), lambda i, j: (0, 0)))
        args.append(gain.reshape(1, K).astype(F32))
    in_specs.append(pl.BlockSpec((K, tn), lambda i, j: (0, j)))
    args.append(w)
    if has_res:
        in_specs.append(pl.BlockSpec((tm, tn), lambda i, j: (i, j)))
        args.append(residual)
    return pl.pallas_call(
        functools.partial(_mm_kernel, has_norm=has_norm, has_res=has_res),
        grid=(M // tm, N // tn),
        in_specs=in_specs,
        out_specs=pl.BlockSpec((tm, tn), lambda i, j: (i, j)),
        out_shape=jax.ShapeDtypeStruct((M, N), out_dtype),
        scratch_shapes=[pltpu.VMEM((tm, K), BF16)] if has_norm else [],
        compiler_params=_cparams(("parallel", "arbitrary")),
        name="mm",
    )(*args)


NA_Q = NA_ROW_BLOCK * GRID_W
NA_KB = 256
NA_NKB = NA_KEY_ROWS * GRID_W // NA_KB


def _na_bias_tables(rpb):
    i = np.arange(NA_ROW_BLOCK)[:, None, None, None]
    c = np.arange(GRID_W)[None, :, None, None]
    j = np.arange(NA_KEY_ROWS)[None, None, :, None]
    kc = np.arange(GRID_W)[None, None, None, :]
    half = NA_WIN_ROWS // 2
    qc0 = np.clip(c - NA_WIN_COLS // 2, 0, GRID_W - NA_WIN_COLS)
    col_ok = (kc >= qc0) & (kc < qc0 + NA_WIN_COLS)
    dcol = np.clip(kc - c + NA_WIN_COLS - 1, 0, 2 * NA_WIN_COLS - 2)
    tabs = []
    for var in range(3):
        if var == 0:
            r0 = np.maximum(i - half, 0)
            dr = j - i
        elif var == 1:
            r0 = i
            dr = j - half - i
        else:
            r0 = np.minimum(i + half, NA_KEY_ROWS - NA_WIN_ROWS)
            dr = j - (NA_KEY_ROWS - NA_ROW_BLOCK) - i
        ok = (j >= r0) & (j < r0 + NA_WIN_ROWS) & col_ok
        dri = np.broadcast_to(np.clip(dr + NA_WIN_ROWS - 1, 0, 2 * NA_WIN_ROWS - 2), ok.shape)
        dci = np.broadcast_to(dcol, ok.shape)
        vals = rpb.astype(F32)[:, dri, dci]
        tabs.append(jnp.where(jnp.asarray(ok)[None], vals, NEG_BIG).reshape(
            NA_HEADS, NA_Q, NA_KEY_ROWS * GRID_W))
    return jnp.stack(tabs)


def _na_steps(groups):
    steps = []
    tok0 = 0
    for (B, S) in groups:
        rows = S // GRID_W
        nrb = rows // NA_ROW_BLOCK
        assert nrb >= 3 and S % NA_Q == 0
        for b in range(B):
            for rb in range(nrb):
                var = 0 if rb == 0 else (2 if rb == nrb - 1 else 1)
                kr0 = min(max(NA_ROW_BLOCK * rb - NA_WIN_ROWS // 2, 0), rows - NA_KEY_ROWS)
                steps.append((var, (tok0 + b * S) // NA_Q + rb, (tok0 + b * S + kr0 * GRID_W) // NA_KB))
        tok0 += B * S
    steps.sort(key=lambda t: t[0])
    arr = np.asarray(steps, np.int32)
    return arr[:, 0], arr[:, 1], arr[:, 2]


def _na_kernel(var_ref, qb_ref, kb_ref, q_ref, *rest):
    k_refs = rest[:NA_NKB]
    v_refs = rest[NA_NKB:2 * NA_NKB]
    bias_ref, qg_ref, kg_ref, o_ref = rest[2 * NA_NKB:]
    scale = NA_HEAD_DIM ** -0.5
    q = _rms(q_ref[...].astype(F32), qg_ref[...]).astype(BF16)
    s = []
    for m in range(NA_NKB):
        k = _rms(k_refs[m][...].astype(F32), kg_ref[...]).astype(BF16)
        s.append(_dot_nt(q, k) * scale + bias_ref[0, 0, :, m * NA_KB:(m + 1) * NA_KB])
    mx = s[0].max(axis=-1, keepdims=True)
    for m in range(1, NA_NKB):
        mx = jnp.maximum(mx, s[m].max(axis=-1, keepdims=True))
    l = jnp.zeros_like(mx)
    acc = jnp.zeros((NA_Q, NA_HEAD_DIM), F32)
    for m in range(NA_NKB):
        p = jnp.exp(s[m] - mx)
        l = l + p.sum(axis=-1, keepdims=True)
        acc = acc + jnp.dot(p.astype(BF16), v_refs[m][...].astype(BF16), preferred_element_type=F32)
    o_ref[...] = (acc / l).astype(o_ref.dtype)


def _neighbourhood_attention(p, groups, rpb, q_gain, k_gain, out_cols):
    T = p.shape[0]
    var, qblk, kblk = _na_steps(groups)
    tabs = _na_bias_tables(rpb)
    H = NA_HEADS

    def qmap(h, n, var, qb, kb):
        return (qb[n], h)

    def kmap(m, off):
        return lambda h, n, var, qb, kb: (kb[n] + m, off + h)

    in_specs = [pl.BlockSpec((NA_Q, NA_HEAD_DIM), qmap)]
    in_specs += [pl.BlockSpec((NA_KB, NA_HEAD_DIM), kmap(m, H)) for m in range(NA_NKB)]
    in_specs += [pl.BlockSpec((NA_KB, NA_HEAD_DIM), kmap(m, 2 * H)) for m in range(NA_NKB)]
    in_specs += [
        pl.BlockSpec((1, 1, NA_Q, NA_KEY_ROWS * GRID_W), lambda h, n, var, qb, kb: (var[n], h, 0, 0)),
        pl.BlockSpec((1, NA_HEAD_DIM), lambda h, n, var, qb, kb: (0, 0)),
        pl.BlockSpec((1, NA_HEAD_DIM), lambda h, n, var, qb, kb: (0, 0)),
    ]
    return pl.pallas_call(
        _na_kernel,
        grid_spec=pltpu.PrefetchScalarGridSpec(
            num_scalar_prefetch=3,
            grid=(H, len(var)),
            in_specs=in_specs,
            out_specs=pl.BlockSpec((NA_Q, NA_HEAD_DIM), qmap),
        ),
        out_shape=jax.ShapeDtypeStruct((T, out_cols), BF16),
        compiler_params=_cparams(("parallel", "arbitrary")),
        name="natten",
    )(jnp.asarray(var), jnp.asarray(qblk), jnp.asarray(kblk), p, *([p] * (2 * NA_NKB)), tabs,
      q_gain.reshape(1, -1).astype(F32), k_gain.reshape(1, -1).astype(F32))


HG_BLOCK = 256
HG_NCHUNK = HG_BLOCK // HG_CHUNK


def _hg_steps(groups, reverse):
    blk, first = [], []
    tok0 = 0
    for (B, S) in groups:
        nb = S // HG_BLOCK
        assert S % HG_BLOCK == 0
        for b in range(B):
            order = range(nb - 1, -1, -1) if reverse else range(nb)
            for n, sb in enumerate(order):
                blk.append((tok0 + b * S) // HG_BLOCK + sb)
                first.append(1 if n == 0 else 0)
        tok0 += B * S
    return np.asarray(blk, np.int32), np.asarray(first, np.int32)


def _hg_kernel(blk_ref, first_ref, q_ref, f_ref, i_ref, lb_ref, tri_ref, *rest, reverse, finish):
    if finish:
        ofw_ref, g_ref, gain_ref, o_ref, st_ref, b_sc, k_sc, v_sc, o_sc = rest
    else:
        o_ref, st_ref, b_sc, k_sc, v_sc, o_sc = rest
    h = pl.program_id(0)
    n = pl.program_id(1)
    C = HG_CHUNK

    @pl.when(first_ref[n] == 1)
    def _():
        st_ref[...] = jnp.zeros_like(st_ref)

    lb = lb_ref[pl.ds(h, 1), :]
    tri = tri_ref[...]
    row = lax.broadcasted_iota(jnp.int32, (C, HG_DIM), 0)
    ones = jnp.ones((HG_DIM, HG_DIM), BF16)

    for cc in range(HG_NCHUNK):
        c = HG_NCHUNK - 1 - cc if reverse else cc
        sl = pl.ds(c * C, C)
        f = lb + (1.0 - lb) * jax.nn.sigmoid(f_ref[sl, :].astype(F32))
        logf = jnp.log(f)
        kk = 1.0 - f
        q = q_ref[sl, :].astype(F32)
        q = q * jax.nn.sigmoid(q)
        v = i_ref[sl, :].astype(F32)
        b = jnp.dot(tri, logf, preferred_element_type=F32, precision=lax.Precision.HIGHEST)
        b_sc[...] = b
        k_sc[...] = kk
        v_sc[...] = v
        st = st_ref[...]
        o_inter = _dot_nt((q * jnp.exp(b)).astype(BF16), st.astype(BF16))
        prods = []
        for s in range(C):
            bs = b_sc[s:s + 1, :]
            keep = (row <= s) if reverse else (row >= s)
            e = jnp.exp(jnp.where(keep, b - bs, -jnp.inf))
            prods.append((q * e * k_sc[s:s + 1, :]).astype(BF16))
        a_bc = jnp.dot(jnp.concatenate(prods, axis=0), ones, preferred_element_type=F32)
        o_intra = jnp.zeros((C, HG_DIM), F32)
        for s in range(C):
            o_intra = o_intra + a_bc[s * C:(s + 1) * C, :] * v_sc[s:s + 1, :]
        o_sc[sl, :] = o_inter + o_intra
        b_end = b_sc[0:1, :] if reverse else b_sc[C - 1:C, :]
        ks = (kk * jnp.exp(b_end - b)).astype(BF16)
        st_ref[...] = st * jnp.exp(b_end) + _dot_tn(v.astype(BF16), ks)

    if finish:
        o = o_sc[...] + ofw_ref[...]
        g = g_ref[...].astype(F32)
        o_ref[...] = (_rms(o, gain_ref[...]) * (g * jax.nn.sigmoid(g))).astype(o_ref.dtype)
    else:
        o_ref[...] = o_sc[...]


def _hgrn2(p, groups, lb, *, reverse, f_col, o_fw=None, out_gain=None, mix=None):
    T = p.shape[0]
    blk, first = _hg_steps(groups, reverse)
    H = HG_HEADS
    finish = o_fw is not None
    C = HG_CHUNK
    tri = np.triu(np.ones((C, C), np.float32)) if reverse else np.tril(np.ones((C, C), np.float32))

    def col(off):
        return lambda h, n, blk, first: (blk[n], off + h)

    const = lambda h, n, blk, first: (0, 0)
    in_specs = [
        pl.BlockSpec((HG_BLOCK, HG_DIM), col(3 * NA_HEADS)),
        pl.BlockSpec((HG_BLOCK, HG_DIM), col(f_col)),
        pl.BlockSpec((HG_BLOCK, HG_DIM), col(6 * NA_HEADS)),
        pl.BlockSpec((H, HG_DIM), const),
        pl.BlockSpec((C, C), const),
    ]
    args = [p, p, p, lb.reshape(H, HG_DIM).astype(F32), jnp.asarray(tri)]
    aliases = {}
    if finish:
        in_specs += [
            pl.BlockSpec((HG_BLOCK, HG_DIM), col(0)),
            pl.BlockSpec((HG_BLOCK, HG_DIM), col(7 * NA_HEADS)),
            pl.BlockSpec((1, HG_DIM), const),
            pl.BlockSpec(memory_space=pl.ANY),
        ]
        args += [o_fw, p, out_gain.reshape(1, HG_DIM).astype(F32), mix]
        aliases = {2 + len(args) - 1: 0}
        out_spec = pl.BlockSpec((HG_BLOCK, HG_DIM), col(NA_HEADS))
        out_shape = jax.ShapeDtypeStruct(mix.shape, mix.dtype)
    else:
        out_spec = pl.BlockSpec((HG_BLOCK, HG_DIM), col(0))
        out_shape = jax.ShapeDtypeStruct((T, H * HG_DIM), F32)

    def body(blk_ref, first_ref, *refs):
        if finish:
            refs = refs[:8] + refs[9:]
        _hg_kernel(blk_ref, first_ref, *refs, reverse=reverse, finish=finish)

    return pl.pallas_call(
        body,
        grid_spec=pltpu.PrefetchScalarGridSpec(
            num_scalar_prefetch=2,
            grid=(H, len(blk)),
            in_specs=in_specs,
            out_specs=out_spec,
            scratch_shapes=[
                pltpu.VMEM((HG_DIM, HG_DIM), F32),
                pltpu.VMEM((C, HG_DIM), F32),
                pltpu.VMEM((C, HG_DIM), F32),
                pltpu.VMEM((C, HG_DIM), F32),
                pltpu.VMEM((HG_BLOCK, HG_DIM), F32),
            ],
        ),
        out_shape=out_shape,
        input_output_aliases=aliases,
        compiler_params=_cparams(("parallel", "arbitrary")),
        name="hgrn2_bw" if reverse else "hgrn2_fw",
    )(jnp.asarray(blk), jnp.asarray(first), *args)


def _rope_tables(groups):
    half = MLA_ROPE // 2
    smax = max(S for _, S in groups)
    pos = jnp.arange(smax, dtype=F32)
    inv = 1.0 / (ROPE_BASE ** (jnp.arange(0, MLA_ROPE, 2, dtype=F32) / MLA_ROPE))
    ang = pos[:, None] * inv[None, :]
    cos, sin = jnp.cos(ang), jnp.sin(ang)
    z = jnp.zeros((smax, LANES - 2 * half), F32)
    zh = jnp.zeros((smax, half), F32)
    c = jnp.concatenate([cos, cos, z], axis=1)
    s_from_x2 = jnp.concatenate([-sin, zh, z], axis=1)
    s_from_x1 = jnp.concatenate([zh, sin, z], axis=1)
    return c, s_from_x2, s_from_x1


def _rope_block(t, c, s2, s1):
    half = MLA_ROPE // 2
    return t * c + pltpu.roll(t, LANES - half, 1) * s2 + pltpu.roll(t, half, 1) * s1


def _qprep_kernel(q_ref, g_ref, c_ref, s2_ref, s1_ref, o_ref, *, scale):
    c, s2, s1 = c_ref[...], s2_ref[...], s1_ref[...]
    for h in range(MLA_HEADS):
        a = q_ref[:, h * MLA_HEAD_PAD:h * MLA_HEAD_PAD + LANES].astype(F32)
        r = q_ref[:, h * MLA_HEAD_PAD + LANES:(h + 1) * MLA_HEAD_PAD].astype(F32)
        ms = (jnp.sum(a * a, axis=-1, keepdims=True) + jnp.sum(r * r, axis=-1, keepdims=True)) / MLA_QK
        inv = lax.rsqrt(ms + EPS)
        o_ref[:, h * MLA_HEAD_PAD:h * MLA_HEAD_PAD + LANES] = (a * inv * g_ref[:, :LANES] * scale).astype(o_ref.dtype)
        rn = r * inv * g_ref[:, LANES:]
        o_ref[:, h * MLA_HEAD_PAD + LANES:(h + 1) * MLA_HEAD_PAD] = (_rope_block(rn, c, s2, s1) * scale).astype(o_ref.dtype)


def _kprep_kernel(kv_ref, kr_ref, g_ref, c_ref, s2_ref, s1_ref, o_ref):
    c, s2, s1 = c_ref[...], s2_ref[...], s1_ref[...]
    r = kr_ref[...].astype(F32)
    rs = jnp.sum(r * r, axis=-1, keepdims=True)
    for h in range(MLA_HEADS):
        a = kv_ref[:, h * MLA_HEAD_PAD:h * MLA_HEAD_PAD + LANES].astype(F32)
        inv = lax.rsqrt((jnp.sum(a * a, axis=-1, keepdims=True) + rs) / MLA_QK + EPS)
        o_ref[:, h * MLA_HEAD_PAD:h * MLA_HEAD_PAD + LANES] = (a * inv * g_ref[:, :LANES]).astype(o_ref.dtype)
        rn = r * inv * g_ref[:, LANES:]
        o_ref[:, h * MLA_HEAD_PAD + LANES:(h + 1) * MLA_HEAD_PAD] = _rope_block(rn, c, s2, s1).astype(o_ref.dtype)


def _pos_block_map(groups, tm):
    bounds = []
    tok0 = 0
    for (B, S) in groups:
        assert S % tm == 0
        bounds.append((tok0 // tm, S // tm))
        tok0 += B * S

    def imap(i):
        out = (i - bounds[0][0]) % bounds[0][1]
        for (b0, per) in bounds[1:]:
            out = jnp.where(i >= b0, (i - b0) % per, out)
        return (out, 0)

    return imap


def _pad_gain(g):
    return jnp.concatenate([g.astype(F32), jnp.zeros((MLA_HEAD_PAD - MLA_QK,), F32)]).reshape(1, MLA_HEAD_PAD)


def _mla_prep(q, kv, cc, kr_block, groups, q_gain, k_gain, tm=256):
    T = q.shape[0]
    W = MLA_HEADS * MLA_HEAD_PAD
    tabs = _rope_tables(groups)
    pmap = _pos_block_map(groups, tm)
    tab_specs = [pl.BlockSpec((tm, LANES), pmap)] * 3
    row = pl.BlockSpec((tm, W), lambda i: (i, 0))
    gspec = pl.BlockSpec((1, MLA_HEAD_PAD), lambda i: (0, 0))
    qf = pl.pallas_call(
        functools.partial(_qprep_kernel, scale=MLA_QK ** -0.5),
        grid=(T // tm,),
        in_specs=[row, gspec] + tab_specs,
        out_specs=row,
        out_shape=jax.ShapeDtypeStruct((T, W), BF16),
        compiler_params=_cparams(("parallel",)),
        name="mla_qprep",
    )(q, _pad_gain(q_gain), *tabs)
    kf = pl.pallas_call(
        _kprep_kernel,
        grid=(T // tm,),
        in_specs=[row, pl.BlockSpec((tm, LANES), lambda i: (i, kr_block)), gspec] + tab_specs,
        out_specs=row,
        out_shape=jax.ShapeDtypeStruct((T, W), BF16),
        compiler_params=_cparams(("parallel",)),
        name="mla_kprep",
    )(kv, cc, _pad_gain(k_gain), *tabs)
    return qf, kf


def _flash_kernel(q_ref, k_ref, v_ref, *rest, aliased):
    o_ref, m_sc, l_sc, acc_sc = rest[1:] if aliased else rest
    ki = pl.program_id(3)

    @pl.when(ki == 0)
    def _():
        m_sc[...] = jnp.full_like(m_sc, -jnp.inf)
        l_sc[...] = jnp.zeros_like(l_sc)
        acc_sc[...] = jnp.zeros_like(acc_sc)

    s = _dot_nt(q_ref[...], k_ref[...])
    m_prev = m_sc[...]
    m_new = jnp.maximum(m_prev, s.max(axis=-1, keepdims=True))
    alpha = jnp.exp(m_prev - m_new)
    p = jnp.exp(s - m_new)
    l_sc[...] = alpha * l_sc[...] + p.sum(axis=-1, keepdims=True)
    acc_sc[...] = alpha * acc_sc[...] + jnp.dot(p.astype(BF16), v_ref[...], preferred_element_type=F32)
    m_sc[...] = m_new

    @pl.when(ki == pl.num_programs(3) - 1)
    def _():
        o_ref[...] = (acc_sc[...] / l_sc[...]).astype(o_ref.dtype)


def _mla_attention(qf, kf, kv, groups, tq=512, tk=512):
    T = qf.shape[0]
    H = MLA_HEADS
    out = None
    tok0 = 0
    for (B, S) in groups:
        assert S % tq == 0 and S % tk == 0 and tok0 % tq == 0 and tok0 % tk == 0
        nq, nk = S // tq, S // tk
        q0, k0 = tok0 // tq, tok0 // tk
        in_specs = [
            pl.BlockSpec((tq, MLA_HEAD_PAD), lambda b, h, qi, ki, q0=q0, nq=nq: (q0 + b * nq + qi, h)),
            pl.BlockSpec((tk, MLA_HEAD_PAD), lambda b, h, qi, ki, k0=k0, nk=nk: (k0 + b * nk + ki, h)),
            pl.BlockSpec((tk, LANES), lambda b, h, qi, ki, k0=k0, nk=nk: (k0 + b * nk + ki, 2 * h + 1)),
        ]
        args = [qf, kf, kv]
        aliases = {}
        if out is not None:
            in_specs.append(pl.BlockSpec(memory_space=pl.ANY))
            args.append(out)
            aliases = {3: 0}
        out = pl.pallas_call(
            functools.partial(_flash_kernel, aliased=out is not None),
            grid=(B, H, nq, nk),
            in_specs=in_specs,
            out_specs=pl.BlockSpec((tq, LANES), lambda b, h, qi, ki, q0=q0, nq=nq: (q0 + b * nq + qi, h)),
            out_shape=jax.ShapeDtypeStruct((T, H * LANES), BF16),
            scratch_shapes=[pltpu.VMEM((tq, 1), F32), pltpu.VMEM((tq, 1), F32), pltpu.VMEM((tq, LANES), F32)],
            input_output_aliases=aliases,
            compiler_params=_cparams(("parallel", "parallel", "parallel", "arbitrary")),
            name="mla_flash",
        )(*args)
        tok0 += B * S
    return out


def _peer_candidates():
    a, b = [], []
    a += [0] * 16; b += list(range(16))
    for aa in (1, 2, 3):
        a += [aa] * 8; b += list(range(8))
    for bb in (0, 1):
        a += [4, 5, 6, 7]; b += [bb] * 4
    a += list(range(8, 16)); b += [0] * 8
    for bb in (2, 3):
        a += [4, 5, 6, 7]; b += [bb] * 4
    return np.asarray(a), np.asarray(b)


_CAND_A, _CAND_B = _peer_candidates()
N_CAND = len(_CAND_A)


def _top16_sublane(s):
    n = s.shape[0]
    idx = lax.broadcasted_iota(jnp.int32, s.shape, 0).astype(F32)
    rank = jnp.full(s.shape, float(PEER_TOPK), F32)
    val = s
    tops = []
    for r in range(PEER_TOPK):
        m = val.max(axis=0, keepdims=True)
        first = jnp.where(val == m, idx, float(n)).min(axis=0, keepdims=True)
        sel = idx == first
        rank = jnp.where(sel, float(r), rank)
        val = jnp.where(sel, -jnp.inf, val)
        tops.append(m)
    return jnp.concatenate(tops, axis=0), rank


def _peer_route_kernel(x_ref, g_ref, wq_ref, keys_ref, pos_ref, amat_ref,
                       hn_ref, cnt_ref, e1_ref, rb_ref, e2_ref):
    tm = x_ref.shape[0]
    hn = _rms(x_ref[...].astype(F32), g_ref[...]).astype(BF16)
    hn_ref[...] = hn
    qry = jnp.dot(hn, wq_ref[...], preferred_element_type=F32)
    pos = pos_ref[...]
    for h in range(PEER_HEADS):
        sc, tops, ranks = [], [], []
        for half in range(2):
            c0 = (2 * h + half) * PEER_HALF
            s = _dot_nt(keys_ref[h, half], qry[:, c0:c0 + PEER_HALF].astype(BF16))
            t, r = _top16_sublane(s)
            sc.append(s); tops.append(t); ranks.append(r)
        t1, t2 = tops
        mid1 = jnp.concatenate([t1[4:8], t1[4:8]], axis=0)
        rep2 = lambda b: jnp.broadcast_to(t2[b:b + 1], (4, tm))
        cand = jnp.concatenate([
            t1[0:1] + t2, t1[1:2] + t2[0:8], t1[2:3] + t2[0:8], t1[3:4] + t2[0:8],
            mid1 + jnp.concatenate([rep2(0), rep2(1)], axis=0),
            t1[8:16] + t2[0:1],
            mid1 + jnp.concatenate([rep2(2), rep2(3)], axis=0)], axis=0)
        best = cand[0:1]
        val = cand
        chosen = jnp.zeros(cand.shape, F32)
        for _ in range(PEER_TOPK):
            m = val.max(axis=0, keepdims=True)
            first = jnp.where(val == m, pos, 1e9).min(axis=0, keepdims=True)
            sel = pos == first
            chosen = jnp.where(sel, 1.0, chosen)
            val = jnp.where(sel, -jnp.inf, val)
        z = jnp.sum(chosen * jnp.exp(cand - best), axis=0, keepdims=True)
        per_a = jnp.dot(amat_ref[...], chosen.astype(BF16), preferred_element_type=F32)
        cnt = jnp.zeros(sc[0].shape, F32)
        for a in range(PEER_TOPK):
            cnt = jnp.where(ranks[0] == float(a), per_a[a:a + 1], cnt)
        cnt_ref[h] = cnt
        e1_ref[h] = jnp.exp(sc[0] - t1[0:1]) / z
        rb_ref[h] = ranks[1]
        e2_ref[h] = jnp.exp(sc[1] - t2[0:1])


def _peer_route(x, norm_gain, w_query, sub_keys, tm=256):
    T, D = x.shape
    H = PEER_HEADS
    pos = np.broadcast_to((_CAND_A * PEER_TOPK + _CAND_B).astype(np.float32)[:, None], (N_CAND, tm))
    amat = (np.arange(PEER_TOPK)[:, None] == _CAND_A[None, :]).astype(np.float32)
    gate_shape = jax.ShapeDtypeStruct((H, PEER_N_KEYS, T), F32)
    gate_spec = pl.BlockSpec((H, PEER_N_KEYS, tm), lambda i: (0, 0, i))
    return pl.pallas_call(
        _peer_route_kernel,
        grid=(T // tm,),
        in_specs=[
            pl.BlockSpec((tm, D), lambda i: (i, 0)),
            pl.BlockSpec((1, D), lambda i: (0, 0)),
            pl.BlockSpec(w_query.shape, lambda i: (0, 0)),
            pl.BlockSpec(sub_keys.shape, lambda i: (0, 0, 0, 0)),
            pl.BlockSpec((N_CAND, tm), lambda i: (0, 0)),
            pl.BlockSpec((PEER_TOPK, N_CAND), lambda i: (0, 0)),
        ],
        out_specs=[pl.BlockSpec((tm, D), lambda i: (i, 0))] + [gate_spec] * 4,
        out_shape=[jax.ShapeDtypeStruct((T, D), BF16)] + [gate_shape] * 4,
        compiler_params=_cparams(("parallel",)),
        name="peer_route",
    )(x, norm_gain.reshape(1, D).astype(F32), w_query, sub_keys, jnp.asarray(pos), jnp.asarray(amat, BF16))


def _gelu(a):
    return 0.5 * a * (1.0 + lax.erf(a * (2.0 ** -0.5)))


def _peer_expert_kernel(hn_ref, u_ref, vt_ref, cnt_ref, e1_ref, rb_ref, e2_ref, x_ref, o_ref, acc_ref, g_ref,
                        *, rows_per_step):
    e = pl.program_id(1)

    @pl.when(e == 0)
    def _():
        acc_ref[...] = jnp.zeros_like(acc_ref)

    act = _gelu(_dot_nt(u_ref[...], hn_ref[...]))
    for il in range(rows_per_step):
        i = e * rows_per_step + il
        w = jnp.zeros((PEER_N_KEYS, act.shape[1]), F32)
        for h in range(PEER_HEADS):
            cnt = cnt_ref[h, pl.ds(i, 1), :]
            e1 = e1_ref[h, pl.ds(i, 1), :]
            w = w + jnp.where(rb_ref[h] < cnt, e2_ref[h], 0.0) * e1
        g_ref[il * PEER_N_KEYS:(il + 1) * PEER_N_KEYS, :] = (
            w * act[il * PEER_N_KEYS:(il + 1) * PEER_N_KEYS, :]).astype(BF16)
    acc_ref[...] += jnp.dot(vt_ref[...], g_ref[...], preferred_element_type=F32)

    @pl.when(e == pl.num_programs(1) - 1)
    def _():
        o_ref[...] = x_ref[...] + acc_ref[...].T


def _peer_experts(x, hn, gates, u, vt, tm=512, te=512):
    T, D = x.shape
    NE = u.shape[0]
    H = PEER_HEADS
    gate_spec = pl.BlockSpec((H, PEER_N_KEYS, tm), lambda i, e: (0, 0, i))
    return pl.pallas_call(
        functools.partial(_peer_expert_kernel, rows_per_step=te // PEER_N_KEYS),
        grid=(T // tm, NE // te),
        in_specs=[
            pl.BlockSpec((tm, D), lambda i, e: (i, 0)),
            pl.BlockSpec((te, D), lambda i, e: (e, 0)),
            pl.BlockSpec((D, te), lambda i, e: (0, e)),
            gate_spec, gate_spec, gate_spec, gate_spec,
            pl.BlockSpec((tm, D), lambda i, e: (i, 0)),
        ],
        out_specs=pl.BlockSpec((tm, D), lambda i, e: (i, 0)),
        out_shape=jax.ShapeDtypeStruct((T, D), F32),
        scratch_shapes=[pltpu.VMEM((D, tm), F32), pltpu.VMEM((te, tm), BF16)],
        compiler_params=_cparams(("parallel", "arbitrary")),
        name="peer_experts",
    )(hn, u, vt, *gates, x)


def _peer(x, norm_gain, w_query, sub_keys, u, v):
    hn, cnt, e1, rb, e2 = _peer_route(x, norm_gain, w_query.astype(BF16), sub_keys.astype(BF16))
    return _peer_experts(x, hn, (cnt, e1, rb, e2), u.astype(BF16), v.astype(BF16).T)


def _even_layer(x, groups, norm, w_in, na_q_gain, na_k_gain, na_rpb, lb, hg_out_gain, w_out):
    p = _matmul(x, w_in.astype(BF16), gain=norm, out_dtype=F32)
    mix_cols = NA_HEADS * NA_HEAD_DIM + HG_HEADS * HG_DIM
    mix = _neighbourhood_attention(p, groups, na_rpb, na_q_gain, na_k_gain, mix_cols)
    o_fw = _hgrn2(p, groups, lb[0], reverse=False, f_col=4 * NA_HEADS)
    mix = _hgrn2(p, groups, lb[1], reverse=True, f_col=5 * NA_HEADS, o_fw=o_fw, out_gain=hg_out_gain, mix=mix)
    return _matmul(mix, w_out.astype(BF16), residual=x, out_dtype=F32)


def _odd_layer(x, groups, norm, w_in, q_norm, kv_norm, w_uq, w_ukv, q_gain, k_gain, w_out):
    D = x.shape[1]
    pad_q = (-MLA_Q_RANK) % MLA_KV_RANK
    w_cq, w_ckv, w_kr = jnp.split(w_in, [MLA_Q_RANK, MLA_Q_RANK + MLA_KV_RANK], axis=1)
    w_in_p = jnp.concatenate(
        [w_cq, jnp.zeros((D, pad_q), w_in.dtype), w_ckv, w_kr, jnp.zeros((D, LANES - MLA_ROPE), w_in.dtype)],
        axis=1).astype(BF16)
    cc = _matmul(x, w_in_p, gain=norm, out_dtype=F32, tn=w_in_p.shape[1])
    w_uq_p = jnp.pad(w_uq.reshape(MLA_Q_RANK, MLA_HEADS, MLA_QK),
                     ((0, 0), (0, 0), (0, MLA_HEAD_PAD - MLA_QK))).reshape(MLA_Q_RANK, -1).astype(BF16)
    q = _matmul(cc, w_uq_p, gain=q_norm, out_dtype=BF16, x_col_block=0)
    kv = _matmul(cc, w_ukv.astype(BF16), gain=kv_norm, out_dtype=BF16,
                 x_col_block=(MLA_Q_RANK + pad_q) // MLA_KV_RANK)
    kr_block = (MLA_Q_RANK + pad_q + MLA_KV_RANK) // LANES
    qf, kf = _mla_prep(q, kv, cc, kr_block, groups, q_gain, k_gain)
    o = _mla_attention(qf, kf, kv, groups)
    return _matmul(o, w_out.astype(BF16), residual=x, out_dtype=F32)


def _forward(xs, ev_norm, ev_w_in, ev_na_q_gain, ev_na_k_gain, ev_na_rpb, ev_hg_lb_logits, ev_hg_out_gain,
             ev_w_out, od_norm, od_w_in, od_q_norm, od_kv_norm, od_w_uq, od_w_ukv, od_q_gain, od_k_gain,
             od_w_out, ff_norm, peer_w_query, peer_sub_keys, peer_u, peer_v):
    groups = [(int(x.shape[0]), int(x.shape[1])) for x in xs]
    D = xs[0].shape[-1]
    depth = ff_norm.shape[0]
    lb_all = jnp.cumsum(jax.nn.softmax(ev_hg_lb_logits.astype(F32), axis=1), axis=1)
    x = jnp.concatenate([t.reshape(-1, D) for t in xs], axis=0)
    for layer in range(depth):
        j = layer // 2
        if layer % 2 == 0:
            x = _even_layer(x, groups, ev_norm[j], ev_w_in[j], ev_na_q_gain[j], ev_na_k_gain[j], ev_na_rpb[j],
                            lb_all[:, j], ev_hg_out_gain[j], ev_w_out[j])
        else:
            x = _odd_layer(x, groups, od_norm[j], od_w_in[j], od_q_norm[j], od_kv_norm[j], od_w_uq[j],
                           od_w_ukv[j], od_q_gain[j], od_k_gain[j], od_w_out[j])
        x = _peer(x, ff_norm[layer], peer_w_query[layer], peer_sub_keys[layer], peer_u[layer], peer_v[layer])
    outs = []
    tok0 = 0
    for (B, S) in groups:
        outs.append(x[tok0:tok0 + B * S].reshape(B, S, D))
        tok0 += B * S
    return tuple(outs)


def kernel(x_prompt, x_sample, ev_norm, ev_w_in, ev_na_q_gain, ev_na_k_gain, ev_na_rpb, ev_hg_lb_logits, ev_hg_out_gain, ev_w_out, od_norm, od_w_in, od_q_norm, od_kv_norm, od_w_uq, od_w_ukv, od_q_gain, od_k_gain, od_w_out, ff_norm, peer_w_query, peer_sub_keys, peer_u, peer_v):
    return _forward((x_prompt, x_sample), ev_norm, ev_w_in, ev_na_q_gain, ev_na_k_gain, ev_na_rpb, ev_hg_lb_logits,
                    ev_hg_out_gain, ev_w_out, od_norm, od_w_in, od_q_norm, od_kv_norm, od_w_uq, od_w_ukv, od_q_gain,
                    od_k_gain, od_w_out, ff_norm, peer_w_query, peer_sub_keys, peer_u, peer_v)
```

```python
import functools

import numpy as np
import jax
import jax.numpy as jnp
from jax import lax
from jax.experimental import pallas as pl
from jax.experimental.pallas import tpu as pltpu

F32 = jnp.float32
BF16 = jnp.bfloat16

EPS = 1e-6
LANES = 128
VMEM_LIMIT_BYTES = 56 * 1024 * 1024
NEG_BIG = -1e30

GRID_W = 64
NA_HEADS = 8
NA_HEAD_DIM = 128
NA_WIN_ROWS = 8
NA_WIN_COLS = 16
NA_ROW_BLOCK = 8
NA_KEY_ROWS = 16
HG_HEADS = 8
HG_DIM = 128
HG_CHUNK = 32
MLA_HEADS = 16
MLA_Q_RANK = 768
MLA_KV_RANK = 512
MLA_NOPE = 128
MLA_ROPE = 64
MLA_QK = MLA_NOPE + MLA_ROPE
MLA_HEAD_PAD = 256
ROPE_BASE = 10000.0
PEER_HEADS = 8
PEER_N_KEYS = 128
PEER_TOPK = 16
PEER_HALF = 128


def _cparams(sem):
    return pltpu.CompilerParams(dimension_semantics=sem, vmem_limit_bytes=VMEM_LIMIT_BYTES)


def _rms(x, gain):
    return x * lax.rsqrt(jnp.mean(x * x, axis=-1, keepdims=True) + EPS) * gain


def _dot_nt(a, b):
    return lax.dot_general(a, b, (((1,), (1,)), ((), ())), preferred_element_type=F32)


def _dot_tn(a, b):
    return lax.dot_general(a, b, (((0,), (0,)), ((), ())), preferred_element_type=F32)


def _mm_kernel(*refs, has_norm, has_res):
    it = iter(refs)
    x_ref = next(it)
    g_ref = next(it) if has_norm else None
    w_ref = next(it)
    r_ref = next(it) if has_res else None
    o_ref = next(it)
    if has_norm:
        hn_ref = next(it)

        @pl.when(pl.program_id(1) == 0)
        def _():
            hn_ref[...] = _rms(x_ref[...].astype(F32), g_ref[...]).astype(BF16)

        a = hn_ref[...]
    else:
        a = x_ref[...].astype(BF16)
    acc = jnp.dot(a, w_ref[...], preferred_element_type=F32)
    if has_res:
        acc = acc + r_ref[...]
    o_ref[...] = acc.astype(o_ref.dtype)


def _matmul(x, w, *, gain=None, residual=None, out_dtype=F32, x_col_block=0, tm=512, tn=1024):
    M = x.shape[0]
    K, N = w.shape
    tn = min(tn, N)
    assert M % tm == 0 and N % tn == 0 and (x_col_block + 1) * K <= x.shape[1]
    has_norm, has_res = gain is not None, residual is not None
    in_specs = [pl.BlockSpec((tm, K), lambda i, j: (i, x_col_block))]
    args = [x]
    if has_norm:
        in_specs.append(pl.BlockSpec((1, K), lambda i, j: (0, 0)))
        args.append(gain.reshape(1, K).astype(F32))
    in_specs.append(pl.BlockSpec((K, tn), lambda i, j: (0, j)))
    args.append(w)
    if has_res:
        in_specs.append(pl.BlockSpec((tm, tn), lambda i, j: (i, j)))
        args.append(residual)
    return pl.pallas_call(
        functools.partial(_mm_kernel, has_norm=has_norm, has_res=has_res),
        grid=(M // tm, N // tn),
        in_specs=in_specs,
        out_specs=pl.BlockSpec((tm, tn), lambda i, j: (i, j)),
        out_shape=jax.ShapeDtypeStruct((M, N), out_dtype),
        scratch_shapes=[pltpu.VMEM((tm, K), BF16)] if has_norm else [],
        compiler_params=_cparams(("parallel", "arbitrary")),
        name="mm",
    )(*args)


NA_Q = NA_ROW_BLOCK * GRID_W
NA_KB = 256
NA_NKB = NA_KEY_ROWS * GRID_W // NA_KB


def _na_bias_tables(rpb):
    i = np.arange(NA_ROW_BLOCK)[:, None, None, None]
    c = np.arange(GRID_W)[None, :, None, None]
    j = np.arange(NA_KEY_ROWS)[None, None, :, None]
    kc = np.arange(GRID_W)[None, None, None, :]
    half = NA_WIN_ROWS // 2
    qc0 = np.clip(c - NA_WIN_COLS // 2, 0, GRID_W - NA_WIN_COLS)
    col_ok = (kc >= qc0) & (kc < qc0 + NA_WIN_COLS)
    dcol = np.clip(kc - c + NA_WIN_COLS - 1, 0, 2 * NA_WIN_COLS - 2)
    col_hot = (dcol[0, :, 0, :, None] == np.arange(2 * NA_WIN_COLS - 1)).astype(np.float32)
    by_col = jnp.einsum('hab,ckb->hack', rpb.astype(F32), col_hot, precision=lax.Precision.HIGHEST)
    tabs = []
    for var in range(3):
        if var == 0:
            r0 = np.maximum(i - half, 0)
            dr = j - i
        elif var == 1:
            r0 = i
            dr = j - half - i
        else:
            r0 = np.minimum(i + half, NA_KEY_ROWS - NA_WIN_ROWS)
            dr = j - (NA_KEY_ROWS - NA_ROW_BLOCK) - i
        ok = (j >= r0) & (j < r0 + NA_WIN_ROWS) & col_ok
        dri = np.clip(dr + NA_WIN_ROWS - 1, 0, 2 * NA_WIN_ROWS - 2)[:, 0, :, 0]
        row_hot = (dri[:, :, None] == np.arange(2 * NA_WIN_ROWS - 1)).astype(np.float32)
        vals = jnp.einsum('ija,hack->hicjk', row_hot, by_col, precision=lax.Precision.HIGHEST)
        tabs.append(jnp.where(jnp.asarray(ok)[None], vals, NEG_BIG).reshape(
            NA_HEADS, NA_Q, NA_KEY_ROWS * GRID_W))
    return jnp.stack(tabs)


def _na_steps(groups):
    steps = []
    tok0 = 0
    for (B, S) in groups:
        rows = S // GRID_W
        nrb = rows // NA_ROW_BLOCK
        assert nrb >= 3 and S % NA_Q == 0
        for b in range(B):
            for rb in range(nrb):
                var = 0 if rb == 0 else (2 if rb == nrb - 1 else 1)
                kr0 = min(max(NA_ROW_BLOCK * rb - NA_WIN_ROWS // 2, 0), rows - NA_KEY_ROWS)
                steps.append((var, (tok0 + b * S) // NA_Q + rb, (tok0 + b * S + kr0 * GRID_W) // NA_KB))
        tok0 += B * S
    steps.sort(key=lambda t: t[0])
    arr = np.asarray(steps, np.int32)
    return arr[:, 0], arr[:, 1], arr[:, 2]


def _na_kernel(var_ref, qb_ref, kb_ref, q_ref, *rest):
    k_refs = rest[:NA_NKB]
    v_refs = rest[NA_NKB:2 * NA_NKB]
    bias_ref, qg_ref, kg_ref, o_ref = rest[2 * NA_NKB:]
    scale = NA_HEAD_DIM ** -0.5
    q = _rms(q_ref[...].astype(F32), qg_ref[...]).astype(BF16)
    s = []
    for m in range(NA_NKB):
        k = _rms(k_refs[m][...].astype(F32), kg_ref[...]).astype(BF16)
        s.append(_dot_nt(q, k) * scale + bias_ref[0, 0, :, m * NA_KB:(m + 1) * NA_KB])
    mx = s[0].max(axis=-1, keepdims=True)
    for m in range(1, NA_NKB):
        mx = jnp.maximum(mx, s[m].max(axis=-1, keepdims=True))
    l = jnp.zeros_like(mx)
    acc = jnp.zeros((NA_Q, NA_HEAD_DIM), F32)
    for m in range(NA_NKB):
        p = jnp.exp(s[m] - mx)
        l = l + p.sum(axis=-1, keepdims=True)
        acc = acc + jnp.dot(p.astype(BF16), v_refs[m][...].astype(BF16), preferred_element_type=F32)
    o_ref[...] = (acc / l).astype(o_ref.dtype)


def _neighbourhood_attention(p, groups, rpb, q_gain, k_gain, out_cols):
    T = p.shape[0]
    var, qblk, kblk = _na_steps(groups)
    tabs = _na_bias_tables(rpb)
    H = NA_HEADS

    def qmap(h, n, var, qb, kb):
        return (qb[n], h)

    def kmap(m, off):
        return lambda h, n, var, qb, kb: (kb[n] + m, off + h)

    in_specs = [pl.BlockSpec((NA_Q, NA_HEAD_DIM), qmap)]
    in_specs += [pl.BlockSpec((NA_KB, NA_HEAD_DIM), kmap(m, H)) for m in range(NA_NKB)]
    in_specs += [pl.BlockSpec((NA_KB, NA_HEAD_DIM), kmap(m, 2 * H)) for m in range(NA_NKB)]
    in_specs += [
        pl.BlockSpec((1, 1, NA_Q, NA_KEY_ROWS * GRID_W), lambda h, n, var, qb, kb: (var[n], h, 0, 0)),
        pl.BlockSpec((1, NA_HEAD_DIM), lambda h, n, var, qb, kb: (0, 0)),
        pl.BlockSpec((1, NA_HEAD_DIM), lambda h, n, var, qb, kb: (0, 0)),
    ]
    return pl.pallas_call(
        _na_kernel,
        grid_spec=pltpu.PrefetchScalarGridSpec(
            num_scalar_prefetch=3,
            grid=(H, len(var)),
            in_specs=in_specs,
            out_specs=pl.BlockSpec((NA_Q, NA_HEAD_DIM), qmap),
        ),
        out_shape=jax.ShapeDtypeStruct((T, out_cols), BF16),
        compiler_params=_cparams(("parallel", "arbitrary")),
        name="natten",
    )(jnp.asarray(var), jnp.asarray(qblk), jnp.asarray(kblk), p, *([p] * (2 * NA_NKB)), tabs,
      q_gain.reshape(1, -1).astype(F32), k_gain.reshape(1, -1).astype(F32))


HG_BLOCK = 256
HG_NCHUNK = HG_BLOCK // HG_CHUNK


def _hg_steps(groups, reverse):
    blk, first = [], []
    tok0 = 0
    for (B, S) in groups:
        nb = S // HG_BLOCK
        assert S % HG_BLOCK == 0
        for b in range(B):
            order = range(nb - 1, -1, -1) if reverse else range(nb)
            for n, sb in enumerate(order):
                blk.append((tok0 + b * S) // HG_BLOCK + sb)
                first.append(1 if n == 0 else 0)
        tok0 += B * S
    return np.asarray(blk, np.int32), np.asarray(first, np.int32)


def _hg_kernel(blk_ref, first_ref, q_ref, f_ref, i_ref, lb_ref, tri_ref, *rest, reverse, finish):
    if finish:
        ofw_ref, g_ref, gain_ref, o_ref, st_ref, b_sc, k_sc, v_sc, o_sc = rest
    else:
        o_ref, st_ref, b_sc, k_sc, v_sc, o_sc = rest
    h = pl.program_id(0)
    n = pl.program_id(1)
    C = HG_CHUNK

    @pl.when(first_ref[n] == 1)
    def _():
        st_ref[...] = jnp.zeros_like(st_ref)

    lb = lb_ref[pl.ds(h, 1), :]
    tri = tri_ref[...]
    row = lax.broadcasted_iota(jnp.int32, (C, HG_DIM), 0)
    ones = jnp.ones((HG_DIM, HG_DIM), BF16)

    for cc in range(HG_NCHUNK):
        c = HG_NCHUNK - 1 - cc if reverse else cc
        sl = pl.ds(c * C, C)
        f = lb + (1.0 - lb) * jax.nn.sigmoid(f_ref[sl, :].astype(F32))
        logf = jnp.log(f)
        kk = 1.0 - f
        q = q_ref[sl, :].astype(F32)
        q = q * jax.nn.sigmoid(q)
        v = i_ref[sl, :].astype(F32)
        b = jnp.dot(tri, logf, preferred_element_type=F32, precision=lax.Precision.HIGHEST)
        b_sc[...] = b
        k_sc[...] = kk
        v_sc[...] = v
        st = st_ref[...]
        o_inter = _dot_nt((q * jnp.exp(b)).astype(BF16), st.astype(BF16))
        prods = []
        for s in range(C):
            bs = b_sc[s:s + 1, :]
            keep = (row <= s) if reverse else (row >= s)
            e = jnp.exp(jnp.where(keep, b - bs, -jnp.inf))
            prods.append((q * e * k_sc[s:s + 1, :]).astype(BF16))
        a_bc = jnp.dot(jnp.concatenate(prods, axis=0), ones, preferred_element_type=F32)
        o_intra = jnp.zeros((C, HG_DIM), F32)
        for s in range(C):
            o_intra = o_intra + a_bc[s * C:(s + 1) * C, :] * v_sc[s:s + 1, :]
        o_sc[sl, :] = o_inter + o_intra
        b_end = b_sc[0:1, :] if reverse else b_sc[C - 1:C, :]
        ks = (kk * jnp.exp(b_end - b)).astype(BF16)
        st_ref[...] = st * jnp.exp(b_end) + _dot_tn(v.astype(BF16), ks)

    if finish:
        o = o_sc[...] + ofw_ref[...]
        g = g_ref[...].astype(F32)
        o_ref[...] = (_rms(o, gain_ref[...]) * (g * jax.nn.sigmoid(g))).astype(o_ref.dtype)
    else:
        o_ref[...] = o_sc[...]


def _hgrn2(p, groups, lb, *, reverse, f_col, o_fw=None, out_gain=None, mix=None):
    T = p.shape[0]
    blk, first = _hg_steps(groups, reverse)
    H = HG_HEADS
    finish = o_fw is not None
    C = HG_CHUNK
    tri = np.triu(np.ones((C, C), np.float32)) if reverse else np.tril(np.ones((C, C), np.float32))

    def col(off):
        return lambda h, n, blk, first: (blk[n], off + h)

    const = lambda h, n, blk, first: (0, 0)
    in_specs = [
        pl.BlockSpec((HG_BLOCK, HG_DIM), col(3 * NA_HEADS)),
        pl.BlockSpec((HG_BLOCK, HG_DIM), col(f_col)),
        pl.BlockSpec((HG_BLOCK, HG_DIM), col(6 * NA_HEADS)),
        pl.BlockSpec((H, HG_DIM), const),
        pl.BlockSpec((C, C), const),
    ]
    args = [p, p, p, lb.reshape(H, HG_DIM).astype(F32), jnp.asarray(tri)]
    aliases = {}
    if finish:
        in_specs += [
            pl.BlockSpec((HG_BLOCK, HG_DIM), col(0)),
            pl.BlockSpec((HG_BLOCK, HG_DIM), col(7 * NA_HEADS)),
            pl.BlockSpec((1, HG_DIM), const),
            pl.BlockSpec(memory_space=pl.ANY),
        ]
        args += [o_fw, p, out_gain.reshape(1, HG_DIM).astype(F32), mix]
        aliases = {2 + len(args) - 1: 0}
        out_spec = pl.BlockSpec((HG_BLOCK, HG_DIM), col(NA_HEADS))
        out_shape = jax.ShapeDtypeStruct(mix.shape, mix.dtype)
    else:
        out_spec = pl.BlockSpec((HG_BLOCK, HG_DIM), col(0))
        out_shape = jax.ShapeDtypeStruct((T, H * HG_DIM), F32)

    def body(blk_ref, first_ref, *refs):
        if finish:
            refs = refs[:8] + refs[9:]
        _hg_kernel(blk_ref, first_ref, *refs, reverse=reverse, finish=finish)

    return pl.pallas_call(
        body,
        grid_spec=pltpu.PrefetchScalarGridSpec(
            num_scalar_prefetch=2,
            grid=(H, len(blk)),
            in_specs=in_specs,
            out_specs=out_spec,
            scratch_shapes=[
                pltpu.VMEM((HG_DIM, HG_DIM), F32),
                pltpu.VMEM((C, HG_DIM), F32),
                pltpu.VMEM((C, HG_DIM), F32),
                pltpu.VMEM((C, HG_DIM), F32),
                pltpu.VMEM((HG_BLOCK, HG_DIM), F32),
            ],
        ),
        out_shape=out_shape,
        input_output_aliases=aliases,
        compiler_params=_cparams(("parallel", "arbitrary")),
        name="hgrn2_bw" if reverse else "hgrn2_fw",
    )(jnp.asarray(blk), jnp.asarray(first), *args)


def _rope_tables(groups):
    half = MLA_ROPE // 2
    smax = max(S for _, S in groups)
    pos = jnp.arange(smax, dtype=F32)
    inv = 1.0 / (ROPE_BASE ** (jnp.arange(0, MLA_ROPE, 2, dtype=F32) / MLA_ROPE))
    ang = pos[:, None] * inv[None, :]
    cos, sin = jnp.cos(ang), jnp.sin(ang)
    z = jnp.zeros((smax, LANES - 2 * half), F32)
    zh = jnp.zeros((smax, half), F32)
    c = jnp.concatenate([cos, cos, z], axis=1)
    s_from_x2 = jnp.concatenate([-sin, zh, z], axis=1)
    s_from_x1 = jnp.concatenate([zh, sin, z], axis=1)
    return c, s_from_x2, s_from_x1


def _rope_block(t, c, s2, s1):
    half = MLA_ROPE // 2
    return t * c + pltpu.roll(t, LANES - half, 1) * s2 + pltpu.roll(t, half, 1) * s1


def _qprep_kernel(q_ref, g_ref, c_ref, s2_ref, s1_ref, o_ref, *, scale):
    c, s2, s1 = c_ref[...], s2_ref[...], s1_ref[...]
    for h in range(MLA_HEADS):
        a = q_ref[:, h * MLA_HEAD_PAD:h * MLA_HEAD_PAD + LANES].astype(F32)
        r = q_ref[:, h * MLA_HEAD_PAD + LANES:(h + 1) * MLA_HEAD_PAD].astype(F32)
        ms = (jnp.sum(a * a, axis=-1, keepdims=True) + jnp.sum(r * r, axis=-1, keepdims=True)) / MLA_QK
        inv = lax.rsqrt(ms + EPS)
        o_ref[:, h * MLA_HEAD_PAD:h * MLA_HEAD_PAD + LANES] = (a * inv * g_ref[:, :LANES] * scale).astype(o_ref.dtype)
        rn = r * inv * g_ref[:, LANES:]
        o_ref[:, h * MLA_HEAD_PAD + LANES:(h + 1) * MLA_HEAD_PAD] = (_rope_block(rn, c, s2, s1) * scale).astype(o_ref.dtype)


def _kprep_kernel(kv_ref, kr_ref, g_ref, c_ref, s2_ref, s1_ref, o_ref):
    c, s2, s1 = c_ref[...], s2_ref[...], s1_ref[...]
    r = kr_ref[...].astype(F32)
    rs = jnp.sum(r * r, axis=-1, keepdims=True)
    for h in range(MLA_HEADS):
        a = kv_ref[:, h * MLA_HEAD_PAD:h * MLA_HEAD_PAD + LANES].astype(F32)
        inv = lax.rsqrt((jnp.sum(a * a, axis=-1, keepdims=True) + rs) / MLA_QK + EPS)
        o_ref[:, h * MLA_HEAD_PAD:h * MLA_HEAD_PAD + LANES] = (a * inv * g_ref[:, :LANES]).astype(o_ref.dtype)
        rn = r * inv * g_ref[:, LANES:]
        o_ref[:, h * MLA_HEAD_PAD + LANES:(h + 1) * MLA_HEAD_PAD] = _rope_block(rn, c, s2, s1).astype(o_ref.dtype)


def _pos_block_map(groups, tm):
    bounds = []
    tok0 = 0
    for (B, S) in groups:
        assert S % tm == 0
        bounds.append((tok0 // tm, S // tm))
        tok0 += B * S

    def imap(i):
        out = (i - bounds[0][0]) % bounds[0][1]
        for (b0, per) in bounds[1:]:
            out = jnp.where(i >= b0, (i - b0) % per, out)
        return (out, 0)

    return imap


def _pad_gain(g):
    return jnp.concatenate([g.astype(F32), jnp.zeros((MLA_HEAD_PAD - MLA_QK,), F32)]).reshape(1, MLA_HEAD_PAD)


def _mla_prep(q, kv, cc, kr_block, groups, q_gain, k_gain, tm=256):
    T = q.shape[0]
    W = MLA_HEADS * MLA_HEAD_PAD
    tabs = _rope_tables(groups)
    pmap = _pos_block_map(groups, tm)
    tab_specs = [pl.BlockSpec((tm, LANES), pmap)] * 3
    row = pl.BlockSpec((tm, W), lambda i: (i, 0))
    gspec = pl.BlockSpec((1, MLA_HEAD_PAD), lambda i: (0, 0))
    qf = pl.pallas_call(
        functools.partial(_qprep_kernel, scale=MLA_QK ** -0.5 * LOG2_E),
        grid=(T // tm,),
        in_specs=[row, gspec] + tab_specs,
        out_specs=row,
        out_shape=jax.ShapeDtypeStruct((T, W), BF16),
        compiler_params=_cparams(("parallel",)),
        name="mla_qprep",
    )(q, _pad_gain(q_gain), *tabs)
    kf = pl.pallas_call(
        _kprep_kernel,
        grid=(T // tm,),
        in_specs=[row, pl.BlockSpec((tm, LANES), lambda i: (i, kr_block)), gspec] + tab_specs,
        out_specs=row,
        out_shape=jax.ShapeDtypeStruct((T, W), BF16),
        compiler_params=_cparams(("parallel",)),
        name="mla_kprep",
    )(kv, cc, _pad_gain(k_gain), *tabs)
    return qf, kf


FLASH_SUB = 512
LOG2_E = 1.4426950408889634


def _flash_kernel(q_ref, k_ref, v_ref, *rest, aliased):
    o_ref, m_sc, l_sc, acc_sc = rest[1:] if aliased else rest
    ki = pl.program_id(3)

    @pl.when(ki == 0)
    def _():
        m_sc[...] = jnp.full_like(m_sc, -jnp.inf)
        l_sc[...] = jnp.zeros_like(l_sc)
        acc_sc[...] = jnp.zeros_like(acc_sc)

    q = q_ref[...]
    nsub = k_ref.shape[0] // FLASH_SUB
    s = [_dot_nt(q, k_ref[c * FLASH_SUB:(c + 1) * FLASH_SUB, :]) for c in range(nsub)]
    m_prev = m_sc[...]
    m_new = m_prev
    for c in range(nsub):
        m_new = jnp.maximum(m_new, s[c].max(axis=-1, keepdims=True))
    alpha = jnp.exp2(m_prev - m_new)
    l = alpha * l_sc[...]
    acc = alpha * acc_sc[...]
    for c in range(nsub):
        p = jnp.exp2(s[c] - m_new)
        l = l + p.sum(axis=-1, keepdims=True)
        acc = acc + jnp.dot(p.astype(BF16), v_ref[c * FLASH_SUB:(c + 1) * FLASH_SUB, :], preferred_element_type=F32)
    l_sc[...] = l
    acc_sc[...] = acc
    m_sc[...] = m_new

    @pl.when(ki == pl.num_programs(3) - 1)
    def _():
        o_ref[...] = (acc_sc[...] / l_sc[...]).astype(o_ref.dtype)


def _mla_attention(qf, kf, kv, groups, tq=256, tk_max=16384):
    T = qf.shape[0]
    H = MLA_HEADS
    out = None
    tok0 = 0
    for (B, S) in groups:
        tk = min(tk_max, S)
        assert S % tq == 0 and S % tk == 0 and tok0 % tq == 0 and tok0 % tk == 0 and tk % FLASH_SUB == 0
        nq, nk = S // tq, S // tk
        q0, k0 = tok0 // tq, tok0 // tk
        kv_mode = pl.Buffered(1) if nk == 1 else None
        in_specs = [
            pl.BlockSpec((tq, MLA_HEAD_PAD), lambda b, h, qi, ki, q0=q0, nq=nq: (q0 + b * nq + qi, h)),
            pl.BlockSpec((tk, MLA_HEAD_PAD), lambda b, h, qi, ki, k0=k0, nk=nk: (k0 + b * nk + ki, h),
                         pipeline_mode=kv_mode),
            pl.BlockSpec((tk, LANES), lambda b, h, qi, ki, k0=k0, nk=nk: (k0 + b * nk + ki, 2 * h + 1),
                         pipeline_mode=kv_mode),
        ]
        args = [qf, kf, kv]
        aliases = {}
        if out is not None:
            in_specs.append(pl.BlockSpec(memory_space=pl.ANY))
            args.append(out)
            aliases = {3: 0}
        out = pl.pallas_call(
            functools.partial(_flash_kernel, aliased=out is not None),
            grid=(B, H, nq, nk),
            in_specs=in_specs,
            out_specs=pl.BlockSpec((tq, LANES), lambda b, h, qi, ki, q0=q0, nq=nq: (q0 + b * nq + qi, h)),
            out_shape=jax.ShapeDtypeStruct((T, H * LANES), BF16),
            scratch_shapes=[pltpu.VMEM((tq, 1), F32), pltpu.VMEM((tq, 1), F32), pltpu.VMEM((tq, LANES), F32)],
            input_output_aliases=aliases,
            compiler_params=_cparams(("parallel", "parallel", "parallel", "arbitrary")),
            name="mla_flash",
        )(*args)
        tok0 += B * S
    return out


def _peer_candidates():
    a, b = [], []
    a += [0] * 16; b += list(range(16))
    for aa in (1, 2, 3):
        a += [aa] * 8; b += list(range(8))
    for bb in (0, 1):
        a += [4, 5, 6, 7]; b += [bb] * 4
    a += list(range(8, 16)); b += [0] * 8
    for bb in (2, 3):
        a += [4, 5, 6, 7]; b += [bb] * 4
    return np.asarray(a), np.asarray(b)


_CAND_A, _CAND_B = _peer_candidates()
N_CAND = len(_CAND_A)


def _top16_sublane(s):
    n = s.shape[0]
    idx = lax.broadcasted_iota(jnp.int32, s.shape, 0).astype(F32)
    rank = jnp.full(s.shape, float(PEER_TOPK), F32)
    val = s
    tops = []
    for r in range(PEER_TOPK):
        m = val.max(axis=0, keepdims=True)
        first = jnp.where(val == m, idx, float(n)).min(axis=0, keepdims=True)
        sel = idx == first
        rank = jnp.where(sel, float(r), rank)
        val = jnp.where(sel, -jnp.inf, val)
        tops.append(m)
    return jnp.concatenate(tops, axis=0), rank


def _peer_route_kernel(x_ref, g_ref, wq_ref, keys_ref, pos_ref, amat_ref,
                       hn_ref, cnt_ref, e1_ref, rb_ref, e2_ref):
    tm = x_ref.shape[0]
    hn = _rms(x_ref[...].astype(F32), g_ref[...]).astype(BF16)
    hn_ref[...] = hn
    qry = jnp.dot(hn, wq_ref[...], preferred_element_type=F32)
    pos = pos_ref[...]
    for h in range(PEER_HEADS):
        sc, tops, ranks = [], [], []
        for half in range(2):
            c0 = (2 * h + half) * PEER_HALF
            s = _dot_nt(keys_ref[h, half], qry[:, c0:c0 + PEER_HALF].astype(BF16))
            t, r = _top16_sublane(s)
            sc.append(s); tops.append(t); ranks.append(r)
        t1, t2 = tops
        mid1 = jnp.concatenate([t1[4:8], t1[4:8]], axis=0)
        rep2 = lambda b: jnp.broadcast_to(t2[b:b + 1], (4, tm))
        cand = jnp.concatenate([
            t1[0:1] + t2, t1[1:2] + t2[0:8], t1[2:3] + t2[0:8], t1[3:4] + t2[0:8],
            mid1 + jnp.concatenate([rep2(0), rep2(1)], axis=0),
            t1[8:16] + t2[0:1],
            mid1 + jnp.concatenate([rep2(2), rep2(3)], axis=0)], axis=0)
        best = cand[0:1]
        val = cand
        chosen = jnp.zeros(cand.shape, F32)
        for _ in range(PEER_TOPK):
            m = val.max(axis=0, keepdims=True)
            first = jnp.where(val == m, pos, 1e9).min(axis=0, keepdims=True)
            sel = pos == first
            chosen = jnp.where(sel, 1.0, chosen)
            val = jnp.where(sel, -jnp.inf, val)
        z = jnp.sum(chosen * jnp.exp(cand - best), axis=0, keepdims=True)
        per_a = jnp.dot(amat_ref[...], chosen.astype(BF16), preferred_element_type=F32)
        cnt = jnp.zeros(sc[0].shape, F32)
        for a in range(PEER_TOPK):
            cnt = jnp.where(ranks[0] == float(a), per_a[a:a + 1], cnt)
        cnt_ref[h] = cnt
        e1_ref[h] = jnp.exp(sc[0] - t1[0:1]) / z
        rb_ref[h] = ranks[1].astype(rb_ref.dtype)
        e2_ref[h] = jnp.exp(sc[1] - t2[0:1]).astype(e2_ref.dtype)


def _peer_route(x, norm_gain, w_query, sub_keys, tm=256):
    T, D = x.shape
    H = PEER_HEADS
    pos = np.broadcast_to((_CAND_A * PEER_TOPK + _CAND_B).astype(np.float32)[:, None], (N_CAND, tm))
    amat = (np.arange(PEER_TOPK)[:, None] == _CAND_A[None, :]).astype(np.float32)
    gate_shape = jax.ShapeDtypeStruct((H, PEER_N_KEYS, T), F32)
    gate_shape_bf16 = jax.ShapeDtypeStruct((H, PEER_N_KEYS, T), BF16)
    gate_spec = pl.BlockSpec((H, PEER_N_KEYS, tm), lambda i: (0, 0, i))
    return pl.pallas_call(
        _peer_route_kernel,
        grid=(T // tm,),
        in_specs=[
            pl.BlockSpec((tm, D), lambda i: (i, 0)),
            pl.BlockSpec((1, D), lambda i: (0, 0)),
            pl.BlockSpec(w_query.shape, lambda i: (0, 0)),
            pl.BlockSpec(sub_keys.shape, lambda i: (0, 0, 0, 0)),
            pl.BlockSpec((N_CAND, tm), lambda i: (0, 0)),
            pl.BlockSpec((PEER_TOPK, N_CAND), lambda i: (0, 0)),
        ],
        out_specs=[pl.BlockSpec((tm, D), lambda i: (i, 0))] + [gate_spec] * 4,
        out_shape=[jax.ShapeDtypeStruct((T, D), BF16), gate_shape, gate_shape, gate_shape_bf16, gate_shape_bf16],
        compiler_params=_cparams(("parallel",)),
        name="peer_route",
    )(x, norm_gain.reshape(1, D).astype(F32), w_query, sub_keys, jnp.asarray(pos), jnp.asarray(amat, BF16))


def _gelu(a):
    return 0.5 * a * (1.0 + lax.erf(a * (2.0 ** -0.5)))


def _peer_expert_kernel(hn_ref, u_ref, vt_ref, cnt_ref, e1_ref, rb_ref, e2_ref, x_ref, o_ref, acc_ref, g_ref, w_ref,
                        *, rows_per_step):
    e = pl.program_id(1)

    @pl.when(e == 0)
    def _():
        acc_ref[...] = jnp.zeros_like(acc_ref)

    def gate_weights(step, slot):
        for il in range(rows_per_step):
            i = step * rows_per_step + il
            cnt = [cnt_ref[pl.ds(h * PEER_N_KEYS + i, 1), :].astype(BF16) for h in range(PEER_HEADS)]
            e1 = [e1_ref[pl.ds(h * PEER_N_KEYS + i, 1), :].astype(BF16) for h in range(PEER_HEADS)]
            for st in range(w_ref.shape[2] // LANES):
                ls = slice(st * LANES, (st + 1) * LANES)
                w = jnp.zeros((PEER_N_KEYS, LANES), BF16)
                for h in range(PEER_HEADS):
                    hs = slice(h * PEER_N_KEYS, (h + 1) * PEER_N_KEYS)
                    w = w + jnp.where(rb_ref[hs, ls] < cnt[h][:, ls], e2_ref[hs, ls], 0.0) * e1[h][:, ls]
                w_ref[slot, il * PEER_N_KEYS:(il + 1) * PEER_N_KEYS, ls] = w

    @pl.when(e == 0)
    def _():
        gate_weights(0, 0)

    gate_weights(jnp.minimum(e + 1, pl.num_programs(1) - 1), (e + 1) % 2)
    act = _gelu(_dot_nt(u_ref[...], hn_ref[...]))
    g_ref[...] = w_ref[e % 2] * act.astype(BF16)
    acc_ref[...] += jnp.dot(vt_ref[...], g_ref[...], preferred_element_type=F32)

    @pl.when(e == pl.num_programs(1) - 1)
    def _():
        o_ref[...] = x_ref[...] + acc_ref[...].T


def _peer_experts(x, hn, gates, u, vt, tm=512, te=512):
    T, D = x.shape
    NE = u.shape[0]
    gates = [g.reshape(-1, T) for g in gates]
    gate_spec = pl.BlockSpec((PEER_HEADS * PEER_N_KEYS, tm), lambda i, e: (0, i))
    return pl.pallas_call(
        functools.partial(_peer_expert_kernel, rows_per_step=te // PEER_N_KEYS),
        grid=(T // tm, NE // te),
        in_specs=[
            pl.BlockSpec((tm, D), lambda i, e: (i, 0)),
            pl.BlockSpec((te, D), lambda i, e: (e, 0)),
            pl.BlockSpec((D, te), lambda i, e: (0, e)),
            gate_spec, gate_spec, gate_spec, gate_spec,
            pl.BlockSpec((tm, D), lambda i, e: (i, 0)),
        ],
        out_specs=pl.BlockSpec((tm, D), lambda i, e: (i, 0)),
        out_shape=jax.ShapeDtypeStruct((T, D), F32),
        scratch_shapes=[pltpu.VMEM((D, tm), F32), pltpu.VMEM((te, tm), BF16), pltpu.VMEM((2, te, tm), BF16)],
        compiler_params=_cparams(("parallel", "arbitrary")),
        name="peer_experts",
    )(hn, u, vt, *gates, x)


def _peer(x, norm_gain, w_query, sub_keys, u, v):
    hn, cnt, e1, rb, e2 = _peer_route(x, norm_gain, w_query.astype(BF16), sub_keys.astype(BF16))
    return _peer_experts(x, hn, (cnt, e1, rb, e2), u.astype(BF16), v.astype(BF16).T)


def _even_layer(x, groups, norm, w_in, na_q_gain, na_k_gain, na_rpb, lb, hg_out_gain, w_out):
    p = _matmul(x, w_in.astype(BF16), gain=norm, out_dtype=F32)
    mix_cols = NA_HEADS * NA_HEAD_DIM + HG_HEADS * HG_DIM
    mix = _neighbourhood_attention(p, groups, na_rpb, na_q_gain, na_k_gain, mix_cols)
    o_fw = _hgrn2(p, groups, lb[0], reverse=False, f_col=4 * NA_HEADS)
    mix = _hgrn2(p, groups, lb[1], reverse=True, f_col=5 * NA_HEADS, o_fw=o_fw, out_gain=hg_out_gain, mix=mix)
    return _matmul(mix, w_out.astype(BF16), residual=x, out_dtype=F32)


def _odd_layer(x, groups, norm, w_in, q_norm, kv_norm, w_uq, w_ukv, q_gain, k_gain, w_out):
    D = x.shape[1]
    pad_q = (-MLA_Q_RANK) % MLA_KV_RANK
    w_cq, w_ckv, w_kr = jnp.split(w_in, [MLA_Q_RANK, MLA_Q_RANK + MLA_KV_RANK], axis=1)
    w_in_p = jnp.concatenate(
        [w_cq, jnp.zeros((D, pad_q), w_in.dtype), w_ckv, w_kr, jnp.zeros((D, LANES - MLA_ROPE), w_in.dtype)],
        axis=1).astype(BF16)
    cc = _matmul(x, w_in_p, gain=norm, out_dtype=F32, tn=w_in_p.shape[1])
    w_uq_p = jnp.pad(w_uq.reshape(MLA_Q_RANK, MLA_HEADS, MLA_QK),
                     ((0, 0), (0, 0), (0, MLA_HEAD_PAD - MLA_QK))).reshape(MLA_Q_RANK, -1).astype(BF16)
    q = _matmul(cc, w_uq_p, gain=q_norm, out_dtype=BF16, x_col_block=0)
    kv = _matmul(cc, w_ukv.astype(BF16), gain=kv_norm, out_dtype=BF16,
                 x_col_block=(MLA_Q_RANK + pad_q) // MLA_KV_RANK)
    kr_block = (MLA_Q_RANK + pad_q + MLA_KV_RANK) // LANES
    qf, kf = _mla_prep(q, kv, cc, kr_block, groups, q_gain, k_gain)
    o = _mla_attention(qf, kf, kv, groups)
    return _matmul(o, w_out.astype(BF16), residual=x, out_dtype=F32)


def _forward(xs, ev_norm, ev_w_in, ev_na_q_gain, ev_na_k_gain, ev_na_rpb, ev_hg_lb_logits, ev_hg_out_gain,
             ev_w_out, od_norm, od_w_in, od_q_norm, od_kv_norm, od_w_uq, od_w_ukv, od_q_gain, od_k_gain,
             od_w_out, ff_norm, peer_w_query, peer_sub_keys, peer_u, peer_v):
    groups = [(int(x.shape[0]), int(x.shape[1])) for x in xs]
    D = xs[0].shape[-1]
    depth = ff_norm.shape[0]
    lb_all = jnp.cumsum(jax.nn.softmax(ev_hg_lb_logits.astype(F32), axis=1), axis=1)
    x = jnp.concatenate([t.reshape(-1, D) for t in xs], axis=0)
    for layer in range(depth):
        j = layer // 2
        if layer % 2 == 0:
            x = _even_layer(x, groups, ev_norm[j], ev_w_in[j], ev_na_q_gain[j], ev_na_k_gain[j], ev_na_rpb[j],
                            lb_all[:, j], ev_hg_out_gain[j], ev_w_out[j])
        else:
            x = _odd_layer(x, groups, od_norm[j], od_w_in[j], od_q_norm[j], od_kv_norm[j], od_w_uq[j],
                           od_w_ukv[j], od_q_gain[j], od_k_gain[j], od_w_out[j])
        x = _peer(x, ff_norm[layer], peer_w_query[layer], peer_sub_keys[layer], peer_u[layer], peer_v[layer])
    outs = []
    tok0 = 0
    for (B, S) in groups:
        outs.append(x[tok0:tok0 + B * S].reshape(B, S, D))
        tok0 += B * S
    return tuple(outs)


def kernel(x_prompt, x_sample, ev_norm, ev_w_in, ev_na_q_gain, ev_na_k_gain, ev_na_rpb, ev_hg_lb_logits, ev_hg_out_gain, ev_w_out, od_norm, od_w_in, od_q_norm, od_kv_norm, od_w_uq, od_w_ukv, od_q_gain, od_k_gain, od_w_out, ff_norm, peer_w_query, peer_sub_keys, peer_u, peer_v):
    return _forward((x_prompt, x_sample), ev_norm, ev_w_in, ev_na_q_gain, ev_na_k_gain, ev_na_rpb, ev_hg_lb_logits,
                    ev_hg_out_gain, ev_w_out, od_norm, od_w_in, od_q_norm, od_kv_norm, od_w_uq, od_w_ukv, od_q_gain,
                    od_k_gain, od_w_out, ff_norm, peer_w_query, peer_sub_keys, peer_u, peer_v)
```

```python
import functools

import numpy as np
import jax
import jax.numpy as jnp
from jax import lax
from jax.experimental import pallas as pl
from jax.experimental.pallas import tpu as pltpu

F32 = jnp.float32
BF16 = jnp.bfloat16

EPS = 1e-6
LANES = 128
VMEM_LIMIT_BYTES = 56 * 1024 * 1024
NEG_BIG = -1e30

GRID_W = 64
NA_HEADS = 8
NA_HEAD_DIM = 128
NA_WIN_ROWS = 8
NA_WIN_COLS = 16
NA_ROW_BLOCK = 8
NA_KEY_ROWS = 16
HG_HEADS = 8
HG_DIM = 128
HG_CHUNK = 32
MLA_HEADS = 16
MLA_Q_RANK = 768
MLA_KV_RANK = 512
MLA_NOPE = 128
MLA_ROPE = 64
MLA_QK = MLA_NOPE + MLA_ROPE
MLA_HEAD_PAD = 256
ROPE_BASE = 10000.0
PEER_HEADS = 8
PEER_N_KEYS = 128
PEER_TOPK = 16
PEER_HALF = 128


def _cparams(sem):
    return pltpu.CompilerParams(dimension_semantics=sem, vmem_limit_bytes=VMEM_LIMIT_BYTES)


def _rms(x, gain):
    return x * lax.rsqrt(jnp.mean(x * x, axis=-1, keepdims=True) + EPS) * gain


def _dot_nt(a, b):
    return lax.dot_general(a, b, (((1,), (1,)), ((), ())), preferred_element_type=F32)


def _dot_tn(a, b):
    return lax.dot_general(a, b, (((0,), (0,)), ((), ())), preferred_element_type=F32)


def _mm_kernel(*refs, has_norm, has_res):
    it = iter(refs)
    x_ref = next(it)
    g_ref = next(it) if has_norm else None
    w_ref = next(it)
    r_ref = next(it) if has_res else None
    o_ref = next(it)
    if has_norm:
        hn_ref = next(it)

        @pl.when(pl.program_id(1) == 0)
        def _():
            hn_ref[...] = _rms(x_ref[...].astype(F32), g_ref[...]).astype(BF16)

        a = hn_ref[...]
    else:
        a = x_ref[...].astype(BF16)
    acc = jnp.dot(a, w_ref[...], preferred_element_type=F32)
    if has_res:
        acc = acc + r_ref[...]
    o_ref[...] = acc.astype(o_ref.dtype)


def _matmul(x, w, *, gain=None, residual=None, out_dtype=F32, x_col_block=0, tm=512, tn=1024):
    M = x.shape[0]
    K, N = w.shape
    tn = min(tn, N)
    assert M % tm == 0 and N % tn == 0 and (x_col_block + 1) * K <= x.shape[1]
    has_norm, has_res = gain is not None, residual is not None
    in_specs = [pl.BlockSpec((tm, K), lambda i, j: (i, x_col_block))]
    args = [x]
    if has_norm:
        in_specs.append(pl.BlockSpec((1, K), lambda i, j: (0, 0)))
        args.append(gain.reshape(1, K).astype(F32))
    in_specs.append(pl.BlockSpec((K, tn), lambda i, j: (0, j)))
    args.append(w)
    if has_res:
        in_specs.append(pl.BlockSpec((tm, tn), lambda i, j: (i, j)))
        args.append(residual)
    return pl.pallas_call(
        functools.partial(_mm_kernel, has_norm=has_norm, has_res=has_res),
        grid=(M // tm, N // tn),
        in_specs=in_specs,
        out_specs=pl.BlockSpec((tm, tn), lambda i, j: (i, j)),
        out_shape=jax.ShapeDtypeStruct((M, N), out_dtype),
        scratch_shapes=[pltpu.VMEM((tm, K), BF16)] if has_norm else [],
        compiler_params=_cparams(("parallel", "arbitrary")),
        name="mm",
    )(*args)


NA_Q = NA_ROW_BLOCK * GRID_W
NA_KB = 256
NA_NKB = NA_KEY_ROWS * GRID_W // NA_KB


def _na_bias_tables(rpb):
    i = np.arange(NA_ROW_BLOCK)[:, None, None, None]
    c = np.arange(GRID_W)[None, :, None, None]
    j = np.arange(NA_KEY_ROWS)[None, None, :, None]
    kc = np.arange(GRID_W)[None, None, None, :]
    half = NA_WIN_ROWS // 2
    qc0 = np.clip(c - NA_WIN_COLS // 2, 0, GRID_W - NA_WIN_COLS)
    col_ok = (kc >= qc0) & (kc < qc0 + NA_WIN_COLS)
    dcol = np.clip(kc - c + NA_WIN_COLS - 1, 0, 2 * NA_WIN_COLS - 2)
    col_hot = (dcol[0, :, 0, :, None] == np.arange(2 * NA_WIN_COLS - 1)).astype(np.float32)
    by_col = jnp.einsum('hab,ckb->hack', rpb.astype(F32), col_hot, precision=lax.Precision.HIGHEST)
    tabs = []
    for var in range(3):
        if var == 0:
            r0 = np.maximum(i - half, 0)
            dr = j - i
        elif var == 1:
            r0 = i
            dr = j - half - i
        else:
            r0 = np.minimum(i + half, NA_KEY_ROWS - NA_WIN_ROWS)
            dr = j - (NA_KEY_ROWS - NA_ROW_BLOCK) - i
        ok = (j >= r0) & (j < r0 + NA_WIN_ROWS) & col_ok
        dri = np.clip(dr + NA_WIN_ROWS - 1, 0, 2 * NA_WIN_ROWS - 2)[:, 0, :, 0]
        row_hot = (dri[:, :, None] == np.arange(2 * NA_WIN_ROWS - 1)).astype(np.float32)
        vals = jnp.einsum('ija,hack->hicjk', row_hot, by_col, precision=lax.Precision.HIGHEST)
        tabs.append(jnp.where(jnp.asarray(ok)[None], vals, NEG_BIG).reshape(
            NA_HEADS, NA_Q, NA_KEY_ROWS * GRID_W))
    return jnp.stack(tabs)


def _na_steps(groups):
    steps = []
    tok0 = 0
    for (B, S) in groups:
        rows = S // GRID_W
        nrb = rows // NA_ROW_BLOCK
        assert nrb >= 3 and S % NA_Q == 0
        for b in range(B):
            for rb in range(nrb):
                var = 0 if rb == 0 else (2 if rb == nrb - 1 else 1)
                kr0 = min(max(NA_ROW_BLOCK * rb - NA_WIN_ROWS // 2, 0), rows - NA_KEY_ROWS)
                steps.append((var, (tok0 + b * S) // NA_Q + rb, (tok0 + b * S + kr0 * GRID_W) // NA_KB))
        tok0 += B * S
    steps.sort(key=lambda t: t[0])
    arr = np.asarray(steps, np.int32)
    return arr[:, 0], arr[:, 1], arr[:, 2]


def _na_kernel(var_ref, qb_ref, kb_ref, q_ref, *rest):
    k_refs = rest[:NA_NKB]
    v_refs = rest[NA_NKB:2 * NA_NKB]
    bias_ref, qg_ref, kg_ref, o_ref = rest[2 * NA_NKB:]
    scale = NA_HEAD_DIM ** -0.5
    q = _rms(q_ref[...].astype(F32), qg_ref[...]).astype(BF16)
    s = []
    for m in range(NA_NKB):
        k = _rms(k_refs[m][...].astype(F32), kg_ref[...]).astype(BF16)
        s.append(_dot_nt(q, k) * scale + bias_ref[0, 0, :, m * NA_KB:(m + 1) * NA_KB])
    mx = s[0].max(axis=-1, keepdims=True)
    for m in range(1, NA_NKB):
        mx = jnp.maximum(mx, s[m].max(axis=-1, keepdims=True))
    l = jnp.zeros_like(mx)
    acc = jnp.zeros((NA_Q, NA_HEAD_DIM), F32)
    for m in range(NA_NKB):
        p = jnp.exp(s[m] - mx)
        l = l + p.sum(axis=-1, keepdims=True)
        acc = acc + jnp.dot(p.astype(BF16), v_refs[m][...].astype(BF16), preferred_element_type=F32)
    o_ref[...] = (acc / l).astype(o_ref.dtype)


def _neighbourhood_attention(p, groups, rpb, q_gain, k_gain, out_cols):
    T = p.shape[0]
    var, qblk, kblk = _na_steps(groups)
    tabs = _na_bias_tables(rpb)
    H = NA_HEADS

    def qmap(h, n, var, qb, kb):
        return (qb[n], h)

    def kmap(m, off):
        return lambda h, n, var, qb, kb: (kb[n] + m, off + h)

    in_specs = [pl.BlockSpec((NA_Q, NA_HEAD_DIM), qmap)]
    in_specs += [pl.BlockSpec((NA_KB, NA_HEAD_DIM), kmap(m, H)) for m in range(NA_NKB)]
    in_specs += [pl.BlockSpec((NA_KB, NA_HEAD_DIM), kmap(m, 2 * H)) for m in range(NA_NKB)]
    in_specs += [
        pl.BlockSpec((1, 1, NA_Q, NA_KEY_ROWS * GRID_W), lambda h, n, var, qb, kb: (var[n], h, 0, 0)),
        pl.BlockSpec((1, NA_HEAD_DIM), lambda h, n, var, qb, kb: (0, 0)),
        pl.BlockSpec((1, NA_HEAD_DIM), lambda h, n, var, qb, kb: (0, 0)),
    ]
    return pl.pallas_call(
        _na_kernel,
        grid_spec=pltpu.PrefetchScalarGridSpec(
            num_scalar_prefetch=3,
            grid=(H, len(var)),
            in_specs=in_specs,
            out_specs=pl.BlockSpec((NA_Q, NA_HEAD_DIM), qmap),
        ),
        out_shape=jax.ShapeDtypeStruct((T, out_cols), BF16),
        compiler_params=_cparams(("parallel", "arbitrary")),
        name="natten",
    )(jnp.asarray(var), jnp.asarray(qblk), jnp.asarray(kblk), p, *([p] * (2 * NA_NKB)), tabs,
      q_gain.reshape(1, -1).astype(F32), k_gain.reshape(1, -1).astype(F32))


HG_BLOCK = 256
HG_NCHUNK = HG_BLOCK // HG_CHUNK


def _hg_steps(groups, reverse):
    blk, first = [], []
    tok0 = 0
    for (B, S) in groups:
        nb = S // HG_BLOCK
        assert S % HG_BLOCK == 0
        for b in range(B):
            order = range(nb - 1, -1, -1) if reverse else range(nb)
            for n, sb in enumerate(order):
                blk.append((tok0 + b * S) // HG_BLOCK + sb)
                first.append(1 if n == 0 else 0)
        tok0 += B * S
    return np.asarray(blk, np.int32), np.asarray(first, np.int32)


def _hg_kernel(blk_ref, first_ref, q_ref, f_ref, i_ref, lb_ref, tri_ref, *rest, reverse, finish):
    if finish:
        ofw_ref, g_ref, gain_ref, o_ref, st_ref, b_sc, k_sc, v_sc, o_sc = rest
    else:
        o_ref, st_ref, b_sc, k_sc, v_sc, o_sc = rest
    h = pl.program_id(0)
    n = pl.program_id(1)
    C = HG_CHUNK

    @pl.when(first_ref[n] == 1)
    def _():
        st_ref[...] = jnp.zeros_like(st_ref)

    lb = lb_ref[pl.ds(h, 1), :]
    tri = tri_ref[...]
    row = lax.broadcasted_iota(jnp.int32, (C, HG_DIM), 0)
    ones = jnp.ones((HG_DIM, HG_DIM), BF16)

    for cc in range(HG_NCHUNK):
        c = HG_NCHUNK - 1 - cc if reverse else cc
        sl = pl.ds(c * C, C)
        f = lb + (1.0 - lb) * jax.nn.sigmoid(f_ref[sl, :].astype(F32))
        logf = jnp.log(f)
        kk = 1.0 - f
        q = q_ref[sl, :].astype(F32)
        q = q * jax.nn.sigmoid(q)
        v = i_ref[sl, :].astype(F32)
        b = jnp.dot(tri, logf, preferred_element_type=F32, precision=lax.Precision.HIGHEST)
        b_sc[...] = b
        k_sc[...] = kk
        v_sc[...] = v
        st = st_ref[...]
        o_inter = _dot_nt((q * jnp.exp(b)).astype(BF16), st.astype(BF16))
        prods = []
        for s in range(C):
            bs = b_sc[s:s + 1, :]
            keep = (row <= s) if reverse else (row >= s)
            e = jnp.exp(jnp.where(keep, b - bs, -jnp.inf))
            prods.append((q * e * k_sc[s:s + 1, :]).astype(BF16))
        a_bc = jnp.dot(jnp.concatenate(prods, axis=0), ones, preferred_element_type=F32)
        o_intra = jnp.zeros((C, HG_DIM), F32)
        for s in range(C):
            o_intra = o_intra + a_bc[s * C:(s + 1) * C, :] * v_sc[s:s + 1, :]
        o_sc[sl, :] = o_inter + o_intra
        b_end = b_sc[0:1, :] if reverse else b_sc[C - 1:C, :]
        ks = (kk * jnp.exp(b_end - b)).astype(BF16)
        st_ref[...] = st * jnp.exp(b_end) + _dot_tn(v.astype(BF16), ks)

    if finish:
        o = o_sc[...] + ofw_ref[...]
        g = g_ref[...].astype(F32)
        o_ref[...] = (_rms(o, gain_ref[...]) * (g * jax.nn.sigmoid(g))).astype(o_ref.dtype)
    else:
        o_ref[...] = o_sc[...]


def _hgrn2(p, groups, lb, *, reverse, f_col, o_fw=None, out_gain=None, mix=None):
    T = p.shape[0]
    blk, first = _hg_steps(groups, reverse)
    H = HG_HEADS
    finish = o_fw is not None
    C = HG_CHUNK
    tri = np.triu(np.ones((C, C), np.float32)) if reverse else np.tril(np.ones((C, C), np.float32))

    def col(off):
        return lambda h, n, blk, first: (blk[n], off + h)

    const = lambda h, n, blk, first: (0, 0)
    in_specs = [
        pl.BlockSpec((HG_BLOCK, HG_DIM), col(3 * NA_HEADS)),
        pl.BlockSpec((HG_BLOCK, HG_DIM), col(f_col)),
        pl.BlockSpec((HG_BLOCK, HG_DIM), col(6 * NA_HEADS)),
        pl.BlockSpec((H, HG_DIM), const),
        pl.BlockSpec((C, C), const),
    ]
    args = [p, p, p, lb.reshape(H, HG_DIM).astype(F32), jnp.asarray(tri)]
    aliases = {}
    if finish:
        in_specs += [
            pl.BlockSpec((HG_BLOCK, HG_DIM), col(0)),
            pl.BlockSpec((HG_BLOCK, HG_DIM), col(7 * NA_HEADS)),
            pl.BlockSpec((1, HG_DIM), const),
            pl.BlockSpec(memory_space=pl.ANY),
        ]
        args += [o_fw, p, out_gain.reshape(1, HG_DIM).astype(F32), mix]
        aliases = {2 + len(args) - 1: 0}
        out_spec = pl.BlockSpec((HG_BLOCK, HG_DIM), col(NA_HEADS))
        out_shape = jax.ShapeDtypeStruct(mix.shape, mix.dtype)
    else:
        out_spec = pl.BlockSpec((HG_BLOCK, HG_DIM), col(0))
        out_shape = jax.ShapeDtypeStruct((T, H * HG_DIM), F32)

    def body(blk_ref, first_ref, *refs):
        if finish:
            refs = refs[:8] + refs[9:]
        _hg_kernel(blk_ref, first_ref, *refs, reverse=reverse, finish=finish)

    return pl.pallas_call(
        body,
        grid_spec=pltpu.PrefetchScalarGridSpec(
            num_scalar_prefetch=2,
            grid=(H, len(blk)),
            in_specs=in_specs,
            out_specs=out_spec,
            scratch_shapes=[
                pltpu.VMEM((HG_DIM, HG_DIM), F32),
                pltpu.VMEM((C, HG_DIM), F32),
                pltpu.VMEM((C, HG_DIM), F32),
                pltpu.VMEM((C, HG_DIM), F32),
                pltpu.VMEM((HG_BLOCK, HG_DIM), F32),
            ],
        ),
        out_shape=out_shape,
        input_output_aliases=aliases,
        compiler_params=_cparams(("parallel", "arbitrary")),
        name="hgrn2_bw" if reverse else "hgrn2_fw",
    )(jnp.asarray(blk), jnp.asarray(first), *args)


def _rope_tables(groups):
    half = MLA_ROPE // 2
    smax = max(S for _, S in groups)
    pos = jnp.arange(smax, dtype=F32)
    inv = 1.0 / (ROPE_BASE ** (jnp.arange(0, MLA_ROPE, 2, dtype=F32) / MLA_ROPE))
    ang = pos[:, None] * inv[None, :]
    cos, sin = jnp.cos(ang), jnp.sin(ang)
    z = jnp.zeros((smax, LANES - 2 * half), F32)
    zh = jnp.zeros((smax, half), F32)
    c = jnp.concatenate([cos, cos, z], axis=1)
    s_from_x2 = jnp.concatenate([-sin, zh, z], axis=1)
    s_from_x1 = jnp.concatenate([zh, sin, z], axis=1)
    return c, s_from_x2, s_from_x1


def _rope_block(t, c, s2, s1):
    half = MLA_ROPE // 2
    return t * c + pltpu.roll(t, LANES - half, 1) * s2 + pltpu.roll(t, half, 1) * s1


def _qprep_kernel(q_ref, g_ref, c_ref, s2_ref, s1_ref, o_ref, *, scale):
    c, s2, s1 = c_ref[...], s2_ref[...], s1_ref[...]
    for h in range(MLA_HEADS):
        a = q_ref[:, h * MLA_HEAD_PAD:h * MLA_HEAD_PAD + LANES].astype(F32)
        r = q_ref[:, h * MLA_HEAD_PAD + LANES:(h + 1) * MLA_HEAD_PAD].astype(F32)
        ms = (jnp.sum(a * a, axis=-1, keepdims=True) + jnp.sum(r * r, axis=-1, keepdims=True)) / MLA_QK
        inv = lax.rsqrt(ms + EPS)
        o_ref[:, h * MLA_HEAD_PAD:h * MLA_HEAD_PAD + LANES] = (a * inv * g_ref[:, :LANES] * scale).astype(o_ref.dtype)
        rn = r * inv * g_ref[:, LANES:]
        o_ref[:, h * MLA_HEAD_PAD + LANES:(h + 1) * MLA_HEAD_PAD] = (_rope_block(rn, c, s2, s1) * scale).astype(o_ref.dtype)


def _kprep_kernel(kv_ref, kr_ref, g_ref, c_ref, s2_ref, s1_ref, o_ref):
    c, s2, s1 = c_ref[...], s2_ref[...], s1_ref[...]
    r = kr_ref[...].astype(F32)
    rs = jnp.sum(r * r, axis=-1, keepdims=True)
    for h in range(MLA_HEADS):
        a = kv_ref[:, h * MLA_HEAD_PAD:h * MLA_HEAD_PAD + LANES].astype(F32)
        inv = lax.rsqrt((jnp.sum(a * a, axis=-1, keepdims=True) + rs) / MLA_QK + EPS)
        o_ref[:, h * MLA_HEAD_PAD:h * MLA_HEAD_PAD + LANES] = (a * inv * g_ref[:, :LANES]).astype(o_ref.dtype)
        rn = r * inv * g_ref[:, LANES:]
        o_ref[:, h * MLA_HEAD_PAD + LANES:(h + 1) * MLA_HEAD_PAD] = _rope_block(rn, c, s2, s1).astype(o_ref.dtype)


def _pos_block_map(groups, tm):
    bounds = []
    tok0 = 0
    for (B, S) in groups:
        assert S % tm == 0
        bounds.append((tok0 // tm, S // tm))
        tok0 += B * S

    def imap(i):
        out = (i - bounds[0][0]) % bounds[0][1]
        for (b0, per) in bounds[1:]:
            out = jnp.where(i >= b0, (i - b0) % per, out)
        return (out, 0)

    return imap


def _pad_gain(g):
    return jnp.concatenate([g.astype(F32), jnp.zeros((MLA_HEAD_PAD - MLA_QK,), F32)]).reshape(1, MLA_HEAD_PAD)


def _mla_prep(q, kv, cc, kr_block, groups, q_gain, k_gain, tm=256):
    T = q.shape[0]
    W = MLA_HEADS * MLA_HEAD_PAD
    tabs = _rope_tables(groups)
    pmap = _pos_block_map(groups, tm)
    tab_specs = [pl.BlockSpec((tm, LANES), pmap)] * 3
    row = pl.BlockSpec((tm, W), lambda i: (i, 0))
    gspec = pl.BlockSpec((1, MLA_HEAD_PAD), lambda i: (0, 0))
    qf = pl.pallas_call(
        functools.partial(_qprep_kernel, scale=MLA_QK ** -0.5 * LOG2_E),
        grid=(T // tm,),
        in_specs=[row, gspec] + tab_specs,
        out_specs=row,
        out_shape=jax.ShapeDtypeStruct((T, W), BF16),
        compiler_params=_cparams(("parallel",)),
        name="mla_qprep",
    )(q, _pad_gain(q_gain), *tabs)
    kf = pl.pallas_call(
        _kprep_kernel,
        grid=(T // tm,),
        in_specs=[row, pl.BlockSpec((tm, LANES), lambda i: (i, kr_block)), gspec] + tab_specs,
        out_specs=row,
        out_shape=jax.ShapeDtypeStruct((T, W), BF16),
        compiler_params=_cparams(("parallel",)),
        name="mla_kprep",
    )(kv, cc, _pad_gain(k_gain), *tabs)
    return qf, kf


FLASH_SUB = 512
LOG2_E = 1.4426950408889634


def _flash_kernel(q_ref, k_ref, v_ref, *rest, aliased):
    o_ref, m_sc, l_sc, acc_sc = rest[1:] if aliased else rest
    ki = pl.program_id(3)

    @pl.when(ki == 0)
    def _():
        m_sc[...] = jnp.full_like(m_sc, -jnp.inf)
        l_sc[...] = jnp.zeros_like(l_sc)
        acc_sc[...] = jnp.zeros_like(acc_sc)

    q = q_ref[...]
    nsub = k_ref.shape[0] // FLASH_SUB
    s = [_dot_nt(q, k_ref[c * FLASH_SUB:(c + 1) * FLASH_SUB, :]) for c in range(nsub)]
    m_prev = m_sc[...]
    m_new = m_prev
    for c in range(nsub):
        m_new = jnp.maximum(m_new, s[c].max(axis=-1, keepdims=True))
    alpha = jnp.exp2(m_prev - m_new)
    l = alpha * l_sc[...]
    acc = alpha * acc_sc[...]
    for c in range(nsub):
        p = jnp.exp2(s[c] - m_new)
        l = l + p.sum(axis=-1, keepdims=True)
        acc = acc + jnp.dot(p.astype(BF16), v_ref[c * FLASH_SUB:(c + 1) * FLASH_SUB, :], preferred_element_type=F32)
    l_sc[...] = l
    acc_sc[...] = acc
    m_sc[...] = m_new

    @pl.when(ki == pl.num_programs(3) - 1)
    def _():
        o_ref[...] = (acc_sc[...] / l_sc[...]).astype(o_ref.dtype)


def _mla_attention(qf, kf, kv, groups, tq=256, tk_max=16384):
    T = qf.shape[0]
    H = MLA_HEADS
    out = None
    tok0 = 0
    for (B, S) in groups:
        tk = min(tk_max, S)
        assert S % tq == 0 and S % tk == 0 and tok0 % tq == 0 and tok0 % tk == 0 and tk % FLASH_SUB == 0
        nq, nk = S // tq, S // tk
        q0, k0 = tok0 // tq, tok0 // tk
        kv_mode = pl.Buffered(1) if nk == 1 else None
        in_specs = [
            pl.BlockSpec((tq, MLA_HEAD_PAD), lambda b, h, qi, ki, q0=q0, nq=nq: (q0 + b * nq + qi, h)),
            pl.BlockSpec((tk, MLA_HEAD_PAD), lambda b, h, qi, ki, k0=k0, nk=nk: (k0 + b * nk + ki, h),
                         pipeline_mode=kv_mode),
            pl.BlockSpec((tk, LANES), lambda b, h, qi, ki, k0=k0, nk=nk: (k0 + b * nk + ki, 2 * h + 1),
                         pipeline_mode=kv_mode),
        ]
        args = [qf, kf, kv]
        aliases = {}
        if out is not None:
            in_specs.append(pl.BlockSpec(memory_space=pl.ANY))
            args.append(out)
            aliases = {3: 0}
        out = pl.pallas_call(
            functools.partial(_flash_kernel, aliased=out is not None),
            grid=(B, H, nq, nk),
            in_specs=in_specs,
            out_specs=pl.BlockSpec((tq, LANES), lambda b, h, qi, ki, q0=q0, nq=nq: (q0 + b * nq + qi, h)),
            out_shape=jax.ShapeDtypeStruct((T, H * LANES), BF16),
            scratch_shapes=[pltpu.VMEM((tq, 1), F32), pltpu.VMEM((tq, 1), F32), pltpu.VMEM((tq, LANES), F32)],
            input_output_aliases=aliases,
            compiler_params=_cparams(("parallel", "parallel", "parallel", "arbitrary")),
            name="mla_flash",
        )(*args)
        tok0 += B * S
    return out


def _peer_candidates():
    a, b = [], []
    a += [0] * 16; b += list(range(16))
    for aa in (1, 2, 3):
        a += [aa] * 8; b += list(range(8))
    for bb in (0, 1):
        a += [4, 5, 6, 7]; b += [bb] * 4
    a += list(range(8, 16)); b += [0] * 8
    for bb in (2, 3):
        a += [4, 5, 6, 7]; b += [bb] * 4
    return np.asarray(a), np.asarray(b)


_CAND_A, _CAND_B = _peer_candidates()
N_CAND = len(_CAND_A)


def _top16_sublane(s):
    n = s.shape[0]
    idx = lax.broadcasted_iota(jnp.int32, s.shape, 0).astype(F32)
    rank = jnp.full(s.shape, float(PEER_TOPK), F32)
    val = s
    tops = []
    for r in range(PEER_TOPK):
        m = val.max(axis=0, keepdims=True)
        first = jnp.where(val == m, idx, float(n)).min(axis=0, keepdims=True)
        sel = idx == first
        rank = jnp.where(sel, float(r), rank)
        val = jnp.where(sel, -jnp.inf, val)
        tops.append(m)
    return jnp.concatenate(tops, axis=0), rank


def _peer_route_kernel(x_ref, g_ref, wq_ref, keys_ref, pos_ref, amat_ref,
                       hn_ref, cnt_ref, e1_ref, rb_ref, e2_ref):
    tm = x_ref.shape[0]
    hn = _rms(x_ref[...].astype(F32), g_ref[...]).astype(BF16)
    hn_ref[...] = hn
    qry = jnp.dot(hn, wq_ref[...], preferred_element_type=F32)
    pos = pos_ref[...]
    for h in range(PEER_HEADS):
        sc, tops, ranks = [], [], []
        for half in range(2):
            c0 = (2 * h + half) * PEER_HALF
            s = _dot_nt(keys_ref[h, half], qry[:, c0:c0 + PEER_HALF].astype(BF16))
            t, r = _top16_sublane(s)
            sc.append(s); tops.append(t); ranks.append(r)
        t1, t2 = tops
        mid1 = jnp.concatenate([t1[4:8], t1[4:8]], axis=0)
        rep2 = lambda b: jnp.broadcast_to(t2[b:b + 1], (4, tm))
        cand = jnp.concatenate([
            t1[0:1] + t2, t1[1:2] + t2[0:8], t1[2:3] + t2[0:8], t1[3:4] + t2[0:8],
            mid1 + jnp.concatenate([rep2(0), rep2(1)], axis=0),
            t1[8:16] + t2[0:1],
            mid1 + jnp.concatenate([rep2(2), rep2(3)], axis=0)], axis=0)
        best = cand[0:1]
        val = cand
        chosen = jnp.zeros(cand.shape, F32)
        for _ in range(PEER_TOPK):
            m = val.max(axis=0, keepdims=True)
            first = jnp.where(val == m, pos, 1e9).min(axis=0, keepdims=True)
            sel = pos == first
            chosen = jnp.where(sel, 1.0, chosen)
            val = jnp.where(sel, -jnp.inf, val)
        z = jnp.sum(chosen * jnp.exp(cand - best), axis=0, keepdims=True)
        per_a = jnp.dot(amat_ref[...], chosen.astype(BF16), preferred_element_type=F32)
        cnt = jnp.zeros(sc[0].shape, F32)
        for a in range(PEER_TOPK):
            cnt = jnp.where(ranks[0] == float(a), per_a[a:a + 1], cnt)
        cnt_ref[h] = cnt
        e1_ref[h] = jnp.exp(sc[0] - t1[0:1]) / z
        rb_ref[h] = pltpu.bitcast(ranks[1].astype(BF16), jnp.uint32)
        e2_ref[h] = pltpu.bitcast(jnp.exp(sc[1] - t2[0:1]).astype(BF16), jnp.uint32)


def _peer_route(x, norm_gain, w_query, sub_keys, tm=256):
    T, D = x.shape
    H = PEER_HEADS
    pos = np.broadcast_to((_CAND_A * PEER_TOPK + _CAND_B).astype(np.float32)[:, None], (N_CAND, tm))
    amat = (np.arange(PEER_TOPK)[:, None] == _CAND_A[None, :]).astype(np.float32)
    gate_shape = jax.ShapeDtypeStruct((H, PEER_N_KEYS, T), F32)
    packed_shape = jax.ShapeDtypeStruct((H, PEER_N_KEYS // 2, T), jnp.uint32)
    gate_spec = pl.BlockSpec((H, PEER_N_KEYS, tm), lambda i: (0, 0, i))
    packed_spec = pl.BlockSpec((H, PEER_N_KEYS // 2, tm), lambda i: (0, 0, i))
    return pl.pallas_call(
        _peer_route_kernel,
        grid=(T // tm,),
        in_specs=[
            pl.BlockSpec((tm, D), lambda i: (i, 0)),
            pl.BlockSpec((1, D), lambda i: (0, 0)),
            pl.BlockSpec(w_query.shape, lambda i: (0, 0)),
            pl.BlockSpec(sub_keys.shape, lambda i: (0, 0, 0, 0)),
            pl.BlockSpec((N_CAND, tm), lambda i: (0, 0)),
            pl.BlockSpec((PEER_TOPK, N_CAND), lambda i: (0, 0)),
        ],
        out_specs=[pl.BlockSpec((tm, D), lambda i: (i, 0)), gate_spec, gate_spec, packed_spec, packed_spec],
        out_shape=[jax.ShapeDtypeStruct((T, D), BF16), gate_shape, gate_shape, packed_shape, packed_shape],
        compiler_params=_cparams(("parallel",)),
        name="peer_route",
    )(x, norm_gain.reshape(1, D).astype(F32), w_query, sub_keys, jnp.asarray(pos), jnp.asarray(amat, BF16))


def _gelu(a):
    return 0.5 * a * (1.0 + lax.erf(a * (2.0 ** -0.5)))


def _peer_expert_kernel(hn_ref, u_ref, vt_ref, cnt_ref, e1_ref, rb_ref, e2_ref, x_ref, o_ref, acc_ref, g_ref, w_ref,
                        *, rows_per_step):
    e = pl.program_id(1)

    @pl.when(e == 0)
    def _():
        acc_ref[...] = jnp.zeros_like(acc_ref)

    def gate_weights(step, slot):
        tile = (PEER_N_KEYS, LANES)
        half = PEER_N_KEYS // 2
        for il in range(rows_per_step):
            i = step * rows_per_step + il
            cnt = [cnt_ref[pl.ds(h * PEER_N_KEYS + i, 1), :] for h in range(PEER_HEADS)]
            e1 = [e1_ref[pl.ds(h * PEER_N_KEYS + i, 1), :] for h in range(PEER_HEADS)]
            for st in range(w_ref.shape[2] // LANES):
                ls = slice(st * LANES, (st + 1) * LANES)
                w = jnp.zeros(tile, BF16)
                for h in range(PEER_HEADS):
                    hs = slice(h * half, (h + 1) * half)
                    rb = pltpu.bitcast(rb_ref[hs, ls], BF16)
                    e2 = pltpu.bitcast(e2_ref[hs, ls], BF16)
                    cnt_t = jnp.broadcast_to(cnt[h][:, ls], tile).astype(BF16)
                    e1_t = jnp.broadcast_to(e1[h][:, ls], tile).astype(BF16)
                    w = w + jnp.where(rb < cnt_t, e2, 0.0) * e1_t
                w_ref[slot, il * half:(il + 1) * half, ls] = pltpu.bitcast(w, jnp.uint32)

    @pl.when(e == 0)
    def _():
        gate_weights(0, 0)

    gate_weights(jnp.minimum(e + 1, pl.num_programs(1) - 1), (e + 1) % 2)
    act = _gelu(_dot_nt(u_ref[...], hn_ref[...]))
    g_ref[...] = (pltpu.bitcast(w_ref[e % 2], BF16).astype(F32) * act).astype(BF16)
    acc_ref[...] += jnp.dot(vt_ref[...], g_ref[...], preferred_element_type=F32)

    @pl.when(e == pl.num_programs(1) - 1)
    def _():
        o_ref[...] = x_ref[...] + acc_ref[...].T


def _peer_experts(x, hn, gates, u, vt, tm=512, te=1024):
    T, D = x.shape
    NE = u.shape[0]
    gates = [g.reshape(-1, T) for g in gates]
    once = pl.Buffered(1)
    gate_spec = pl.BlockSpec((PEER_HEADS * PEER_N_KEYS, tm), lambda i, e: (0, i), pipeline_mode=once)
    packed_spec = pl.BlockSpec((PEER_HEADS * PEER_N_KEYS // 2, tm), lambda i, e: (0, i), pipeline_mode=once)
    return pl.pallas_call(
        functools.partial(_peer_expert_kernel, rows_per_step=te // PEER_N_KEYS),
        grid=(T // tm, NE // te),
        in_specs=[
            pl.BlockSpec((tm, D), lambda i, e: (i, 0), pipeline_mode=once),
            pl.BlockSpec((te, D), lambda i, e: (e, 0)),
            pl.BlockSpec((D, te), lambda i, e: (0, e)),
            gate_spec, gate_spec, packed_spec, packed_spec,
            pl.BlockSpec((tm, D), lambda i, e: (i, 0), pipeline_mode=once),
        ],
        out_specs=pl.BlockSpec((tm, D), lambda i, e: (i, 0)),
        out_shape=jax.ShapeDtypeStruct((T, D), F32),
        scratch_shapes=[pltpu.VMEM((D, tm), F32), pltpu.VMEM((te, tm), BF16),
                        pltpu.VMEM((2, te // 2, tm), jnp.uint32)],
        compiler_params=_cparams(("parallel", "arbitrary")),
        name="peer_experts",
    )(hn, u, vt, *gates, x)


def _peer(x, norm_gain, w_query, sub_keys, u, v):
    hn, cnt, e1, rb, e2 = _peer_route(x, norm_gain, w_query.astype(BF16), sub_keys.astype(BF16))
    return _peer_experts(x, hn, (cnt, e1, rb, e2), u.astype(BF16), v.astype(BF16).T)


def _even_layer(x, groups, norm, w_in, na_q_gain, na_k_gain, na_rpb, lb, hg_out_gain, w_out):
    p = _matmul(x, w_in.astype(BF16), gain=norm, out_dtype=F32)
    mix_cols = NA_HEADS * NA_HEAD_DIM + HG_HEADS * HG_DIM
    mix = _neighbourhood_attention(p, groups, na_rpb, na_q_gain, na_k_gain, mix_cols)
    o_fw = _hgrn2(p, groups, lb[0], reverse=False, f_col=4 * NA_HEADS)
    mix = _hgrn2(p, groups, lb[1], reverse=True, f_col=5 * NA_HEADS, o_fw=o_fw, out_gain=hg_out_gain, mix=mix)
    return _matmul(mix, w_out.astype(BF16), residual=x, out_dtype=F32)


def _odd_layer(x, groups, norm, w_in, q_norm, kv_norm, w_uq, w_ukv, q_gain, k_gain, w_out):
    D = x.shape[1]
    pad_q = (-MLA_Q_RANK) % MLA_KV_RANK
    w_cq, w_ckv, w_kr = jnp.split(w_in, [MLA_Q_RANK, MLA_Q_RANK + MLA_KV_RANK], axis=1)
    w_in_p = jnp.concatenate(
        [w_cq, jnp.zeros((D, pad_q), w_in.dtype), w_ckv, w_kr, jnp.zeros((D, LANES - MLA_ROPE), w_in.dtype)],
        axis=1).astype(BF16)
    cc = _matmul(x, w_in_p, gain=norm, out_dtype=F32, tn=w_in_p.shape[1])
    w_uq_p = jnp.pad(w_uq.reshape(MLA_Q_RANK, MLA_HEADS, MLA_QK),
                     ((0, 0), (0, 0), (0, MLA_HEAD_PAD - MLA_QK))).reshape(MLA_Q_RANK, -1).astype(BF16)
    q = _matmul(cc, w_uq_p, gain=q_norm, out_dtype=BF16, x_col_block=0)
    kv = _matmul(cc, w_ukv.astype(BF16), gain=kv_norm, out_dtype=BF16,
                 x_col_block=(MLA_Q_RANK + pad_q) // MLA_KV_RANK)
    kr_block = (MLA_Q_RANK + pad_q + MLA_KV_RANK) // LANES
    qf, kf = _mla_prep(q, kv, cc, kr_block, groups, q_gain, k_gain)
    o = _mla_attention(qf, kf, kv, groups)
    return _matmul(o, w_out.astype(BF16), residual=x, out_dtype=F32)


def _forward(xs, ev_norm, ev_w_in, ev_na_q_gain, ev_na_k_gain, ev_na_rpb, ev_hg_lb_logits, ev_hg_out_gain,
             ev_w_out, od_norm, od_w_in, od_q_norm, od_kv_norm, od_w_uq, od_w_ukv, od_q_gain, od_k_gain,
             od_w_out, ff_norm, peer_w_query, peer_sub_keys, peer_u, peer_v):
    groups = [(int(x.shape[0]), int(x.shape[1])) for x in xs]
    D = xs[0].shape[-1]
    depth = ff_norm.shape[0]
    lb_all = jnp.cumsum(jax.nn.softmax(ev_hg_lb_logits.astype(F32), axis=1), axis=1)
    x = jnp.concatenate([t.reshape(-1, D) for t in xs], axis=0)
    for layer in range(depth):
        j = layer // 2
        if layer % 2 == 0:
            x = _even_layer(x, groups, ev_norm[j], ev_w_in[j], ev_na_q_gain[j], ev_na_k_gain[j], ev_na_rpb[j],
                            lb_all[:, j], ev_hg_out_gain[j], ev_w_out[j])
        else:
            x = _odd_layer(x, groups, od_norm[j], od_w_in[j], od_q_norm[j], od_kv_norm[j], od_w_uq[j],
                           od_w_ukv[j], od_q_gain[j], od_k_gain[j], od_w_out[j])
        x = _peer(x, ff_norm[layer], peer_w_query[layer], peer_sub_keys[layer], peer_u[layer], peer_v[layer])
    outs = []
    tok0 = 0
    for (B, S) in groups:
        outs.append(x[tok0:tok0 + B * S].reshape(B, S, D))
        tok0 += B * S
    return tuple(outs)


def kernel(x_prompt, x_sample, ev_norm, ev_w_in, ev_na_q_gain, ev_na_k_gain, ev_na_rpb, ev_hg_lb_logits, ev_hg_out_gain, ev_w_out, od_norm, od_w_in, od_q_norm, od_kv_norm, od_w_uq, od_w_ukv, od_q_gain, od_k_gain, od_w_out, ff_norm, peer_w_query, peer_sub_keys, peer_u, peer_v):
    return _forward((x_prompt, x_sample), ev_norm, ev_w_in, ev_na_q_gain, ev_na_k_gain, ev_na_rpb, ev_hg_lb_logits,
                    ev_hg_out_gain, ev_w_out, od_norm, od_w_in, od_q_norm, od_kv_norm, od_w_uq, od_w_ukv, od_q_gain,
                    od_k_gain, od_w_out, ff_norm, peer_w_query, peer_sub_keys, peer_u, peer_v)
```

```python
import functools

import numpy as np
import jax
import jax.numpy as jnp
from jax import lax
from jax.experimental import pallas as pl
from jax.experimental.pallas import tpu as pltpu

F32 = jnp.float32
BF16 = jnp.bfloat16

EPS = 1e-6
LANES = 128
VMEM_LIMIT_BYTES = 56 * 1024 * 1024
NEG_BIG = -1e30

GRID_W = 64
NA_HEADS = 8
NA_HEAD_DIM = 128
NA_WIN_ROWS = 8
NA_WIN_COLS = 16
NA_ROW_BLOCK = 8
NA_KEY_ROWS = 16
HG_HEADS = 8
HG_DIM = 128
HG_CHUNK = 32
MLA_HEADS = 16
MLA_Q_RANK = 768
MLA_KV_RANK = 512
MLA_NOPE = 128
MLA_ROPE = 64
MLA_QK = MLA_NOPE + MLA_ROPE
MLA_HEAD_PAD = 256
ROPE_BASE = 10000.0
PEER_HEADS = 8
PEER_N_KEYS = 128
PEER_TOPK = 16
PEER_HALF = 128


def _cparams(sem):
    return pltpu.CompilerParams(dimension_semantics=sem, vmem_limit_bytes=VMEM_LIMIT_BYTES)


def _rms(x, gain):
    return x * lax.rsqrt(jnp.mean(x * x, axis=-1, keepdims=True) + EPS) * gain


def _dot_nt(a, b):
    return lax.dot_general(a, b, (((1,), (1,)), ((), ())), preferred_element_type=F32)


def _dot_tn(a, b):
    return lax.dot_general(a, b, (((0,), (0,)), ((), ())), preferred_element_type=F32)


def _mm_kernel(*refs, has_norm, has_res):
    it = iter(refs)
    x_ref = next(it)
    g_ref = next(it) if has_norm else None
    w_ref = next(it)
    r_ref = next(it) if has_res else None
    o_ref = next(it)
    if has_norm:
        hn_ref = next(it)

        @pl.when(pl.program_id(1) == 0)
        def _():
            hn_ref[...] = _rms(x_ref[...].astype(F32), g_ref[...]).astype(BF16)

        a = hn_ref[...]
    else:
        a = x_ref[...].astype(BF16)
    acc = jnp.dot(a, w_ref[...], preferred_element_type=F32)
    if has_res:
        acc = acc + r_ref[...]
    o_ref[...] = acc.astype(o_ref.dtype)


def _matmul(x, w, *, gain=None, residual=None, out_dtype=F32, x_col_block=0, tm=512, tn=1024):
    M = x.shape[0]
    K, N = w.shape
    tn = min(tn, N)
    assert M % tm == 0 and N % tn == 0 and (x_col_block + 1) * K <= x.shape[1]
    has_norm, has_res = gain is not None, residual is not None
    in_specs = [pl.BlockSpec((tm, K), lambda i, j: (i, x_col_block))]
    args = [x]
    if has_norm:
        in_specs.append(pl.BlockSpec((1, K), lambda i, j: (0, 0)))
        args.append(gain.reshape(1, K).astype(F32))
    in_specs.append(pl.BlockSpec((K, tn), lambda i, j: (0, j)))
    args.append(w)
    if has_res:
        in_specs.append(pl.BlockSpec((tm, tn), lambda i, j: (i, j)))
        args.append(residual)
    return pl.pallas_call(
        functools.partial(_mm_kernel, has_norm=has_norm, has_res=has_res),
        grid=(M // tm, N // tn),
        in_specs=in_specs,
        out_specs=pl.BlockSpec((tm, tn), lambda i, j: (i, j)),
        out_shape=jax.ShapeDtypeStruct((M, N), out_dtype),
        scratch_shapes=[pltpu.VMEM((tm, K), BF16)] if has_norm else [],
        compiler_params=_cparams(("parallel", "arbitrary")),
        name="mm",
    )(*args)


NA_Q = NA_ROW_BLOCK * GRID_W
NA_KB = 256
NA_NKB = NA_KEY_ROWS * GRID_W // NA_KB


def _na_bias_tables(rpb):
    i = np.arange(NA_ROW_BLOCK)[:, None, None, None]
    c = np.arange(GRID_W)[None, :, None, None]
    j = np.arange(NA_KEY_ROWS)[None, None, :, None]
    kc = np.arange(GRID_W)[None, None, None, :]
    half = NA_WIN_ROWS // 2
    qc0 = np.clip(c - NA_WIN_COLS // 2, 0, GRID_W - NA_WIN_COLS)
    col_ok = (kc >= qc0) & (kc < qc0 + NA_WIN_COLS)
    dcol = np.clip(kc - c + NA_WIN_COLS - 1, 0, 2 * NA_WIN_COLS - 2)
    col_hot = (dcol[0, :, 0, :, None] == np.arange(2 * NA_WIN_COLS - 1)).astype(np.float32)
    by_col = jnp.einsum('hab,ckb->hack', rpb.astype(F32), col_hot, precision=lax.Precision.HIGHEST)
    tabs = []
    for var in range(3):
        if var == 0:
            r0 = np.maximum(i - half, 0)
            dr = j - i
        elif var == 1:
            r0 = i
            dr = j - half - i
        else:
            r0 = np.minimum(i + half, NA_KEY_ROWS - NA_WIN_ROWS)
            dr = j - (NA_KEY_ROWS - NA_ROW_BLOCK) - i
        ok = (j >= r0) & (j < r0 + NA_WIN_ROWS) & col_ok
        dri = np.clip(dr + NA_WIN_ROWS - 1, 0, 2 * NA_WIN_ROWS - 2)[:, 0, :, 0]
        row_hot = (dri[:, :, None] == np.arange(2 * NA_WIN_ROWS - 1)).astype(np.float32)
        vals = jnp.einsum('ija,hack->hicjk', row_hot, by_col, precision=lax.Precision.HIGHEST)
        tabs.append(jnp.where(jnp.asarray(ok)[None], vals, NEG_BIG).reshape(
            NA_HEADS, NA_Q, NA_KEY_ROWS * GRID_W))
    return jnp.stack(tabs)


def _na_steps(groups):
    steps = []
    tok0 = 0
    for (B, S) in groups:
        rows = S // GRID_W
        nrb = rows // NA_ROW_BLOCK
        assert nrb >= 3 and S % NA_Q == 0
        for b in range(B):
            for rb in range(nrb):
                var = 0 if rb == 0 else (2 if rb == nrb - 1 else 1)
                kr0 = min(max(NA_ROW_BLOCK * rb - NA_WIN_ROWS // 2, 0), rows - NA_KEY_ROWS)
                steps.append((var, (tok0 + b * S) // NA_Q + rb, (tok0 + b * S + kr0 * GRID_W) // NA_KB))
        tok0 += B * S
    steps.sort(key=lambda t: t[0])
    arr = np.asarray(steps, np.int32)
    return arr[:, 0], arr[:, 1], arr[:, 2]


def _na_kernel(var_ref, qb_ref, kb_ref, q_ref, *rest):
    k_refs = rest[:NA_NKB]
    v_refs = rest[NA_NKB:2 * NA_NKB]
    bias_ref, qg_ref, kg_ref, o_ref = rest[2 * NA_NKB:]
    scale = NA_HEAD_DIM ** -0.5
    q = _rms(q_ref[...].astype(F32), qg_ref[...]).astype(BF16)
    s = []
    for m in range(NA_NKB):
        k = _rms(k_refs[m][...].astype(F32), kg_ref[...]).astype(BF16)
        s.append(_dot_nt(q, k) * scale + bias_ref[0, 0, :, m * NA_KB:(m + 1) * NA_KB])
    mx = s[0].max(axis=-1, keepdims=True)
    for m in range(1, NA_NKB):
        mx = jnp.maximum(mx, s[m].max(axis=-1, keepdims=True))
    l = jnp.zeros_like(mx)
    acc = jnp.zeros((NA_Q, NA_HEAD_DIM), F32)
    for m in range(NA_NKB):
        p = jnp.exp(s[m] - mx)
        l = l + p.sum(axis=-1, keepdims=True)
        acc = acc + jnp.dot(p.astype(BF16), v_refs[m][...].astype(BF16), preferred_element_type=F32)
    o_ref[...] = (acc / l).astype(o_ref.dtype)


def _neighbourhood_attention(p, groups, rpb, q_gain, k_gain, out_cols):
    T = p.shape[0]
    var, qblk, kblk = _na_steps(groups)
    tabs = _na_bias_tables(rpb)
    H = NA_HEADS

    def qmap(h, n, var, qb, kb):
        return (qb[n], h)

    def kmap(m, off):
        return lambda h, n, var, qb, kb: (kb[n] + m, off + h)

    in_specs = [pl.BlockSpec((NA_Q, NA_HEAD_DIM), qmap)]
    in_specs += [pl.BlockSpec((NA_KB, NA_HEAD_DIM), kmap(m, H)) for m in range(NA_NKB)]
    in_specs += [pl.BlockSpec((NA_KB, NA_HEAD_DIM), kmap(m, 2 * H)) for m in range(NA_NKB)]
    in_specs += [
        pl.BlockSpec((1, 1, NA_Q, NA_KEY_ROWS * GRID_W), lambda h, n, var, qb, kb: (var[n], h, 0, 0)),
        pl.BlockSpec((1, NA_HEAD_DIM), lambda h, n, var, qb, kb: (0, 0)),
        pl.BlockSpec((1, NA_HEAD_DIM), lambda h, n, var, qb, kb: (0, 0)),
    ]
    return pl.pallas_call(
        _na_kernel,
        grid_spec=pltpu.PrefetchScalarGridSpec(
            num_scalar_prefetch=3,
            grid=(H, len(var)),
            in_specs=in_specs,
            out_specs=pl.BlockSpec((NA_Q, NA_HEAD_DIM), qmap),
        ),
        out_shape=jax.ShapeDtypeStruct((T, out_cols), BF16),
        compiler_params=_cparams(("parallel", "arbitrary")),
        name="natten",
    )(jnp.asarray(var), jnp.asarray(qblk), jnp.asarray(kblk), p, *([p] * (2 * NA_NKB)), tabs,
      q_gain.reshape(1, -1).astype(F32), k_gain.reshape(1, -1).astype(F32))


HG_BLOCK = 512
HG_NCHUNK = HG_BLOCK // HG_CHUNK
HG_SUB = 8
assert HG_CHUNK == 4 * HG_SUB


def _hg_steps(groups, reverse):
    blk, first = [], []
    tok0 = 0
    for (B, S) in groups:
        nb = S // HG_BLOCK
        assert S % HG_BLOCK == 0
        for b in range(B):
            order = range(nb - 1, -1, -1) if reverse else range(nb)
            for n, sb in enumerate(order):
                blk.append((tok0 + b * S) // HG_BLOCK + sb)
                first.append(1 if n == 0 else 0)
        tok0 += B * S
    return np.asarray(blk, np.int32), np.asarray(first, np.int32)


def _hg_matrices(reverse):
    C, S = HG_CHUNK, HG_SUB
    t = np.arange(C)[:, None]
    u = np.arange(C)[None, :]
    upto = (u >= t) if reverse else (u <= t)
    after = (u < t) if reverse else (u > t)
    same = lambda m: (u // m) == (t // m)
    sums = np.concatenate([upto, upto & same(S), upto & same(2 * S), after & same(S), after & same(2 * S)], axis=0)
    query_side = 0 if reverse else 1
    masks = []
    for m in (S, 2 * S):
        bt, bs = t // m, u // m
        masks.append((bt % 2 == query_side) & (bs == (bt + 1 if reverse else bt - 1)))
    return sums.astype(np.float32), np.stack(masks).astype(np.float32)


def _hg_kernel(blk_ref, first_ref, q_ref, f_ref, i_ref, lb_ref, sums_ref, mask_ref, *rest, reverse, finish):
    if finish:
        ofw_ref, g_ref, gain_ref, o_ref, st_ref = rest
    else:
        o_ref, st_ref = rest
    h = pl.program_id(0)
    n = pl.program_id(1)
    C = HG_CHUNK

    @pl.when(first_ref[n] == 1)
    def _():
        st_ref[...] = jnp.zeros_like(st_ref)

    lb = lb_ref[pl.ds(h, 1), :]
    sums = sums_ref[...]
    mask0, mask1 = mask_ref[0], mask_ref[1]
    S = HG_SUB
    NC = HG_NCHUNK
    TB = HG_BLOCK
    blk3 = (TB // S, S, HG_DIM)
    row8 = lax.broadcasted_iota(jnp.int32, blk3, 1)
    ones = jnp.ones((HG_DIM, HG_DIM), BF16)
    rows = lambda x, c: x[c * C:(c + 1) * C]

    f = lb + (1.0 - lb) * jax.nn.sigmoid(f_ref[...].astype(F32))
    logf = jnp.log(f)
    kk = 1.0 - f
    q = q_ref[...].astype(F32)
    q = q * jax.nn.sigmoid(q)
    v = i_ref[...].astype(F32)
    logf_hi = logf.astype(BF16)
    logf_lo = (logf - logf_hi.astype(F32)).astype(BF16)
    e_cat = (jnp.dot(sums, jnp.concatenate([rows(logf_hi, c) for c in range(NC)], axis=1), preferred_element_type=F32)
             + jnp.dot(sums, jnp.concatenate([rows(logf_lo, c) for c in range(NC)], axis=1),
                       preferred_element_type=F32))
    part = lambda r: jnp.concatenate(
        [e_cat[r * C:(r + 1) * C, c * HG_DIM:(c + 1) * HG_DIM] for c in range(NC)], axis=0)
    b, bloc, eq1, ek0, ek1 = [part(r) for r in range(5)]
    b_end = jnp.concatenate(
        [jnp.broadcast_to(rows(b, c)[0:1] if reverse else rows(b, c)[C - 1:C], (C, HG_DIM)) for c in range(NC)], axis=0)
    qb = (q * jnp.exp(b)).astype(BF16)
    q0, k0 = (q * jnp.exp(bloc)).astype(BF16), (kk * jnp.exp(ek0)).astype(BF16)
    q1, k1 = (q * jnp.exp(eq1)).astype(BF16), (kk * jnp.exp(ek1)).astype(BF16)
    ks = (kk * jnp.exp(b_end - b)).astype(BF16)
    vb = v.astype(BF16)
    o_off, upd = [], []
    for c in range(NC):
        a_off = (_dot_nt(rows(q0, c), rows(k0, c)) * mask0 + _dot_nt(rows(q1, c), rows(k1, c)) * mask1)
        o_off.append(jnp.dot(a_off.astype(BF16), rows(vb, c), preferred_element_type=F32))
        upd.append(_dot_tn(rows(vb, c), rows(ks, c)))
    q3, k3, v3, b3 = q.reshape(blk3), kk.reshape(blk3), v.reshape(blk3), bloc.reshape(blk3)
    prods = []
    for ds in range(S):
        keep = (row8 <= ds) if reverse else (row8 >= ds)
        e = jnp.exp(jnp.where(keep, b3 - b3[:, ds:ds + 1, :], -jnp.inf))
        prods.append((q3 * e * k3[:, ds:ds + 1, :]).reshape(TB, HG_DIM).astype(BF16))
    a_bc = jnp.dot(jnp.concatenate(prods, axis=0), ones, preferred_element_type=F32)
    o_diag = jnp.zeros(blk3, F32)
    for ds in range(S):
        o_diag = o_diag + a_bc[ds * TB:(ds + 1) * TB, :].reshape(blk3) * v3[:, ds:ds + 1, :]
    st = st_ref[...]
    o_inter = [None] * NC
    for cc in range(NC):
        c = NC - 1 - cc if reverse else cc
        o_inter[c] = _dot_nt(rows(qb, c), st.astype(BF16))
        st = st * jnp.exp(rows(b_end, c)[0:1]) + upd[c]
    st_ref[...] = st
    o_blk = jnp.concatenate([o_inter[c] + o_off[c] for c in range(NC)], axis=0) + o_diag.reshape(TB, HG_DIM)

    if finish:
        o = o_blk + ofw_ref[...]
        g = g_ref[...].astype(F32)
        o_ref[...] = (_rms(o, gain_ref[...]) * (g * jax.nn.sigmoid(g))).astype(o_ref.dtype)
    else:
        o_ref[...] = o_blk


def _hgrn2(p, groups, lb, *, reverse, f_col, o_fw=None, out_gain=None, mix=None):
    T = p.shape[0]
    blk, first = _hg_steps(groups, reverse)
    H = HG_HEADS
    finish = o_fw is not None
    C = HG_CHUNK
    sums, masks = _hg_matrices(reverse)

    def col(off):
        return lambda h, n, blk, first: (blk[n], off + h)

    const = lambda h, n, blk, first: (0, 0)
    in_specs = [
        pl.BlockSpec((HG_BLOCK, HG_DIM), col(3 * NA_HEADS)),
        pl.BlockSpec((HG_BLOCK, HG_DIM), col(f_col)),
        pl.BlockSpec((HG_BLOCK, HG_DIM), col(6 * NA_HEADS)),
        pl.BlockSpec((H, HG_DIM), const),
        pl.BlockSpec(sums.shape, const),
        pl.BlockSpec(masks.shape, lambda h, n, blk, first: (0, 0, 0)),
    ]
    args = [p, p, p, lb.reshape(H, HG_DIM).astype(F32), jnp.asarray(sums, BF16), jnp.asarray(masks)]
    aliases = {}
    if finish:
        in_specs += [
            pl.BlockSpec((HG_BLOCK, HG_DIM), col(0)),
            pl.BlockSpec((HG_BLOCK, HG_DIM), col(7 * NA_HEADS)),
            pl.BlockSpec((1, HG_DIM), const),
            pl.BlockSpec(memory_space=pl.ANY),
        ]
        args += [o_fw, p, out_gain.reshape(1, HG_DIM).astype(F32), mix]
        aliases = {2 + len(args) - 1: 0}
        out_spec = pl.BlockSpec((HG_BLOCK, HG_DIM), col(NA_HEADS))
        out_shape = jax.ShapeDtypeStruct(mix.shape, mix.dtype)
    else:
        out_spec = pl.BlockSpec((HG_BLOCK, HG_DIM), col(0))
        out_shape = jax.ShapeDtypeStruct((T, H * HG_DIM), F32)

    def body(blk_ref, first_ref, *refs):
        if finish:
            refs = refs[:9] + refs[10:]
        _hg_kernel(blk_ref, first_ref, *refs, reverse=reverse, finish=finish)

    return pl.pallas_call(
        body,
        grid_spec=pltpu.PrefetchScalarGridSpec(
            num_scalar_prefetch=2,
            grid=(H, len(blk)),
            in_specs=in_specs,
            out_specs=out_spec,
            scratch_shapes=[
                pltpu.VMEM((HG_DIM, HG_DIM), F32),
            ],
        ),
        out_shape=out_shape,
        input_output_aliases=aliases,
        compiler_params=_cparams(("parallel", "arbitrary")),
        name="hgrn2_bw" if reverse else "hgrn2_fw",
    )(jnp.asarray(blk), jnp.asarray(first), *args)


def _rope_tables(groups):
    half = MLA_ROPE // 2
    smax = max(S for _, S in groups)
    pos = jnp.arange(smax, dtype=F32)
    inv = 1.0 / (ROPE_BASE ** (jnp.arange(0, MLA_ROPE, 2, dtype=F32) / MLA_ROPE))
    ang = pos[:, None] * inv[None, :]
    cos, sin = jnp.cos(ang), jnp.sin(ang)
    z = jnp.zeros((smax, LANES - 2 * half), F32)
    zh = jnp.zeros((smax, half), F32)
    c = jnp.concatenate([cos, cos, z], axis=1)
    s_from_x2 = jnp.concatenate([-sin, zh, z], axis=1)
    s_from_x1 = jnp.concatenate([zh, sin, z], axis=1)
    return c, s_from_x2, s_from_x1


def _rope_block(t, c, s2, s1):
    half = MLA_ROPE // 2
    return t * c + pltpu.roll(t, LANES - half, 1) * s2 + pltpu.roll(t, half, 1) * s1


def _qprep_kernel(q_ref, g_ref, c_ref, s2_ref, s1_ref, o_ref, *, scale):
    c, s2, s1 = c_ref[...], s2_ref[...], s1_ref[...]
    for h in range(MLA_HEADS):
        a = q_ref[:, h * MLA_HEAD_PAD:h * MLA_HEAD_PAD + LANES].astype(F32)
        r = q_ref[:, h * MLA_HEAD_PAD + LANES:(h + 1) * MLA_HEAD_PAD].astype(F32)
        ms = (jnp.sum(a * a, axis=-1, keepdims=True) + jnp.sum(r * r, axis=-1, keepdims=True)) / MLA_QK
        inv = lax.rsqrt(ms + EPS)
        o_ref[:, h * MLA_HEAD_PAD:h * MLA_HEAD_PAD + LANES] = (a * inv * g_ref[:, :LANES] * scale).astype(o_ref.dtype)
        rn = r * inv * g_ref[:, LANES:]
        o_ref[:, h * MLA_HEAD_PAD + LANES:(h + 1) * MLA_HEAD_PAD] = (_rope_block(rn, c, s2, s1) * scale).astype(o_ref.dtype)


def _kprep_kernel(kv_ref, kr_ref, g_ref, c_ref, s2_ref, s1_ref, o_ref):
    c, s2, s1 = c_ref[...], s2_ref[...], s1_ref[...]
    r = kr_ref[...].astype(F32)
    rs = jnp.sum(r * r, axis=-1, keepdims=True)
    for h in range(MLA_HEADS):
        a = kv_ref[:, h * MLA_HEAD_PAD:h * MLA_HEAD_PAD + LANES].astype(F32)
        inv = lax.rsqrt((jnp.sum(a * a, axis=-1, keepdims=True) + rs) / MLA_QK + EPS)
        o_ref[:, h * MLA_HEAD_PAD:h * MLA_HEAD_PAD + LANES] = (a * inv * g_ref[:, :LANES]).astype(o_ref.dtype)
        rn = r * inv * g_ref[:, LANES:]
        o_ref[:, h * MLA_HEAD_PAD + LANES:(h + 1) * MLA_HEAD_PAD] = _rope_block(rn, c, s2, s1).astype(o_ref.dtype)


def _pos_block_map(groups, tm):
    bounds = []
    tok0 = 0
    for (B, S) in groups:
        assert S % tm == 0
        bounds.append((tok0 // tm, S // tm))
        tok0 += B * S

    def imap(i):
        out = (i - bounds[0][0]) % bounds[0][1]
        for (b0, per) in bounds[1:]:
            out = jnp.where(i >= b0, (i - b0) % per, out)
        return (out, 0)

    return imap


def _pad_gain(g):
    return jnp.concatenate([g.astype(F32), jnp.zeros((MLA_HEAD_PAD - MLA_QK,), F32)]).reshape(1, MLA_HEAD_PAD)


def _mla_prep(q, kv, cc, kr_block, groups, q_gain, k_gain, tm=256):
    T = q.shape[0]
    W = MLA_HEADS * MLA_HEAD_PAD
    tabs = _rope_tables(groups)
    pmap = _pos_block_map(groups, tm)
    tab_specs = [pl.BlockSpec((tm, LANES), pmap)] * 3
    row = pl.BlockSpec((tm, W), lambda i: (i, 0))
    gspec = pl.BlockSpec((1, MLA_HEAD_PAD), lambda i: (0, 0))
    qf = pl.pallas_call(
        functools.partial(_qprep_kernel, scale=MLA_QK ** -0.5 * LOG2_E),
        grid=(T // tm,),
        in_specs=[row, gspec] + tab_specs,
        out_specs=row,
        out_shape=jax.ShapeDtypeStruct((T, W), BF16),
        compiler_params=_cparams(("parallel",)),
        name="mla_qprep",
    )(q, _pad_gain(q_gain), *tabs)
    kf = pl.pallas_call(
        _kprep_kernel,
        grid=(T // tm,),
        in_specs=[row, pl.BlockSpec((tm, LANES), lambda i: (i, kr_block)), gspec] + tab_specs,
        out_specs=row,
        out_shape=jax.ShapeDtypeStruct((T, W), BF16),
        compiler_params=_cparams(("parallel",)),
        name="mla_kprep",
    )(kv, cc, _pad_gain(k_gain), *tabs)
    return qf, kf


FLASH_SUB = 512
LOG2_E = 1.4426950408889634


def _flash_kernel(q_ref, k_ref, v_ref, *rest, aliased):
    o_ref, m_sc, l_sc, acc_sc = rest[1:] if aliased else rest
    ki = pl.program_id(3)

    @pl.when(ki == 0)
    def _():
        m_sc[...] = jnp.full_like(m_sc, -jnp.inf)
        l_sc[...] = jnp.zeros_like(l_sc)
        acc_sc[...] = jnp.zeros_like(acc_sc)

    q = q_ref[...]
    nsub = k_ref.shape[0] // FLASH_SUB
    s = [_dot_nt(q, k_ref[c * FLASH_SUB:(c + 1) * FLASH_SUB, :]) for c in range(nsub)]
    m_prev = m_sc[...]
    m_new = m_prev
    for c in range(nsub):
        m_new = jnp.maximum(m_new, s[c].max(axis=-1, keepdims=True))
    alpha = jnp.exp2(m_prev - m_new)
    l = alpha * l_sc[...]
    acc = alpha * acc_sc[...]
    for c in range(nsub):
        p = jnp.exp2(s[c] - m_new)
        l = l + p.sum(axis=-1, keepdims=True)
        acc = acc + jnp.dot(p.astype(BF16), v_ref[c * FLASH_SUB:(c + 1) * FLASH_SUB, :], preferred_element_type=F32)
    l_sc[...] = l
    acc_sc[...] = acc
    m_sc[...] = m_new

    @pl.when(ki == pl.num_programs(3) - 1)
    def _():
        o_ref[...] = (acc_sc[...] / l_sc[...]).astype(o_ref.dtype)


def _mla_attention(qf, kf, kv, groups, tq=256, tk_max=16384):
    T = qf.shape[0]
    H = MLA_HEADS
    out = None
    tok0 = 0
    for (B, S) in groups:
        tk = min(tk_max, S)
        assert S % tq == 0 and S % tk == 0 and tok0 % tq == 0 and tok0 % tk == 0 and tk % FLASH_SUB == 0
        nq, nk = S // tq, S // tk
        q0, k0 = tok0 // tq, tok0 // tk
        kv_mode = pl.Buffered(1) if nk == 1 else None
        in_specs = [
            pl.BlockSpec((tq, MLA_HEAD_PAD), lambda b, h, qi, ki, q0=q0, nq=nq: (q0 + b * nq + qi, h)),
            pl.BlockSpec((tk, MLA_HEAD_PAD), lambda b, h, qi, ki, k0=k0, nk=nk: (k0 + b * nk + ki, h),
                         pipeline_mode=kv_mode),
            pl.BlockSpec((tk, LANES), lambda b, h, qi, ki, k0=k0, nk=nk: (k0 + b * nk + ki, 2 * h + 1),
                         pipeline_mode=kv_mode),
        ]
        args = [qf, kf, kv]
        aliases = {}
        if out is not None:
            in_specs.append(pl.BlockSpec(memory_space=pl.ANY))
            args.append(out)
            aliases = {3: 0}
        out = pl.pallas_call(
            functools.partial(_flash_kernel, aliased=out is not None),
            grid=(B, H, nq, nk),
            in_specs=in_specs,
            out_specs=pl.BlockSpec((tq, LANES), lambda b, h, qi, ki, q0=q0, nq=nq: (q0 + b * nq + qi, h)),
            out_shape=jax.ShapeDtypeStruct((T, H * LANES), BF16),
            scratch_shapes=[pltpu.VMEM((tq, 1), F32), pltpu.VMEM((tq, 1), F32), pltpu.VMEM((tq, LANES), F32)],
            input_output_aliases=aliases,
            compiler_params=_cparams(("parallel", "parallel", "parallel", "arbitrary")),
            name="mla_flash",
        )(*args)
        tok0 += B * S
    return out


def _peer_candidates():
    a, b = [], []
    a += [0] * 16; b += list(range(16))
    for aa in (1, 2, 3):
        a += [aa] * 8; b += list(range(8))
    for bb in (0, 1):
        a += [4, 5, 6, 7]; b += [bb] * 4
    a += list(range(8, 16)); b += [0] * 8
    for bb in (2, 3):
        a += [4, 5, 6, 7]; b += [bb] * 4
    return np.asarray(a), np.asarray(b)


_CAND_A, _CAND_B = _peer_candidates()
N_CAND = len(_CAND_A)


def _extract16(val, order, exact):
    step = jnp.full(val.shape, float(PEER_TOPK), F32)
    tops = []
    for r in range(PEER_TOPK):
        m = val.max(axis=0, keepdims=True)
        if exact:
            first = jnp.where(val == m, order, 1e9).min(axis=0, keepdims=True)
            sel = order == first
        else:
            sel = val == m
        step = jnp.where(sel, float(r), step)
        val = jnp.where(sel, -jnp.inf, val)
        tops.append(m)
    return jnp.concatenate(tops, axis=0), step


def _route_head(s1, s2, pos, amat, exact):
    tm = s1.shape[1]
    idx = lax.broadcasted_iota(jnp.int32, s1.shape, 0).astype(F32)
    t1, r1 = _extract16(s1, idx, exact)
    t2, r2 = _extract16(s2, idx, exact)
    mid1 = jnp.concatenate([t1[4:8], t1[4:8]], axis=0)
    rep2 = lambda b: jnp.broadcast_to(t2[b:b + 1], (4, tm))
    cand = jnp.concatenate([
        t1[0:1] + t2, t1[1:2] + t2[0:8], t1[2:3] + t2[0:8], t1[3:4] + t2[0:8],
        mid1 + jnp.concatenate([rep2(0), rep2(1)], axis=0),
        t1[8:16] + t2[0:1],
        mid1 + jnp.concatenate([rep2(2), rep2(3)], axis=0)], axis=0)
    best = cand[0:1]
    _, cstep = _extract16(cand, pos, exact)
    chosen = (cstep < float(PEER_TOPK)).astype(F32)
    z = jnp.sum(chosen * jnp.exp(cand - best), axis=0, keepdims=True)
    per_a = jnp.dot(amat, chosen.astype(BF16), preferred_element_type=F32)
    cnt = jnp.zeros(s1.shape, F32)
    for a in range(PEER_TOPK):
        cnt = jnp.where(r1 == float(a), per_a[a:a + 1], cnt)
    e1n = jnp.exp(s1 - t1[0:1]) / z
    e2 = jnp.exp(s2 - t2[0:1])
    removed = lambda st: jnp.sum((st < float(PEER_TOPK)).astype(F32), axis=0, keepdims=True)
    ties = (jnp.abs(removed(r1) - PEER_TOPK) + jnp.abs(removed(r2) - PEER_TOPK)
            + jnp.abs(jnp.sum(chosen, axis=0, keepdims=True) - PEER_TOPK))
    return cnt, e1n, r2, e2, ties


def _peer_route_kernel(x_ref, g_ref, wq_ref, keys_ref, pos_ref, amat_ref,
                       hn_ref, cnt_ref, e1_ref, rb_ref, e2_ref, sc_ref):
    tm = x_ref.shape[0]
    hn = _rms(x_ref[...].astype(F32), g_ref[...]).astype(BF16)
    hn_ref[...] = hn
    qry = jnp.dot(hn, wq_ref[...], preferred_element_type=F32)
    for hp in range(2 * PEER_HEADS):
        sc_ref[hp] = _dot_nt(keys_ref[hp // 2, hp % 2], qry[:, hp * PEER_HALF:(hp + 1) * PEER_HALF].astype(BF16))

    def route(exact):
        ties = jnp.zeros((1, tm), F32)
        for h in range(PEER_HEADS):
            cnt, e1n, r2, e2, t = _route_head(sc_ref[2 * h], sc_ref[2 * h + 1], pos_ref[...], amat_ref[...], exact)
            cnt_ref[h] = cnt
            e1_ref[h] = e1n
            rb_ref[h] = pltpu.bitcast(r2.astype(BF16), jnp.uint32)
            e2_ref[h] = pltpu.bitcast(e2.astype(BF16), jnp.uint32)
            ties = ties + t
        return ties

    ties = route(exact=False)

    @pl.when(jnp.max(ties) > 0.0)
    def _():
        route(exact=True)


def _peer_route(x, norm_gain, w_query, sub_keys, tm=256):
    T, D = x.shape
    H = PEER_HEADS
    pos = np.broadcast_to((_CAND_A * PEER_TOPK + _CAND_B).astype(np.float32)[:, None], (N_CAND, tm))
    amat = (np.arange(PEER_TOPK)[:, None] == _CAND_A[None, :]).astype(np.float32)
    gate_shape = jax.ShapeDtypeStruct((H, PEER_N_KEYS, T), F32)
    packed_shape = jax.ShapeDtypeStruct((H, PEER_N_KEYS // 2, T), jnp.uint32)
    gate_spec = pl.BlockSpec((H, PEER_N_KEYS, tm), lambda i: (0, 0, i))
    packed_spec = pl.BlockSpec((H, PEER_N_KEYS // 2, tm), lambda i: (0, 0, i))
    return pl.pallas_call(
        _peer_route_kernel,
        grid=(T // tm,),
        in_specs=[
            pl.BlockSpec((tm, D), lambda i: (i, 0)),
            pl.BlockSpec((1, D), lambda i: (0, 0)),
            pl.BlockSpec(w_query.shape, lambda i: (0, 0)),
            pl.BlockSpec(sub_keys.shape, lambda i: (0, 0, 0, 0)),
            pl.BlockSpec((N_CAND, tm), lambda i: (0, 0)),
            pl.BlockSpec((PEER_TOPK, N_CAND), lambda i: (0, 0)),
        ],
        out_specs=[pl.BlockSpec((tm, D), lambda i: (i, 0)), gate_spec, gate_spec, packed_spec, packed_spec],
        out_shape=[jax.ShapeDtypeStruct((T, D), BF16), gate_shape, gate_shape, packed_shape, packed_shape],
        scratch_shapes=[pltpu.VMEM((2 * H, PEER_N_KEYS, tm), F32)],
        compiler_params=_cparams(("parallel",)),
        name="peer_route",
    )(x, norm_gain.reshape(1, D).astype(F32), w_query, sub_keys, jnp.asarray(pos), jnp.asarray(amat, BF16))


def _gelu(a):
    return 0.5 * a * (1.0 + lax.erf(a * (2.0 ** -0.5)))


def _peer_expert_kernel(hn_ref, u_ref, vt_ref, cnt_ref, e1_ref, rb_ref, e2_ref, x_ref, o_ref, acc_ref, g_ref, w_ref,
                        *, rows_per_step):
    e = pl.program_id(1)

    @pl.when(e == 0)
    def _():
        acc_ref[...] = jnp.zeros_like(acc_ref)

    def gate_weights(step, slot):
        tile = (PEER_N_KEYS, LANES)
        half = PEER_N_KEYS // 2
        for il in range(rows_per_step):
            i = step * rows_per_step + il
            cnt = [cnt_ref[pl.ds(h * PEER_N_KEYS + i, 1), :] for h in range(PEER_HEADS)]
            e1 = [e1_ref[pl.ds(h * PEER_N_KEYS + i, 1), :] for h in range(PEER_HEADS)]
            for st in range(w_ref.shape[2] // LANES):
                ls = slice(st * LANES, (st + 1) * LANES)
                w = jnp.zeros(tile, BF16)
                for h in range(PEER_HEADS):
                    hs = slice(h * half, (h + 1) * half)
                    rb = pltpu.bitcast(rb_ref[hs, ls], BF16)
                    e2 = pltpu.bitcast(e2_ref[hs, ls], BF16)
                    cnt_t = jnp.broadcast_to(cnt[h][:, ls], tile).astype(BF16)
                    e1_t = jnp.broadcast_to(e1[h][:, ls], tile).astype(BF16)
                    w = w + jnp.where(rb < cnt_t, e2, 0.0) * e1_t
                w_ref[slot, il * half:(il + 1) * half, ls] = pltpu.bitcast(w, jnp.uint32)

    @pl.when(e == 0)
    def _():
        gate_weights(0, 0)

    gate_weights(jnp.minimum(e + 1, pl.num_programs(1) - 1), (e + 1) % 2)
    act = _gelu(_dot_nt(u_ref[...], hn_ref[...]))
    g_ref[...] = (pltpu.bitcast(w_ref[e % 2], BF16).astype(F32) * act).astype(BF16)
    acc_ref[...] += jnp.dot(vt_ref[...], g_ref[...], preferred_element_type=F32)

    @pl.when(e == pl.num_programs(1) - 1)
    def _():
        o_ref[...] = x_ref[...] + acc_ref[...].T


def _peer_experts(x, hn, gates, u, vt, tm=512, te=1024):
    T, D = x.shape
    NE = u.shape[0]
    gates = [g.reshape(-1, T) for g in gates]
    once = pl.Buffered(1)
    gate_spec = pl.BlockSpec((PEER_HEADS * PEER_N_KEYS, tm), lambda i, e: (0, i), pipeline_mode=once)
    packed_spec = pl.BlockSpec((PEER_HEADS * PEER_N_KEYS // 2, tm), lambda i, e: (0, i), pipeline_mode=once)
    return pl.pallas_call(
        functools.partial(_peer_expert_kernel, rows_per_step=te // PEER_N_KEYS),
        grid=(T // tm, NE // te),
        in_specs=[
            pl.BlockSpec((tm, D), lambda i, e: (i, 0), pipeline_mode=once),
            pl.BlockSpec((te, D), lambda i, e: (e, 0)),
            pl.BlockSpec((D, te), lambda i, e: (0, e)),
            gate_spec, gate_spec, packed_spec, packed_spec,
            pl.BlockSpec((tm, D), lambda i, e: (i, 0), pipeline_mode=once),
        ],
        out_specs=pl.BlockSpec((tm, D), lambda i, e: (i, 0)),
        out_shape=jax.ShapeDtypeStruct((T, D), F32),
        scratch_shapes=[pltpu.VMEM((D, tm), F32), pltpu.VMEM((te, tm), BF16),
                        pltpu.VMEM((2, te // 2, tm), jnp.uint32)],
        compiler_params=_cparams(("parallel", "arbitrary")),
        name="peer_experts",
    )(hn, u, vt, *gates, x)


def _peer(x, norm_gain, w_query, sub_keys, u, v):
    hn, cnt, e1, rb, e2 = _peer_route(x, norm_gain, w_query.astype(BF16), sub_keys.astype(BF16))
    return _peer_experts(x, hn, (cnt, e1, rb, e2), u.astype(BF16), v.astype(BF16).T)


def _even_layer(x, groups, norm, w_in, na_q_gain, na_k_gain, na_rpb, lb, hg_out_gain, w_out):
    p = _matmul(x, w_in.astype(BF16), gain=norm, out_dtype=F32)
    mix_cols = NA_HEADS * NA_HEAD_DIM + HG_HEADS * HG_DIM
    mix = _neighbourhood_attention(p, groups, na_rpb, na_q_gain, na_k_gain, mix_cols)
    o_fw = _hgrn2(p, groups, lb[0], reverse=False, f_col=4 * NA_HEADS)
    mix = _hgrn2(p, groups, lb[1], reverse=True, f_col=5 * NA_HEADS, o_fw=o_fw, out_gain=hg_out_gain, mix=mix)
    return _matmul(mix, w_out.astype(BF16), residual=x, out_dtype=F32)


def _odd_layer(x, groups, norm, w_in, q_norm, kv_norm, w_uq, w_ukv, q_gain, k_gain, w_out):
    D = x.shape[1]
    pad_q = (-MLA_Q_RANK) % MLA_KV_RANK
    w_cq, w_ckv, w_kr = jnp.split(w_in, [MLA_Q_RANK, MLA_Q_RANK + MLA_KV_RANK], axis=1)
    w_in_p = jnp.concatenate(
        [w_cq, jnp.zeros((D, pad_q), w_in.dtype), w_ckv, w_kr, jnp.zeros((D, LANES - MLA_ROPE), w_in.dtype)],
        axis=1).astype(BF16)
    cc = _matmul(x, w_in_p, gain=norm, out_dtype=F32, tn=w_in_p.shape[1])
    w_uq_p = jnp.pad(w_uq.reshape(MLA_Q_RANK, MLA_HEADS, MLA_QK),
                     ((0, 0), (0, 0), (0, MLA_HEAD_PAD - MLA_QK))).reshape(MLA_Q_RANK, -1).astype(BF16)
    q = _matmul(cc, w_uq_p, gain=q_norm, out_dtype=BF16, x_col_block=0)
    kv = _matmul(cc, w_ukv.astype(BF16), gain=kv_norm, out_dtype=BF16,
                 x_col_block=(MLA_Q_RANK + pad_q) // MLA_KV_RANK)
    kr_block = (MLA_Q_RANK + pad_q + MLA_KV_RANK) // LANES
    qf, kf = _mla_prep(q, kv, cc, kr_block, groups, q_gain, k_gain)
    o = _mla_attention(qf, kf, kv, groups)
    return _matmul(o, w_out.astype(BF16), residual=x, out_dtype=F32)


def _forward(xs, ev_norm, ev_w_in, ev_na_q_gain, ev_na_k_gain, ev_na_rpb, ev_hg_lb_logits, ev_hg_out_gain,
             ev_w_out, od_norm, od_w_in, od_q_norm, od_kv_norm, od_w_uq, od_w_ukv, od_q_gain, od_k_gain,
             od_w_out, ff_norm, peer_w_query, peer_sub_keys, peer_u, peer_v):
    groups = [(int(x.shape[0]), int(x.shape[1])) for x in xs]
    D = xs[0].shape[-1]
    depth = ff_norm.shape[0]
    lb_all = jnp.cumsum(jax.nn.softmax(ev_hg_lb_logits.astype(F32), axis=1), axis=1)
    x = jnp.concatenate([t.reshape(-1, D) for t in xs], axis=0)
    for layer in range(depth):
        j = layer // 2
        if layer % 2 == 0:
            x = _even_layer(x, groups, ev_norm[j], ev_w_in[j], ev_na_q_gain[j], ev_na_k_gain[j], ev_na_rpb[j],
                            lb_all[:, j], ev_hg_out_gain[j], ev_w_out[j])
        else:
            x = _odd_layer(x, groups, od_norm[j], od_w_in[j], od_q_norm[j], od_kv_norm[j], od_w_uq[j],
                           od_w_ukv[j], od_q_gain[j], od_k_gain[j], od_w_out[j])
        x = _peer(x, ff_norm[layer], peer_w_query[layer], peer_sub_keys[layer], peer_u[layer], peer_v[layer])
    outs = []
    tok0 = 0
    for (B, S) in groups:
        outs.append(x[tok0:tok0 + B * S].reshape(B, S, D))
        tok0 += B * S
    return tuple(outs)


def kernel(x_prompt, x_sample, ev_norm, ev_w_in, ev_na_q_gain, ev_na_k_gain, ev_na_rpb, ev_hg_lb_logits, ev_hg_out_gain, ev_w_out, od_norm, od_w_in, od_q_norm, od_kv_norm, od_w_uq, od_w_ukv, od_q_gain, od_k_gain, od_w_out, ff_norm, peer_w_query, peer_sub_keys, peer_u, peer_v):
    return _forward((x_prompt, x_sample), ev_norm, ev_w_in, ev_na_q_gain, ev_na_k_gain, ev_na_rpb, ev_hg_lb_logits,
                    ev_hg_out_gain, ev_w_out, od_norm, od_w_in, od_q_norm, od_kv_norm, od_w_uq, od_w_ukv, od_q_gain,
                    od_k_gain, od_w_out, ff_norm, peer_w_query, peer_sub_keys, peer_u, peer_v)
```

```python
import functools

import numpy as np
import jax
import jax.numpy as jnp
from jax import lax
from jax.experimental import pallas as pl
from jax.experimental.pallas import tpu as pltpu

F32 = jnp.float32
BF16 = jnp.bfloat16

EPS = 1e-6
LANES = 128
VMEM_LIMIT_BYTES = 56 * 1024 * 1024
NEG_BIG = -1e30

GRID_W = 64
NA_HEADS = 8
NA_HEAD_DIM = 128
NA_WIN_ROWS = 8
NA_WIN_COLS = 16
NA_ROW_BLOCK = 8
NA_KEY_ROWS = 16
HG_HEADS = 8
HG_DIM = 128
HG_CHUNK = 32
MLA_HEADS = 16
MLA_Q_RANK = 768
MLA_KV_RANK = 512
MLA_NOPE = 128
MLA_ROPE = 64
MLA_QK = MLA_NOPE + MLA_ROPE
MLA_HEAD_PAD = 256
ROPE_BASE = 10000.0
PEER_HEADS = 8
PEER_N_KEYS = 128
PEER_TOPK = 16
PEER_HALF = 128


def _cparams(sem):
    return pltpu.CompilerParams(dimension_semantics=sem, vmem_limit_bytes=VMEM_LIMIT_BYTES)


def _rms(x, gain):
    return x * lax.rsqrt(jnp.mean(x * x, axis=-1, keepdims=True) + EPS) * gain


def _dot_nt(a, b):
    return lax.dot_general(a, b, (((1,), (1,)), ((), ())), preferred_element_type=F32)


def _dot_tn(a, b):
    return lax.dot_general(a, b, (((0,), (0,)), ((), ())), preferred_element_type=F32)


def _mm_kernel(*refs, has_norm, has_res):
    it = iter(refs)
    x_ref = next(it)
    g_ref = next(it) if has_norm else None
    w_ref = next(it)
    r_ref = next(it) if has_res else None
    o_ref = next(it)
    if has_norm:
        hn_ref = next(it)

        @pl.when(pl.program_id(1) == 0)
        def _():
            hn_ref[...] = _rms(x_ref[...].astype(F32), g_ref[...]).astype(BF16)

        a = hn_ref[...]
    else:
        a = x_ref[...].astype(BF16)
    acc = jnp.dot(a, w_ref[...], preferred_element_type=F32)
    if has_res:
        acc = acc + r_ref[...]
    o_ref[...] = acc.astype(o_ref.dtype)


def _matmul(x, w, *, gain=None, residual=None, out_dtype=F32, x_col_block=0, tm=512, tn=1024):
    M = x.shape[0]
    K, N = w.shape
    tn = min(tn, N)
    assert M % tm == 0 and N % tn == 0 and (x_col_block + 1) * K <= x.shape[1]
    has_norm, has_res = gain is not None, residual is not None
    in_specs = [pl.BlockSpec((tm, K), lambda i, j: (i, x_col_block))]
    args = [x]
    if has_norm:
        in_specs.append(pl.BlockSpec((1, K), lambda i, j: (0, 0)))
        args.append(gain.reshape(1, K).astype(F32))
    in_specs.append(pl.BlockSpec((K, tn), lambda i, j: (0, j)))
    args.append(w)
    if has_res:
        in_specs.append(pl.BlockSpec((tm, tn), lambda i, j: (i, j)))
        args.append(residual)
    return pl.pallas_call(
        functools.partial(_mm_kernel, has_norm=has_norm, has_res=has_res),
        grid=(M // tm, N // tn),
        in_specs=in_specs,
        out_specs=pl.BlockSpec((tm, tn), lambda i, j: (i, j)),
        out_shape=jax.ShapeDtypeStruct((M, N), out_dtype),
        scratch_shapes=[pltpu.VMEM((tm, K), BF16)] if has_norm else [],
        compiler_params=_cparams(("parallel", "arbitrary")),
        name="mm",
    )(*args)


NA_Q = NA_ROW_BLOCK * GRID_W
NA_KB = 256
NA_NKB = NA_KEY_ROWS * GRID_W // NA_KB


def _na_bias_tables(rpb):
    i = np.arange(NA_ROW_BLOCK)[:, None, None, None]
    c = np.arange(GRID_W)[None, :, None, None]
    j = np.arange(NA_KEY_ROWS)[None, None, :, None]
    kc = np.arange(GRID_W)[None, None, None, :]
    half = NA_WIN_ROWS // 2
    qc0 = np.clip(c - NA_WIN_COLS // 2, 0, GRID_W - NA_WIN_COLS)
    col_ok = (kc >= qc0) & (kc < qc0 + NA_WIN_COLS)
    dcol = np.clip(kc - c + NA_WIN_COLS - 1, 0, 2 * NA_WIN_COLS - 2)
    col_hot = (dcol[0, :, 0, :, None] == np.arange(2 * NA_WIN_COLS - 1)).astype(np.float32)
    by_col = jnp.einsum('hab,ckb->hack', rpb.astype(F32), col_hot, precision=lax.Precision.HIGHEST)
    tabs = []
    for var in range(3):
        if var == 0:
            r0 = np.maximum(i - half, 0)
            dr = j - i
        elif var == 1:
            r0 = i
            dr = j - half - i
        else:
            r0 = np.minimum(i + half, NA_KEY_ROWS - NA_WIN_ROWS)
            dr = j - (NA_KEY_ROWS - NA_ROW_BLOCK) - i
        ok = (j >= r0) & (j < r0 + NA_WIN_ROWS) & col_ok
        dri = np.clip(dr + NA_WIN_ROWS - 1, 0, 2 * NA_WIN_ROWS - 2)[:, 0, :, 0]
        row_hot = (dri[:, :, None] == np.arange(2 * NA_WIN_ROWS - 1)).astype(np.float32)
        vals = jnp.einsum('ija,hack->hicjk', row_hot, by_col, precision=lax.Precision.HIGHEST)
        tabs.append(jnp.where(jnp.asarray(ok)[None], vals, NEG_BIG).reshape(
            NA_HEADS, NA_Q, NA_KEY_ROWS * GRID_W))
    return jnp.stack(tabs)


def _na_steps(groups):
    steps = []
    tok0 = 0
    for (B, S) in groups:
        rows = S // GRID_W
        nrb = rows // NA_ROW_BLOCK
        assert nrb >= 3 and S % NA_Q == 0
        for b in range(B):
            for rb in range(nrb):
                var = 0 if rb == 0 else (2 if rb == nrb - 1 else 1)
                kr0 = min(max(NA_ROW_BLOCK * rb - NA_WIN_ROWS // 2, 0), rows - NA_KEY_ROWS)
                steps.append((var, (tok0 + b * S) // NA_Q + rb, (tok0 + b * S + kr0 * GRID_W) // NA_KB))
        tok0 += B * S
    steps.sort(key=lambda t: t[0])
    arr = np.asarray(steps, np.int32)
    return arr[:, 0], arr[:, 1], arr[:, 2]


def _na_kernel(var_ref, qb_ref, kb_ref, q_ref, *rest):
    k_refs = rest[:NA_NKB]
    v_refs = rest[NA_NKB:2 * NA_NKB]
    bias_ref, qg_ref, kg_ref, o_ref = rest[2 * NA_NKB:]
    scale = NA_HEAD_DIM ** -0.5
    q = _rms(q_ref[...].astype(F32), qg_ref[...]).astype(BF16)
    s = []
    for m in range(NA_NKB):
        k = _rms(k_refs[m][...].astype(F32), kg_ref[...]).astype(BF16)
        s.append(_dot_nt(q, k) * scale + bias_ref[0, 0, :, m * NA_KB:(m + 1) * NA_KB])
    mx = s[0].max(axis=-1, keepdims=True)
    for m in range(1, NA_NKB):
        mx = jnp.maximum(mx, s[m].max(axis=-1, keepdims=True))
    l = jnp.zeros_like(mx)
    acc = jnp.zeros((NA_Q, NA_HEAD_DIM), F32)
    for m in range(NA_NKB):
        p = jnp.exp(s[m] - mx)
        l = l + p.sum(axis=-1, keepdims=True)
        acc = acc + jnp.dot(p.astype(BF16), v_refs[m][...].astype(BF16), preferred_element_type=F32)
    o_ref[...] = (acc / l).astype(o_ref.dtype)


def _neighbourhood_attention(p, groups, rpb, q_gain, k_gain, out_cols):
    T = p.shape[0]
    var, qblk, kblk = _na_steps(groups)
    tabs = _na_bias_tables(rpb)
    H = NA_HEADS

    def qmap(h, n, var, qb, kb):
        return (qb[n], h)

    def kmap(m, off):
        return lambda h, n, var, qb, kb: (kb[n] + m, off + h)

    in_specs = [pl.BlockSpec((NA_Q, NA_HEAD_DIM), qmap)]
    in_specs += [pl.BlockSpec((NA_KB, NA_HEAD_DIM), kmap(m, H)) for m in range(NA_NKB)]
    in_specs += [pl.BlockSpec((NA_KB, NA_HEAD_DIM), kmap(m, 2 * H)) for m in range(NA_NKB)]
    in_specs += [
        pl.BlockSpec((1, 1, NA_Q, NA_KEY_ROWS * GRID_W), lambda h, n, var, qb, kb: (var[n], h, 0, 0)),
        pl.BlockSpec((1, NA_HEAD_DIM), lambda h, n, var, qb, kb: (0, 0)),
        pl.BlockSpec((1, NA_HEAD_DIM), lambda h, n, var, qb, kb: (0, 0)),
    ]
    return pl.pallas_call(
        _na_kernel,
        grid_spec=pltpu.PrefetchScalarGridSpec(
            num_scalar_prefetch=3,
            grid=(H, len(var)),
            in_specs=in_specs,
            out_specs=pl.BlockSpec((NA_Q, NA_HEAD_DIM), qmap),
        ),
        out_shape=jax.ShapeDtypeStruct((T, out_cols), BF16),
        compiler_params=_cparams(("parallel", "arbitrary")),
        name="natten",
    )(jnp.asarray(var), jnp.asarray(qblk), jnp.asarray(kblk), p, *([p] * (2 * NA_NKB)), tabs,
      q_gain.reshape(1, -1).astype(F32), k_gain.reshape(1, -1).astype(F32))


HG_BLOCK = 512
HG_NCHUNK = HG_BLOCK // HG_CHUNK
HG_SUB = 8
assert HG_CHUNK == 4 * HG_SUB


def _hg_steps(groups, reverse):
    blk, first = [], []
    tok0 = 0
    for (B, S) in groups:
        nb = S // HG_BLOCK
        assert S % HG_BLOCK == 0
        for b in range(B):
            order = range(nb - 1, -1, -1) if reverse else range(nb)
            for n, sb in enumerate(order):
                blk.append((tok0 + b * S) // HG_BLOCK + sb)
                first.append(1 if n == 0 else 0)
        tok0 += B * S
    return np.asarray(blk, np.int32), np.asarray(first, np.int32)


def _hg_matrices(reverse):
    C, S = HG_CHUNK, HG_SUB
    t = np.arange(C)[:, None]
    u = np.arange(C)[None, :]
    upto = (u >= t) if reverse else (u <= t)
    after = (u < t) if reverse else (u > t)
    same = lambda m: (u // m) == (t // m)
    sums = np.concatenate([upto, upto & same(S), upto & same(2 * S), after & same(S), after & same(2 * S)], axis=0)
    query_side = 0 if reverse else 1
    masks = []
    for m in (S, 2 * S):
        bt, bs = t // m, u // m
        masks.append((bt % 2 == query_side) & (bs == (bt + 1 if reverse else bt - 1)))
    return sums.astype(np.float32), np.stack(masks).astype(np.float32)


def _hg_kernel(blk_ref, first_ref, q_ref, f_ref, i_ref, lb_ref, sums_ref, mask_ref, *rest, reverse, finish):
    if finish:
        ofw_ref, g_ref, gain_ref, o_ref, st_ref = rest
    else:
        o_ref, st_ref = rest
    h = pl.program_id(0)
    n = pl.program_id(1)
    C = HG_CHUNK

    @pl.when(first_ref[n] == 1)
    def _():
        st_ref[...] = jnp.zeros_like(st_ref)

    lb = lb_ref[pl.ds(h, 1), :]
    sums = sums_ref[...]
    mask0, mask1 = mask_ref[0], mask_ref[1]
    S = HG_SUB
    NC = HG_NCHUNK
    TB = HG_BLOCK
    blk3 = (TB // S, S, HG_DIM)
    row8 = lax.broadcasted_iota(jnp.int32, blk3, 1)
    ones = jnp.ones((HG_DIM, HG_DIM), BF16)
    rows = lambda x, c: x[c * C:(c + 1) * C]

    f = lb + (1.0 - lb) * jax.nn.sigmoid(f_ref[...].astype(F32))
    logf = jnp.log(f)
    kk = 1.0 - f
    q = q_ref[...].astype(F32)
    q = q * jax.nn.sigmoid(q)
    v = i_ref[...].astype(F32)
    logf_hi = logf.astype(BF16)
    logf_lo = (logf - logf_hi.astype(F32)).astype(BF16)
    e_cat = (jnp.dot(sums, jnp.concatenate([rows(logf_hi, c) for c in range(NC)], axis=1), preferred_element_type=F32)
             + jnp.dot(sums, jnp.concatenate([rows(logf_lo, c) for c in range(NC)], axis=1),
                       preferred_element_type=F32))
    part = lambda r: jnp.concatenate(
        [e_cat[r * C:(r + 1) * C, c * HG_DIM:(c + 1) * HG_DIM] for c in range(NC)], axis=0)
    b, bloc, eq1, ek0, ek1 = [part(r) for r in range(5)]
    b_end = jnp.concatenate(
        [jnp.broadcast_to(rows(b, c)[0:1] if reverse else rows(b, c)[C - 1:C], (C, HG_DIM)) for c in range(NC)], axis=0)
    qb = (q * jnp.exp(b)).astype(BF16)
    q0, k0 = (q * jnp.exp(bloc)).astype(BF16), (kk * jnp.exp(ek0)).astype(BF16)
    q1, k1 = (q * jnp.exp(eq1)).astype(BF16), (kk * jnp.exp(ek1)).astype(BF16)
    ks = (kk * jnp.exp(b_end - b)).astype(BF16)
    vb = v.astype(BF16)
    o_off, upd = [], []
    for c in range(NC):
        a_off = (_dot_nt(rows(q0, c), rows(k0, c)) * mask0 + _dot_nt(rows(q1, c), rows(k1, c)) * mask1)
        o_off.append(jnp.dot(a_off.astype(BF16), rows(vb, c), preferred_element_type=F32))
        upd.append(_dot_tn(rows(vb, c), rows(ks, c)))
    q3, k3, v3, b3 = q.reshape(blk3), kk.reshape(blk3), v.reshape(blk3), bloc.reshape(blk3)
    prods = []
    for ds in range(S):
        keep = (row8 <= ds) if reverse else (row8 >= ds)
        e = jnp.exp(jnp.where(keep, b3 - b3[:, ds:ds + 1, :], -jnp.inf))
        prods.append((q3 * e * k3[:, ds:ds + 1, :]).reshape(TB, HG_DIM).astype(BF16))
    a_bc = jnp.dot(jnp.concatenate(prods, axis=0), ones, preferred_element_type=F32)
    o_diag = jnp.zeros(blk3, F32)
    for ds in range(S):
        o_diag = o_diag + a_bc[ds * TB:(ds + 1) * TB, :].reshape(blk3) * v3[:, ds:ds + 1, :]
    st = st_ref[...]
    o_inter = [None] * NC
    for cc in range(NC):
        c = NC - 1 - cc if reverse else cc
        o_inter[c] = _dot_nt(rows(qb, c), st.astype(BF16))
        st = st * jnp.exp(rows(b_end, c)[0:1]) + upd[c]
    st_ref[...] = st
    o_blk = jnp.concatenate([o_inter[c] + o_off[c] for c in range(NC)], axis=0) + o_diag.reshape(TB, HG_DIM)

    if finish:
        o = o_blk + ofw_ref[...]
        g = g_ref[...].astype(F32)
        o_ref[...] = (_rms(o, gain_ref[...]) * (g * jax.nn.sigmoid(g))).astype(o_ref.dtype)
    else:
        o_ref[...] = o_blk


def _hgrn2(p, groups, lb, *, reverse, f_col, o_fw=None, out_gain=None, mix=None):
    T = p.shape[0]
    blk, first = _hg_steps(groups, reverse)
    H = HG_HEADS
    finish = o_fw is not None
    C = HG_CHUNK
    sums, masks = _hg_matrices(reverse)

    def col(off):
        return lambda h, n, blk, first: (blk[n], off + h)

    const = lambda h, n, blk, first: (0, 0)
    in_specs = [
        pl.BlockSpec((HG_BLOCK, HG_DIM), col(3 * NA_HEADS)),
        pl.BlockSpec((HG_BLOCK, HG_DIM), col(f_col)),
        pl.BlockSpec((HG_BLOCK, HG_DIM), col(6 * NA_HEADS)),
        pl.BlockSpec((H, HG_DIM), const),
        pl.BlockSpec(sums.shape, const),
        pl.BlockSpec(masks.shape, lambda h, n, blk, first: (0, 0, 0)),
    ]
    args = [p, p, p, lb.reshape(H, HG_DIM).astype(F32), jnp.asarray(sums, BF16), jnp.asarray(masks)]
    aliases = {}
    if finish:
        in_specs += [
            pl.BlockSpec((HG_BLOCK, HG_DIM), col(0)),
            pl.BlockSpec((HG_BLOCK, HG_DIM), col(7 * NA_HEADS)),
            pl.BlockSpec((1, HG_DIM), const),
            pl.BlockSpec(memory_space=pl.ANY),
        ]
        args += [o_fw, p, out_gain.reshape(1, HG_DIM).astype(F32), mix]
        aliases = {2 + len(args) - 1: 0}
        out_spec = pl.BlockSpec((HG_BLOCK, HG_DIM), col(NA_HEADS))
        out_shape = jax.ShapeDtypeStruct(mix.shape, mix.dtype)
    else:
        out_spec = pl.BlockSpec((HG_BLOCK, HG_DIM), col(0))
        out_shape = jax.ShapeDtypeStruct((T, H * HG_DIM), F32)

    def body(blk_ref, first_ref, *refs):
        if finish:
            refs = refs[:9] + refs[10:]
        _hg_kernel(blk_ref, first_ref, *refs, reverse=reverse, finish=finish)

    return pl.pallas_call(
        body,
        grid_spec=pltpu.PrefetchScalarGridSpec(
            num_scalar_prefetch=2,
            grid=(H, len(blk)),
            in_specs=in_specs,
            out_specs=out_spec,
            scratch_shapes=[
                pltpu.VMEM((HG_DIM, HG_DIM), F32),
            ],
        ),
        out_shape=out_shape,
        input_output_aliases=aliases,
        compiler_params=_cparams(("parallel", "arbitrary")),
        name="hgrn2_bw" if reverse else "hgrn2_fw",
    )(jnp.asarray(blk), jnp.asarray(first), *args)


def _rope_tables(groups):
    half = MLA_ROPE // 2
    smax = max(S for _, S in groups)
    pos = jnp.arange(smax, dtype=F32)
    inv = 1.0 / (ROPE_BASE ** (jnp.arange(0, MLA_ROPE, 2, dtype=F32) / MLA_ROPE))
    ang = pos[:, None] * inv[None, :]
    cos, sin = jnp.cos(ang), jnp.sin(ang)
    z = jnp.zeros((smax, LANES - 2 * half), F32)
    zh = jnp.zeros((smax, half), F32)
    c = jnp.concatenate([cos, cos, z], axis=1)
    s_from_x2 = jnp.concatenate([-sin, zh, z], axis=1)
    s_from_x1 = jnp.concatenate([zh, sin, z], axis=1)
    return c, s_from_x2, s_from_x1


def _rope_block(t, c, s2, s1):
    half = MLA_ROPE // 2
    return t * c + pltpu.roll(t, LANES - half, 1) * s2 + pltpu.roll(t, half, 1) * s1


def _qprep_kernel(q_ref, g_ref, c_ref, s2_ref, s1_ref, o_ref, *, scale):
    c, s2, s1 = c_ref[...], s2_ref[...], s1_ref[...]
    for h in range(MLA_HEADS):
        a = q_ref[:, h * MLA_HEAD_PAD:h * MLA_HEAD_PAD + LANES].astype(F32)
        r = q_ref[:, h * MLA_HEAD_PAD + LANES:(h + 1) * MLA_HEAD_PAD].astype(F32)
        ms = (jnp.sum(a * a, axis=-1, keepdims=True) + jnp.sum(r * r, axis=-1, keepdims=True)) / MLA_QK
        inv = lax.rsqrt(ms + EPS)
        o_ref[:, h * MLA_HEAD_PAD:h * MLA_HEAD_PAD + LANES] = (a * inv * g_ref[:, :LANES] * scale).astype(o_ref.dtype)
        rn = r * inv * g_ref[:, LANES:]
        o_ref[:, h * MLA_HEAD_PAD + LANES:(h + 1) * MLA_HEAD_PAD] = (_rope_block(rn, c, s2, s1) * scale).astype(o_ref.dtype)


def _kprep_kernel(kv_ref, kr_ref, g_ref, c_ref, s2_ref, s1_ref, o_ref):
    c, s2, s1 = c_ref[...], s2_ref[...], s1_ref[...]
    r = kr_ref[...].astype(F32)
    rs = jnp.sum(r * r, axis=-1, keepdims=True)
    for h in range(MLA_HEADS):
        a = kv_ref[:, h * MLA_HEAD_PAD:h * MLA_HEAD_PAD + LANES].astype(F32)
        inv = lax.rsqrt((jnp.sum(a * a, axis=-1, keepdims=True) + rs) / MLA_QK + EPS)
        o_ref[:, h * MLA_HEAD_PAD:h * MLA_HEAD_PAD + LANES] = (a * inv * g_ref[:, :LANES]).astype(o_ref.dtype)
        rn = r * inv * g_ref[:, LANES:]
        o_ref[:, h * MLA_HEAD_PAD + LANES:(h + 1) * MLA_HEAD_PAD] = _rope_block(rn, c, s2, s1).astype(o_ref.dtype)


def _pos_block_map(groups, tm):
    bounds = []
    tok0 = 0
    for (B, S) in groups:
        assert S % tm == 0
        bounds.append((tok0 // tm, S // tm))
        tok0 += B * S

    def imap(i):
        out = (i - bounds[0][0]) % bounds[0][1]
        for (b0, per) in bounds[1:]:
            out = jnp.where(i >= b0, (i - b0) % per, out)
        return (out, 0)

    return imap


def _pad_gain(g):
    return jnp.concatenate([g.astype(F32), jnp.zeros((MLA_HEAD_PAD - MLA_QK,), F32)]).reshape(1, MLA_HEAD_PAD)


def _mla_prep(q, kv, cc, kr_block, groups, q_gain, k_gain, tm=256):
    T = q.shape[0]
    W = MLA_HEADS * MLA_HEAD_PAD
    tabs = _rope_tables(groups)
    pmap = _pos_block_map(groups, tm)
    tab_specs = [pl.BlockSpec((tm, LANES), pmap)] * 3
    row = pl.BlockSpec((tm, W), lambda i: (i, 0))
    gspec = pl.BlockSpec((1, MLA_HEAD_PAD), lambda i: (0, 0))
    qf = pl.pallas_call(
        functools.partial(_qprep_kernel, scale=MLA_QK ** -0.5 * LOG2_E),
        grid=(T // tm,),
        in_specs=[row, gspec] + tab_specs,
        out_specs=row,
        out_shape=jax.ShapeDtypeStruct((T, W), BF16),
        compiler_params=_cparams(("parallel",)),
        name="mla_qprep",
    )(q, _pad_gain(q_gain), *tabs)
    kf = pl.pallas_call(
        _kprep_kernel,
        grid=(T // tm,),
        in_specs=[row, pl.BlockSpec((tm, LANES), lambda i: (i, kr_block)), gspec] + tab_specs,
        out_specs=row,
        out_shape=jax.ShapeDtypeStruct((T, W), BF16),
        compiler_params=_cparams(("parallel",)),
        name="mla_kprep",
    )(kv, cc, _pad_gain(k_gain), *tabs)
    return qf, kf


FLASH_SUB = 512
LOG2_E = 1.4426950408889634


def _attn_kernel(q_ref, k_ref, vt_ref, *rest):
    o_ref = rest[-1]
    q = q_ref[...]
    nsub = k_ref.shape[0] // FLASH_SUB
    s = [_dot_nt(k_ref[c * FLASH_SUB:(c + 1) * FLASH_SUB, :], q) for c in range(nsub)]
    m = s[0].max(axis=0, keepdims=True)
    for c in range(1, nsub):
        m = jnp.maximum(m, s[c].max(axis=0, keepdims=True))
    l = jnp.zeros_like(m)
    acc = jnp.zeros((vt_ref.shape[0], q.shape[0]), F32)
    for c in range(nsub):
        p = jnp.exp2(s[c] - m)
        l = l + p.sum(axis=0, keepdims=True)
        acc = acc + jnp.dot(vt_ref[:, c * FLASH_SUB:(c + 1) * FLASH_SUB], p.astype(BF16), preferred_element_type=F32)
    o_ref[...] = (acc / l).T.astype(o_ref.dtype)


MLA_MAX_SEQ = 16384


def _mla_attention(qf, kf, vt, groups, tq=512):
    T = qf.shape[0]
    H = MLA_HEADS
    out = None
    tok0 = 0
    for (B, S) in groups:
        assert S % tq == 0 and S % FLASH_SUB == 0 and tok0 % S == 0 and S <= MLA_MAX_SEQ
        nq = S // tq
        q0, s0 = tok0 // tq, tok0 // S
        once = pl.Buffered(1)
        in_specs = [
            pl.BlockSpec((tq, MLA_HEAD_PAD), lambda b, h, qi, q0=q0, nq=nq: (q0 + b * nq + qi, h)),
            pl.BlockSpec((S, MLA_HEAD_PAD), lambda b, h, qi, s0=s0: (s0 + b, h), pipeline_mode=once),
            pl.BlockSpec((LANES, S), lambda b, h, qi, s0=s0: (h, s0 + b), pipeline_mode=once),
        ]
        args = [qf, kf, vt]
        aliases = {}
        if out is not None:
            in_specs.append(pl.BlockSpec(memory_space=pl.ANY))
            args.append(out)
            aliases = {3: 0}
        out = pl.pallas_call(
            _attn_kernel,
            grid=(B, H, nq),
            in_specs=in_specs,
            out_specs=pl.BlockSpec((tq, LANES), lambda b, h, qi, q0=q0, nq=nq: (q0 + b * nq + qi, h)),
            out_shape=jax.ShapeDtypeStruct((T, H * LANES), BF16),
            input_output_aliases=aliases,
            compiler_params=_cparams(("parallel", "parallel", "parallel")),
            name="mla_attn",
        )(*args)
        tok0 += B * S
    return out


def _peer_candidates():
    a, b = [], []
    a += [0] * 16; b += list(range(16))
    for aa in (1, 2, 3):
        a += [aa] * 8; b += list(range(8))
    for bb in (0, 1):
        a += [4, 5, 6, 7]; b += [bb] * 4
    a += list(range(8, 16)); b += [0] * 8
    for bb in (2, 3):
        a += [4, 5, 6, 7]; b += [bb] * 4
    return np.asarray(a), np.asarray(b)


_CAND_A, _CAND_B = _peer_candidates()
N_CAND = len(_CAND_A)


def _extract16(val, order, exact):
    step = jnp.full(val.shape, float(PEER_TOPK), F32)
    tops = []
    for r in range(PEER_TOPK):
        m = val.max(axis=0, keepdims=True)
        if exact:
            first = jnp.where(val == m, order, 1e9).min(axis=0, keepdims=True)
            sel = order == first
        else:
            sel = val == m
        step = jnp.where(sel, float(r), step)
        val = jnp.where(sel, -jnp.inf, val)
        tops.append(m)
    return jnp.concatenate(tops, axis=0), step


def _route_head(s1, s2, pos, amat, exact):
    tm = s1.shape[1]
    idx = lax.broadcasted_iota(jnp.int32, s1.shape, 0).astype(F32)
    t1, r1 = _extract16(s1, idx, exact)
    t2, r2 = _extract16(s2, idx, exact)
    mid1 = jnp.concatenate([t1[4:8], t1[4:8]], axis=0)
    rep2 = lambda b: jnp.broadcast_to(t2[b:b + 1], (4, tm))
    cand = jnp.concatenate([
        t1[0:1] + t2, t1[1:2] + t2[0:8], t1[2:3] + t2[0:8], t1[3:4] + t2[0:8],
        mid1 + jnp.concatenate([rep2(0), rep2(1)], axis=0),
        t1[8:16] + t2[0:1],
        mid1 + jnp.concatenate([rep2(2), rep2(3)], axis=0)], axis=0)
    best = cand[0:1]
    _, cstep = _extract16(cand, pos, exact)
    chosen = (cstep < float(PEER_TOPK)).astype(F32)
    z = jnp.sum(chosen * jnp.exp(cand - best), axis=0, keepdims=True)
    per_a = jnp.dot(amat, chosen.astype(BF16), preferred_element_type=F32)
    cnt = jnp.zeros(s1.shape, F32)
    for a in range(PEER_TOPK):
        cnt = jnp.where(r1 == float(a), per_a[a:a + 1], cnt)
    e1n = jnp.exp(s1 - t1[0:1]) / z
    e2 = jnp.exp(s2 - t2[0:1])
    removed = lambda st: jnp.sum((st < float(PEER_TOPK)).astype(F32), axis=0, keepdims=True)
    ties = (jnp.abs(removed(r1) - PEER_TOPK) + jnp.abs(removed(r2) - PEER_TOPK)
            + jnp.abs(jnp.sum(chosen, axis=0, keepdims=True) - PEER_TOPK))
    return cnt, e1n, r2, e2, ties


def _peer_route_kernel(x_ref, g_ref, wq_ref, keys_ref, pos_ref, amat_ref,
                       hn_ref, cnt_ref, e1_ref, rb_ref, e2_ref, sc_ref):
    tm = x_ref.shape[0]
    hn = _rms(x_ref[...].astype(F32), g_ref[...]).astype(BF16)
    hn_ref[...] = hn
    qry = jnp.dot(hn, wq_ref[...], preferred_element_type=F32)
    for hp in range(2 * PEER_HEADS):
        sc_ref[hp] = _dot_nt(keys_ref[hp // 2, hp % 2], qry[:, hp * PEER_HALF:(hp + 1) * PEER_HALF].astype(BF16))

    def route(exact):
        ties = jnp.zeros((1, tm), F32)
        for h in range(PEER_HEADS):
            cnt, e1n, r2, e2, t = _route_head(sc_ref[2 * h], sc_ref[2 * h + 1], pos_ref[...], amat_ref[...], exact)
            cnt_ref[h] = cnt
            e1_ref[h] = e1n
            rb_ref[h] = pltpu.bitcast(r2.astype(BF16), jnp.uint32)
            e2_ref[h] = pltpu.bitcast(e2.astype(BF16), jnp.uint32)
            ties = ties + t
        return ties

    ties = route(exact=False)

    @pl.when(jnp.max(ties) > 0.0)
    def _():
        route(exact=True)


def _peer_route(x, norm_gain, w_query, sub_keys, tm=256):
    T, D = x.shape
    H = PEER_HEADS
    pos = np.broadcast_to((_CAND_A * PEER_TOPK + _CAND_B).astype(np.float32)[:, None], (N_CAND, tm))
    amat = (np.arange(PEER_TOPK)[:, None] == _CAND_A[None, :]).astype(np.float32)
    gate_shape = jax.ShapeDtypeStruct((H, PEER_N_KEYS, T), F32)
    packed_shape = jax.ShapeDtypeStruct((H, PEER_N_KEYS // 2, T), jnp.uint32)
    gate_spec = pl.BlockSpec((H, PEER_N_KEYS, tm), lambda i: (0, 0, i))
    packed_spec = pl.BlockSpec((H, PEER_N_KEYS // 2, tm), lambda i: (0, 0, i))
    return pl.pallas_call(
        _peer_route_kernel,
        grid=(T // tm,),
        in_specs=[
            pl.BlockSpec((tm, D), lambda i: (i, 0)),
            pl.BlockSpec((1, D), lambda i: (0, 0)),
            pl.BlockSpec(w_query.shape, lambda i: (0, 0)),
            pl.BlockSpec(sub_keys.shape, lambda i: (0, 0, 0, 0)),
            pl.BlockSpec((N_CAND, tm), lambda i: (0, 0)),
            pl.BlockSpec((PEER_TOPK, N_CAND), lambda i: (0, 0)),
        ],
        out_specs=[pl.BlockSpec((tm, D), lambda i: (i, 0)), gate_spec, gate_spec, packed_spec, packed_spec],
        out_shape=[jax.ShapeDtypeStruct((T, D), BF16), gate_shape, gate_shape, packed_shape, packed_shape],
        scratch_shapes=[pltpu.VMEM((2 * H, PEER_N_KEYS, tm), F32)],
        compiler_params=_cparams(("parallel",)),
        name="peer_route",
    )(x, norm_gain.reshape(1, D).astype(F32), w_query, sub_keys, jnp.asarray(pos), jnp.asarray(amat, BF16))


def _gelu(a):
    return 0.5 * a * (1.0 + lax.erf(a * (2.0 ** -0.5)))


def _peer_expert_kernel(hn_ref, u_ref, vt_ref, cnt_ref, e1_ref, rb_ref, e2_ref, x_ref, o_ref, acc_ref, g_ref, w_ref,
                        *, rows_per_step):
    e = pl.program_id(1)

    @pl.when(e == 0)
    def _():
        acc_ref[...] = jnp.zeros_like(acc_ref)

    def gate_weights(step, slot):
        tile = (PEER_N_KEYS, LANES)
        half = PEER_N_KEYS // 2
        for il in range(rows_per_step):
            i = step * rows_per_step + il
            cnt = [cnt_ref[pl.ds(h * PEER_N_KEYS + i, 1), :] for h in range(PEER_HEADS)]
            e1 = [e1_ref[pl.ds(h * PEER_N_KEYS + i, 1), :] for h in range(PEER_HEADS)]
            for st in range(w_ref.shape[2] // LANES):
                ls = slice(st * LANES, (st + 1) * LANES)
                w = jnp.zeros(tile, BF16)
                for h in range(PEER_HEADS):
                    hs = slice(h * half, (h + 1) * half)
                    rb = pltpu.bitcast(rb_ref[hs, ls], BF16)
                    e2 = pltpu.bitcast(e2_ref[hs, ls], BF16)
                    cnt_t = jnp.broadcast_to(cnt[h][:, ls], tile).astype(BF16)
                    e1_t = jnp.broadcast_to(e1[h][:, ls], tile).astype(BF16)
                    w = w + jnp.where(rb < cnt_t, e2, 0.0) * e1_t
                w_ref[slot, il * half:(il + 1) * half, ls] = pltpu.bitcast(w, jnp.uint32)

    @pl.when(e == 0)
    def _():
        gate_weights(0, 0)

    gate_weights(jnp.minimum(e + 1, pl.num_programs(1) - 1), (e + 1) % 2)
    act = _gelu(_dot_nt(u_ref[...], hn_ref[...]))
    g_ref[...] = (pltpu.bitcast(w_ref[e % 2], BF16).astype(F32) * act).astype(BF16)
    acc_ref[...] += jnp.dot(vt_ref[...], g_ref[...], preferred_element_type=F32)

    @pl.when(e == pl.num_programs(1) - 1)
    def _():
        o_ref[...] = x_ref[...] + acc_ref[...].T


def _peer_experts(x, hn, gates, u, vt, tm=512, te=1024):
    T, D = x.shape
    NE = u.shape[0]
    gates = [g.reshape(-1, T) for g in gates]
    once = pl.Buffered(1)
    gate_spec = pl.BlockSpec((PEER_HEADS * PEER_N_KEYS, tm), lambda i, e: (0, i), pipeline_mode=once)
    packed_spec = pl.BlockSpec((PEER_HEADS * PEER_N_KEYS // 2, tm), lambda i, e: (0, i), pipeline_mode=once)
    return pl.pallas_call(
        functools.partial(_peer_expert_kernel, rows_per_step=te // PEER_N_KEYS),
        grid=(T // tm, NE // te),
        in_specs=[
            pl.BlockSpec((tm, D), lambda i, e: (i, 0), pipeline_mode=once),
            pl.BlockSpec((te, D), lambda i, e: (e, 0)),
            pl.BlockSpec((D, te), lambda i, e: (0, e)),
            gate_spec, gate_spec, packed_spec, packed_spec,
            pl.BlockSpec((tm, D), lambda i, e: (i, 0), pipeline_mode=once),
        ],
        out_specs=pl.BlockSpec((tm, D), lambda i, e: (i, 0)),
        out_shape=jax.ShapeDtypeStruct((T, D), F32),
        scratch_shapes=[pltpu.VMEM((D, tm), F32), pltpu.VMEM((te, tm), BF16),
                        pltpu.VMEM((2, te // 2, tm), jnp.uint32)],
        compiler_params=_cparams(("parallel", "arbitrary")),
        name="peer_experts",
    )(hn, u, vt, *gates, x)


def _peer(x, norm_gain, w_query, sub_keys, u, v):
    hn, cnt, e1, rb, e2 = _peer_route(x, norm_gain, w_query.astype(BF16), sub_keys.astype(BF16))
    return _peer_experts(x, hn, (cnt, e1, rb, e2), u.astype(BF16), v.astype(BF16).T)


def _even_layer(x, groups, norm, w_in, na_q_gain, na_k_gain, na_rpb, lb, hg_out_gain, w_out):
    p = _matmul(x, w_in.astype(BF16), gain=norm, out_dtype=F32)
    mix_cols = NA_HEADS * NA_HEAD_DIM + HG_HEADS * HG_DIM
    mix = _neighbourhood_attention(p, groups, na_rpb, na_q_gain, na_k_gain, mix_cols)
    o_fw = _hgrn2(p, groups, lb[0], reverse=False, f_col=4 * NA_HEADS)
    mix = _hgrn2(p, groups, lb[1], reverse=True, f_col=5 * NA_HEADS, o_fw=o_fw, out_gain=hg_out_gain, mix=mix)
    return _matmul(mix, w_out.astype(BF16), residual=x, out_dtype=F32)


def _odd_layer(x, groups, norm, w_in, q_norm, kv_norm, w_uq, w_ukv, q_gain, k_gain, w_out):
    D = x.shape[1]
    pad_q = (-MLA_Q_RANK) % MLA_KV_RANK
    w_cq, w_ckv, w_kr = jnp.split(w_in, [MLA_Q_RANK, MLA_Q_RANK + MLA_KV_RANK], axis=1)
    w_in_p = jnp.concatenate(
        [w_cq, jnp.zeros((D, pad_q), w_in.dtype), w_ckv, w_kr, jnp.zeros((D, LANES - MLA_ROPE), w_in.dtype)],
        axis=1).astype(BF16)
    cc = _matmul(x, w_in_p, gain=norm, out_dtype=F32, tn=w_in_p.shape[1])
    w_uq_p = jnp.pad(w_uq.reshape(MLA_Q_RANK, MLA_HEADS, MLA_QK),
                     ((0, 0), (0, 0), (0, MLA_HEAD_PAD - MLA_QK))).reshape(MLA_Q_RANK, -1).astype(BF16)
    q = _matmul(cc, w_uq_p, gain=q_norm, out_dtype=BF16, x_col_block=0)
    kv = _matmul(cc, w_ukv.astype(BF16), gain=kv_norm, out_dtype=BF16,
                 x_col_block=(MLA_Q_RANK + pad_q) // MLA_KV_RANK)
    kr_block = (MLA_Q_RANK + pad_q + MLA_KV_RANK) // LANES
    qf, kf = _mla_prep(q, kv, cc, kr_block, groups, q_gain, k_gain)
    T = x.shape[0]
    vt = kv.reshape(T, MLA_HEADS, 2, LANES)[:, :, 1, :].transpose(1, 2, 0).reshape(MLA_HEADS * LANES, T)
    o = _mla_attention(qf, kf, vt, groups)
    return _matmul(o, w_out.astype(BF16), residual=x, out_dtype=F32)


def _forward(xs, ev_norm, ev_w_in, ev_na_q_gain, ev_na_k_gain, ev_na_rpb, ev_hg_lb_logits, ev_hg_out_gain,
             ev_w_out, od_norm, od_w_in, od_q_norm, od_kv_norm, od_w_uq, od_w_ukv, od_q_gain, od_k_gain,
             od_w_out, ff_norm, peer_w_query, peer_sub_keys, peer_u, peer_v):
    groups = [(int(x.shape[0]), int(x.shape[1])) for x in xs]
    D = xs[0].shape[-1]
    depth = ff_norm.shape[0]
    lb_all = jnp.cumsum(jax.nn.softmax(ev_hg_lb_logits.astype(F32), axis=1), axis=1)
    x = jnp.concatenate([t.reshape(-1, D) for t in xs], axis=0)
    for layer in range(depth):
        j = layer // 2
        if layer % 2 == 0:
            x = _even_layer(x, groups, ev_norm[j], ev_w_in[j], ev_na_q_gain[j], ev_na_k_gain[j], ev_na_rpb[j],
                            lb_all[:, j], ev_hg_out_gain[j], ev_w_out[j])
        else:
            x = _odd_layer(x, groups, od_norm[j], od_w_in[j], od_q_norm[j], od_kv_norm[j], od_w_uq[j],
                           od_w_ukv[j], od_q_gain[j], od_k_gain[j], od_w_out[j])
        x = _peer(x, ff_norm[layer], peer_w_query[layer], peer_sub_keys[layer], peer_u[layer], peer_v[layer])
    outs = []
    tok0 = 0
    for (B, S) in groups:
        outs.append(x[tok0:tok0 + B * S].reshape(B, S, D))
        tok0 += B * S
    return tuple(outs)


def kernel(x_prompt, x_sample, ev_norm, ev_w_in, ev_na_q_gain, ev_na_k_gain, ev_na_rpb, ev_hg_lb_logits, ev_hg_out_gain, ev_w_out, od_norm, od_w_in, od_q_norm, od_kv_norm, od_w_uq, od_w_ukv, od_q_gain, od_k_gain, od_w_out, ff_norm, peer_w_query, peer_sub_keys, peer_u, peer_v):
    return _forward((x_prompt, x_sample), ev_norm, ev_w_in, ev_na_q_gain, ev_na_k_gain, ev_na_rpb, ev_hg_lb_logits,
                    ev_hg_out_gain, ev_w_out, od_norm, od_w_in, od_q_norm, od_kv_norm, od_w_uq, od_w_ukv, od_q_gain,
                    od_k_gain, od_w_out, ff_norm, peer_w_query, peer_sub_keys, peer_u, peer_v)
```

```python
import functools

import numpy as np
import jax
import jax.numpy as jnp
from jax import lax
from jax.experimental import pallas as pl
from jax.experimental.pallas import tpu as pltpu

F32 = jnp.float32
BF16 = jnp.bfloat16

EPS = 1e-6
LANES = 128
VMEM_LIMIT_BYTES = 56 * 1024 * 1024
NEG_BIG = -1e30

GRID_W = 64
NA_HEADS = 8
NA_HEAD_DIM = 128
NA_WIN_ROWS = 8
NA_WIN_COLS = 16
NA_ROW_BLOCK = 8
NA_KEY_ROWS = 16
HG_HEADS = 8
HG_DIM = 128
HG_CHUNK = 32
MLA_HEADS = 16
MLA_Q_RANK = 768
MLA_KV_RANK = 512
MLA_NOPE = 128
MLA_ROPE = 64
MLA_QK = MLA_NOPE + MLA_ROPE
MLA_HEAD_PAD = 256
ROPE_BASE = 10000.0
PEER_HEADS = 8
PEER_N_KEYS = 128
PEER_TOPK = 16
PEER_HALF = 128


def _cparams(sem):
    return pltpu.CompilerParams(dimension_semantics=sem, vmem_limit_bytes=VMEM_LIMIT_BYTES)


def _rms(x, gain):
    return x * lax.rsqrt(jnp.mean(x * x, axis=-1, keepdims=True) + EPS) * gain


def _dot_nt(a, b):
    return lax.dot_general(a, b, (((1,), (1,)), ((), ())), preferred_element_type=F32)


def _dot_tn(a, b):
    return lax.dot_general(a, b, (((0,), (0,)), ((), ())), preferred_element_type=F32)


def _mm_kernel(*refs, has_norm, has_res):
    it = iter(refs)
    x_ref = next(it)
    g_ref = next(it) if has_norm else None
    w_ref = next(it)
    r_ref = next(it) if has_res else None
    o_ref = next(it)
    if has_norm:
        hn_ref = next(it)

        @pl.when(pl.program_id(1) == 0)
        def _():
            hn_ref[...] = _rms(x_ref[...].astype(F32), g_ref[...]).astype(BF16)

        a = hn_ref[...]
    else:
        a = x_ref[...].astype(BF16)
    acc = jnp.dot(a, w_ref[...], preferred_element_type=F32)
    if has_res:
        acc = acc + r_ref[...]
    o_ref[...] = acc.astype(o_ref.dtype)


def _matmul(x, w, *, gain=None, residual=None, out_dtype=F32, x_col_block=0, tm=512, tn=1024):
    M = x.shape[0]
    K, N = w.shape
    tn = min(tn, N)
    assert M % tm == 0 and N % tn == 0 and (x_col_block + 1) * K <= x.shape[1]
    has_norm, has_res = gain is not None, residual is not None
    in_specs = [pl.BlockSpec((tm, K), lambda i, j: (i, x_col_block))]
    args = [x]
    if has_norm:
        in_specs.append(pl.BlockSpec((1, K), lambda i, j: (0, 0)))
        args.append(gain.reshape(1, K).astype(F32))
    in_specs.append(pl.BlockSpec((K, tn), lambda i, j: (0, j)))
    args.append(w)
    if has_res:
        in_specs.append(pl.BlockSpec((tm, tn), lambda i, j: (i, j)))
        args.append(residual)
    return pl.pallas_call(
        functools.partial(_mm_kernel, has_norm=has_norm, has_res=has_res),
        grid=(M // tm, N // tn),
        in_specs=in_specs,
        out_specs=pl.BlockSpec((tm, tn), lambda i, j: (i, j)),
        out_shape=jax.ShapeDtypeStruct((M, N), out_dtype),
        scratch_shapes=[pltpu.VMEM((tm, K), BF16)] if has_norm else [],
        compiler_params=_cparams(("parallel", "arbitrary")),
        name="mm",
    )(*args)


NA_Q = NA_ROW_BLOCK * GRID_W
NA_KB = 256
NA_NKB = NA_KEY_ROWS * GRID_W // NA_KB


def _na_bias_tables(rpb):
    i = np.arange(NA_ROW_BLOCK)[:, None, None, None]
    c = np.arange(GRID_W)[None, :, None, None]
    j = np.arange(NA_KEY_ROWS)[None, None, :, None]
    kc = np.arange(GRID_W)[None, None, None, :]
    half = NA_WIN_ROWS // 2
    qc0 = np.clip(c - NA_WIN_COLS // 2, 0, GRID_W - NA_WIN_COLS)
    col_ok = (kc >= qc0) & (kc < qc0 + NA_WIN_COLS)
    dcol = np.clip(kc - c + NA_WIN_COLS - 1, 0, 2 * NA_WIN_COLS - 2)
    col_hot = (dcol[0, :, 0, :, None] == np.arange(2 * NA_WIN_COLS - 1)).astype(np.float32)
    by_col = jnp.einsum('hab,ckb->hack', rpb.astype(F32), col_hot, precision=lax.Precision.HIGHEST)
    tabs = []
    for var in range(3):
        if var == 0:
            r0 = np.maximum(i - half, 0)
            dr = j - i
        elif var == 1:
            r0 = i
            dr = j - half - i
        else:
            r0 = np.minimum(i + half, NA_KEY_ROWS - NA_WIN_ROWS)
            dr = j - (NA_KEY_ROWS - NA_ROW_BLOCK) - i
        ok = (j >= r0) & (j < r0 + NA_WIN_ROWS) & col_ok
        dri = np.clip(dr + NA_WIN_ROWS - 1, 0, 2 * NA_WIN_ROWS - 2)[:, 0, :, 0]
        row_hot = (dri[:, :, None] == np.arange(2 * NA_WIN_ROWS - 1)).astype(np.float32)
        vals = jnp.einsum('ija,hack->hicjk', row_hot, by_col, precision=lax.Precision.HIGHEST)
        tabs.append(jnp.where(jnp.asarray(ok)[None], vals, NEG_BIG).reshape(
            NA_HEADS, NA_Q, NA_KEY_ROWS * GRID_W))
    return jnp.stack(tabs)


def _na_steps(groups):
    steps = []
    tok0 = 0
    for (B, S) in groups:
        rows = S // GRID_W
        nrb = rows // NA_ROW_BLOCK
        assert nrb >= 3 and S % NA_Q == 0
        for b in range(B):
            for rb in range(nrb):
                var = 0 if rb == 0 else (2 if rb == nrb - 1 else 1)
                kr0 = min(max(NA_ROW_BLOCK * rb - NA_WIN_ROWS // 2, 0), rows - NA_KEY_ROWS)
                steps.append((var, (tok0 + b * S) // NA_Q + rb, (tok0 + b * S + kr0 * GRID_W) // NA_KB))
        tok0 += B * S
    steps.sort(key=lambda t: t[0])
    arr = np.asarray(steps, np.int32)
    return arr[:, 0], arr[:, 1], arr[:, 2]


def _na_kernel(var_ref, qb_ref, kb_ref, q_ref, *rest):
    k_refs = rest[:NA_NKB]
    v_refs = rest[NA_NKB:2 * NA_NKB]
    bias_ref, qg_ref, kg_ref, o_ref = rest[2 * NA_NKB:]
    scale = NA_HEAD_DIM ** -0.5
    q = _rms(q_ref[...].astype(F32), qg_ref[...]).astype(BF16)
    s = []
    for m in range(NA_NKB):
        k = _rms(k_refs[m][...].astype(F32), kg_ref[...]).astype(BF16)
        s.append(_dot_nt(q, k) * scale + bias_ref[0, 0, :, m * NA_KB:(m + 1) * NA_KB])
    mx = s[0].max(axis=-1, keepdims=True)
    for m in range(1, NA_NKB):
        mx = jnp.maximum(mx, s[m].max(axis=-1, keepdims=True))
    l = jnp.zeros_like(mx)
    acc = jnp.zeros((NA_Q, NA_HEAD_DIM), F32)
    for m in range(NA_NKB):
        p = jnp.exp(s[m] - mx)
        l = l + p.sum(axis=-1, keepdims=True)
        acc = acc + jnp.dot(p.astype(BF16), v_refs[m][...].astype(BF16), preferred_element_type=F32)
    o_ref[...] = (acc / l).astype(o_ref.dtype)


def _neighbourhood_attention(p, groups, rpb, q_gain, k_gain, out_cols):
    T = p.shape[0]
    var, qblk, kblk = _na_steps(groups)
    tabs = _na_bias_tables(rpb)
    H = NA_HEADS

    def qmap(h, n, var, qb, kb):
        return (qb[n], h)

    def kmap(m, off):
        return lambda h, n, var, qb, kb: (kb[n] + m, off + h)

    in_specs = [pl.BlockSpec((NA_Q, NA_HEAD_DIM), qmap)]
    in_specs += [pl.BlockSpec((NA_KB, NA_HEAD_DIM), kmap(m, H)) for m in range(NA_NKB)]
    in_specs += [pl.BlockSpec((NA_KB, NA_HEAD_DIM), kmap(m, 2 * H)) for m in range(NA_NKB)]
    in_specs += [
        pl.BlockSpec((1, 1, NA_Q, NA_KEY_ROWS * GRID_W), lambda h, n, var, qb, kb: (var[n], h, 0, 0)),
        pl.BlockSpec((1, NA_HEAD_DIM), lambda h, n, var, qb, kb: (0, 0)),
        pl.BlockSpec((1, NA_HEAD_DIM), lambda h, n, var, qb, kb: (0, 0)),
    ]
    return pl.pallas_call(
        _na_kernel,
        grid_spec=pltpu.PrefetchScalarGridSpec(
            num_scalar_prefetch=3,
            grid=(H, len(var)),
            in_specs=in_specs,
            out_specs=pl.BlockSpec((NA_Q, NA_HEAD_DIM), qmap),
        ),
        out_shape=jax.ShapeDtypeStruct((T, out_cols), BF16),
        compiler_params=_cparams(("parallel", "arbitrary")),
        name="natten",
    )(jnp.asarray(var), jnp.asarray(qblk), jnp.asarray(kblk), p, *([p] * (2 * NA_NKB)), tabs,
      q_gain.reshape(1, -1).astype(F32), k_gain.reshape(1, -1).astype(F32))


HG_BLOCK = 512
HG_NCHUNK = HG_BLOCK // HG_CHUNK
HG_SUB = 8
assert HG_CHUNK == 4 * HG_SUB


def _hg_steps(groups, reverse):
    blk, first = [], []
    tok0 = 0
    for (B, S) in groups:
        nb = S // HG_BLOCK
        assert S % HG_BLOCK == 0
        for b in range(B):
            order = range(nb - 1, -1, -1) if reverse else range(nb)
            for n, sb in enumerate(order):
                blk.append((tok0 + b * S) // HG_BLOCK + sb)
                first.append(1 if n == 0 else 0)
        tok0 += B * S
    return np.asarray(blk, np.int32), np.asarray(first, np.int32)


def _hg_matrices(reverse):
    C, S = HG_CHUNK, HG_SUB
    t = np.arange(C)[:, None]
    u = np.arange(C)[None, :]
    upto = (u >= t) if reverse else (u <= t)
    after = (u < t) if reverse else (u > t)
    same = lambda m: (u // m) == (t // m)
    sums = np.concatenate([upto, upto & same(S), upto & same(2 * S), after & same(S), after & same(2 * S)], axis=0)
    query_side = 0 if reverse else 1
    masks = []
    for m in (S, 2 * S):
        bt, bs = t // m, u // m
        masks.append((bt % 2 == query_side) & (bs == (bt + 1 if reverse else bt - 1)))
    return sums.astype(np.float32), np.stack(masks).astype(np.float32)


def _hg_kernel(blk_ref, first_ref, q_ref, f_ref, i_ref, lb_ref, sums_ref, mask_ref, *rest, reverse, finish):
    if finish:
        ofw_ref, g_ref, gain_ref, o_ref, st_ref = rest
    else:
        o_ref, st_ref = rest
    h = pl.program_id(0)
    n = pl.program_id(1)
    C = HG_CHUNK

    @pl.when(first_ref[n] == 1)
    def _():
        st_ref[...] = jnp.zeros_like(st_ref)

    lb = lb_ref[pl.ds(h, 1), :]
    sums = sums_ref[...]
    mask0, mask1 = mask_ref[0], mask_ref[1]
    S = HG_SUB
    NC = HG_NCHUNK
    TB = HG_BLOCK
    blk3 = (TB // S, S, HG_DIM)
    row8 = lax.broadcasted_iota(jnp.int32, blk3, 1)
    ones = jnp.ones((HG_DIM, HG_DIM), BF16)
    rows = lambda x, c: x[c * C:(c + 1) * C]

    f = lb + (1.0 - lb) * jax.nn.sigmoid(f_ref[...].astype(F32))
    logf = jnp.log(f)
    kk = 1.0 - f
    q = q_ref[...].astype(F32)
    q = q * jax.nn.sigmoid(q)
    v = i_ref[...].astype(F32)
    logf_hi = logf.astype(BF16)
    logf_lo = (logf - logf_hi.astype(F32)).astype(BF16)
    e_cat = (jnp.dot(sums, jnp.concatenate([rows(logf_hi, c) for c in range(NC)], axis=1), preferred_element_type=F32)
             + jnp.dot(sums, jnp.concatenate([rows(logf_lo, c) for c in range(NC)], axis=1),
                       preferred_element_type=F32))
    part = lambda r: jnp.concatenate(
        [e_cat[r * C:(r + 1) * C, c * HG_DIM:(c + 1) * HG_DIM] for c in range(NC)], axis=0)
    b, bloc, eq1, ek0, ek1 = [part(r) for r in range(5)]
    b_end = jnp.concatenate(
        [jnp.broadcast_to(rows(b, c)[0:1] if reverse else rows(b, c)[C - 1:C], (C, HG_DIM)) for c in range(NC)], axis=0)
    qb = (q * jnp.exp(b)).astype(BF16)
    q0, k0 = (q * jnp.exp(bloc)).astype(BF16), (kk * jnp.exp(ek0)).astype(BF16)
    q1, k1 = (q * jnp.exp(eq1)).astype(BF16), (kk * jnp.exp(ek1)).astype(BF16)
    ks = (kk * jnp.exp(b_end - b)).astype(BF16)
    vb = v.astype(BF16)
    o_off, upd = [], []
    for c in range(NC):
        a_off = (_dot_nt(rows(q0, c), rows(k0, c)) * mask0 + _dot_nt(rows(q1, c), rows(k1, c)) * mask1)
        o_off.append(jnp.dot(a_off.astype(BF16), rows(vb, c), preferred_element_type=F32))
        upd.append(_dot_tn(rows(vb, c), rows(ks, c)))
    q3, k3, v3, b3 = q.reshape(blk3), kk.reshape(blk3), v.reshape(blk3), bloc.reshape(blk3)
    prods = []
    for ds in range(S):
        keep = (row8 <= ds) if reverse else (row8 >= ds)
        e = jnp.exp(jnp.where(keep, b3 - b3[:, ds:ds + 1, :], -jnp.inf))
        prods.append((q3 * e * k3[:, ds:ds + 1, :]).reshape(TB, HG_DIM).astype(BF16))
    a_bc = jnp.dot(jnp.concatenate(prods, axis=0), ones, preferred_element_type=F32)
    o_diag = jnp.zeros(blk3, F32)
    for ds in range(S):
        o_diag = o_diag + a_bc[ds * TB:(ds + 1) * TB, :].reshape(blk3) * v3[:, ds:ds + 1, :]
    st = st_ref[...]
    o_inter = [None] * NC
    for cc in range(NC):
        c = NC - 1 - cc if reverse else cc
        o_inter[c] = _dot_nt(rows(qb, c), st.astype(BF16))
        st = st * jnp.exp(rows(b_end, c)[0:1]) + upd[c]
    st_ref[...] = st
    o_blk = jnp.concatenate([o_inter[c] + o_off[c] for c in range(NC)], axis=0) + o_diag.reshape(TB, HG_DIM)

    if finish:
        o = o_blk + ofw_ref[...]
        g = g_ref[...].astype(F32)
        o_ref[...] = (_rms(o, gain_ref[...]) * (g * jax.nn.sigmoid(g))).astype(o_ref.dtype)
    else:
        o_ref[...] = o_blk


def _hgrn2(p, groups, lb, *, reverse, f_col, o_fw=None, out_gain=None, mix=None):
    T = p.shape[0]
    blk, first = _hg_steps(groups, reverse)
    H = HG_HEADS
    finish = o_fw is not None
    C = HG_CHUNK
    sums, masks = _hg_matrices(reverse)

    def col(off):
        return lambda h, n, blk, first: (blk[n], off + h)

    const = lambda h, n, blk, first: (0, 0)
    in_specs = [
        pl.BlockSpec((HG_BLOCK, HG_DIM), col(3 * NA_HEADS)),
        pl.BlockSpec((HG_BLOCK, HG_DIM), col(f_col)),
        pl.BlockSpec((HG_BLOCK, HG_DIM), col(6 * NA_HEADS)),
        pl.BlockSpec((H, HG_DIM), const),
        pl.BlockSpec(sums.shape, const),
        pl.BlockSpec(masks.shape, lambda h, n, blk, first: (0, 0, 0)),
    ]
    args = [p, p, p, lb.reshape(H, HG_DIM).astype(F32), jnp.asarray(sums, BF16), jnp.asarray(masks)]
    aliases = {}
    if finish:
        in_specs += [
            pl.BlockSpec((HG_BLOCK, HG_DIM), col(0)),
            pl.BlockSpec((HG_BLOCK, HG_DIM), col(7 * NA_HEADS)),
            pl.BlockSpec((1, HG_DIM), const),
            pl.BlockSpec(memory_space=pl.ANY),
        ]
        args += [o_fw, p, out_gain.reshape(1, HG_DIM).astype(F32), mix]
        aliases = {2 + len(args) - 1: 0}
        out_spec = pl.BlockSpec((HG_BLOCK, HG_DIM), col(NA_HEADS))
        out_shape = jax.ShapeDtypeStruct(mix.shape, mix.dtype)
    else:
        out_spec = pl.BlockSpec((HG_BLOCK, HG_DIM), col(0))
        out_shape = jax.ShapeDtypeStruct((T, H * HG_DIM), F32)

    def body(blk_ref, first_ref, *refs):
        if finish:
            refs = refs[:9] + refs[10:]
        _hg_kernel(blk_ref, first_ref, *refs, reverse=reverse, finish=finish)

    return pl.pallas_call(
        body,
        grid_spec=pltpu.PrefetchScalarGridSpec(
            num_scalar_prefetch=2,
            grid=(H, len(blk)),
            in_specs=in_specs,
            out_specs=out_spec,
            scratch_shapes=[
                pltpu.VMEM((HG_DIM, HG_DIM), F32),
            ],
        ),
        out_shape=out_shape,
        input_output_aliases=aliases,
        compiler_params=_cparams(("parallel", "arbitrary")),
        name="hgrn2_bw" if reverse else "hgrn2_fw",
    )(jnp.asarray(blk), jnp.asarray(first), *args)


def _rope_tables(groups):
    half = MLA_ROPE // 2
    smax = max(S for _, S in groups)
    pos = jnp.arange(smax, dtype=F32)
    inv = 1.0 / (ROPE_BASE ** (jnp.arange(0, MLA_ROPE, 2, dtype=F32) / MLA_ROPE))
    ang = pos[:, None] * inv[None, :]
    cos, sin = jnp.cos(ang), jnp.sin(ang)
    z = jnp.zeros((smax, LANES - 2 * half), F32)
    zh = jnp.zeros((smax, half), F32)
    c = jnp.concatenate([cos, cos, z], axis=1)
    s_from_x2 = jnp.concatenate([-sin, zh, z], axis=1)
    s_from_x1 = jnp.concatenate([zh, sin, z], axis=1)
    return c, s_from_x2, s_from_x1


def _rope_block(t, c, s2, s1):
    half = MLA_ROPE // 2
    return t * c + pltpu.roll(t, LANES - half, 1) * s2 + pltpu.roll(t, half, 1) * s1


def _qprep_kernel(q_ref, g_ref, c_ref, s2_ref, s1_ref, o_ref, *, scale):
    c, s2, s1 = c_ref[...], s2_ref[...], s1_ref[...]
    for h in range(MLA_HEADS):
        a = q_ref[:, h * MLA_HEAD_PAD:h * MLA_HEAD_PAD + LANES].astype(F32)
        r = q_ref[:, h * MLA_HEAD_PAD + LANES:(h + 1) * MLA_HEAD_PAD].astype(F32)
        ms = (jnp.sum(a * a, axis=-1, keepdims=True) + jnp.sum(r * r, axis=-1, keepdims=True)) / MLA_QK
        inv = lax.rsqrt(ms + EPS)
        o_ref[:, h * MLA_HEAD_PAD:h * MLA_HEAD_PAD + LANES] = (a * inv * g_ref[:, :LANES] * scale).astype(o_ref.dtype)
        rn = r * inv * g_ref[:, LANES:]
        o_ref[:, h * MLA_HEAD_PAD + LANES:(h + 1) * MLA_HEAD_PAD] = (_rope_block(rn, c, s2, s1) * scale).astype(o_ref.dtype)


def _kprep_kernel(kv_ref, kr_ref, g_ref, c_ref, s2_ref, s1_ref, o_ref, vt_ref):
    c, s2, s1 = c_ref[...], s2_ref[...], s1_ref[...]
    r = kr_ref[...].astype(F32)
    rs = jnp.sum(r * r, axis=-1, keepdims=True)
    for h in range(MLA_HEADS):
        a = kv_ref[:, h * MLA_HEAD_PAD:h * MLA_HEAD_PAD + LANES].astype(F32)
        inv = lax.rsqrt((jnp.sum(a * a, axis=-1, keepdims=True) + rs) / MLA_QK + EPS)
        o_ref[:, h * MLA_HEAD_PAD:h * MLA_HEAD_PAD + LANES] = (a * inv * g_ref[:, :LANES]).astype(o_ref.dtype)
        rn = r * inv * g_ref[:, LANES:]
        o_ref[:, h * MLA_HEAD_PAD + LANES:(h + 1) * MLA_HEAD_PAD] = _rope_block(rn, c, s2, s1).astype(o_ref.dtype)
        v = kv_ref[:, h * MLA_HEAD_PAD + LANES:(h + 1) * MLA_HEAD_PAD].astype(F32)
        vt_ref[h * LANES:(h + 1) * LANES, :] = v.T.astype(vt_ref.dtype)


def _pos_block_map(groups, tm):
    bounds = []
    tok0 = 0
    for (B, S) in groups:
        assert S % tm == 0
        bounds.append((tok0 // tm, S // tm))
        tok0 += B * S

    def imap(i):
        out = (i - bounds[0][0]) % bounds[0][1]
        for (b0, per) in bounds[1:]:
            out = jnp.where(i >= b0, (i - b0) % per, out)
        return (out, 0)

    return imap


def _pad_gain(g):
    return jnp.concatenate([g.astype(F32), jnp.zeros((MLA_HEAD_PAD - MLA_QK,), F32)]).reshape(1, MLA_HEAD_PAD)


def _mla_prep(q, kv, cc, kr_block, groups, q_gain, k_gain, tm=256):
    T = q.shape[0]
    W = MLA_HEADS * MLA_HEAD_PAD
    tabs = _rope_tables(groups)
    pmap = _pos_block_map(groups, tm)
    tab_specs = [pl.BlockSpec((tm, LANES), pmap)] * 3
    row = pl.BlockSpec((tm, W), lambda i: (i, 0))
    gspec = pl.BlockSpec((1, MLA_HEAD_PAD), lambda i: (0, 0))
    qf = pl.pallas_call(
        functools.partial(_qprep_kernel, scale=MLA_QK ** -0.5 * LOG2_E),
        grid=(T // tm,),
        in_specs=[row, gspec] + tab_specs,
        out_specs=row,
        out_shape=jax.ShapeDtypeStruct((T, W), BF16),
        compiler_params=_cparams(("parallel",)),
        name="mla_qprep",
    )(q, _pad_gain(q_gain), *tabs)
    kf, vt = pl.pallas_call(
        _kprep_kernel,
        grid=(T // tm,),
        in_specs=[row, pl.BlockSpec((tm, LANES), lambda i: (i, kr_block)), gspec] + tab_specs,
        out_specs=[row, pl.BlockSpec((MLA_HEADS * LANES, tm), lambda i: (0, i))],
        out_shape=[jax.ShapeDtypeStruct((T, W), BF16), jax.ShapeDtypeStruct((MLA_HEADS * LANES, T), BF16)],
        compiler_params=_cparams(("parallel",)),
        name="mla_kprep",
    )(kv, cc, _pad_gain(k_gain), *tabs)
    return qf, kf, vt


FLASH_SUB = 512
LOG2_E = 1.4426950408889634


def _attn_kernel(q_ref, k_ref, vt_ref, *rest):
    o_ref = rest[-1]
    q = q_ref[...]
    nsub = k_ref.shape[0] // FLASH_SUB
    s = [_dot_nt(k_ref[c * FLASH_SUB:(c + 1) * FLASH_SUB, :], q) for c in range(nsub)]
    m = s[0].max(axis=0, keepdims=True)
    for c in range(1, nsub):
        m = jnp.maximum(m, s[c].max(axis=0, keepdims=True))
    l = jnp.zeros_like(m)
    acc = jnp.zeros((vt_ref.shape[0], q.shape[0]), F32)
    for c in range(nsub):
        p = jnp.exp2(s[c] - m)
        l = l + p.sum(axis=0, keepdims=True)
        acc = acc + jnp.dot(vt_ref[:, c * FLASH_SUB:(c + 1) * FLASH_SUB], p.astype(BF16), preferred_element_type=F32)
    o_ref[...] = (acc / l).T.astype(o_ref.dtype)


MLA_MAX_SEQ = 16384


def _mla_attention(qf, kf, vt, groups, tq=512):
    T = qf.shape[0]
    H = MLA_HEADS
    out = None
    tok0 = 0
    for (B, S) in groups:
        assert S % tq == 0 and S % FLASH_SUB == 0 and tok0 % S == 0 and S <= MLA_MAX_SEQ
        nq = S // tq
        q0, s0 = tok0 // tq, tok0 // S
        once = pl.Buffered(1)
        in_specs = [
            pl.BlockSpec((tq, MLA_HEAD_PAD), lambda b, h, qi, q0=q0, nq=nq: (q0 + b * nq + qi, h)),
            pl.BlockSpec((S, MLA_HEAD_PAD), lambda b, h, qi, s0=s0: (s0 + b, h), pipeline_mode=once),
            pl.BlockSpec((LANES, S), lambda b, h, qi, s0=s0: (h, s0 + b), pipeline_mode=once),
        ]
        args = [qf, kf, vt]
        aliases = {}
        if out is not None:
            in_specs.append(pl.BlockSpec(memory_space=pl.ANY))
            args.append(out)
            aliases = {3: 0}
        out = pl.pallas_call(
            _attn_kernel,
            grid=(B, H, nq),
            in_specs=in_specs,
            out_specs=pl.BlockSpec((tq, LANES), lambda b, h, qi, q0=q0, nq=nq: (q0 + b * nq + qi, h)),
            out_shape=jax.ShapeDtypeStruct((T, H * LANES), BF16),
            input_output_aliases=aliases,
            compiler_params=_cparams(("parallel", "parallel", "parallel")),
            name="mla_attn",
        )(*args)
        tok0 += B * S
    return out


def _peer_candidates():
    a, b = [], []
    a += [0] * 16; b += list(range(16))
    for aa in (1, 2, 3):
        a += [aa] * 8; b += list(range(8))
    for bb in (0, 1):
        a += [4, 5, 6, 7]; b += [bb] * 4
    a += list(range(8, 16)); b += [0] * 8
    for bb in (2, 3):
        a += [4, 5, 6, 7]; b += [bb] * 4
    return np.asarray(a), np.asarray(b)


_CAND_A, _CAND_B = _peer_candidates()
N_CAND = len(_CAND_A)


def _extract16(val, order, exact):
    step = jnp.full(val.shape, float(PEER_TOPK), F32)
    tops = []
    for r in range(PEER_TOPK):
        m = val.max(axis=0, keepdims=True)
        if exact:
            first = jnp.where(val == m, order, 1e9).min(axis=0, keepdims=True)
            sel = order == first
        else:
            sel = val == m
        step = jnp.where(sel, float(r), step)
        val = jnp.where(sel, -jnp.inf, val)
        tops.append(m)
    return jnp.concatenate(tops, axis=0), step


def _route_head(s1, s2, pos, amat, exact):
    tm = s1.shape[1]
    idx = lax.broadcasted_iota(jnp.int32, s1.shape, 0).astype(F32)
    t1, r1 = _extract16(s1, idx, exact)
    t2, r2 = _extract16(s2, idx, exact)
    mid1 = jnp.concatenate([t1[4:8], t1[4:8]], axis=0)
    rep2 = lambda b: jnp.broadcast_to(t2[b:b + 1], (4, tm))
    cand = jnp.concatenate([
        t1[0:1] + t2, t1[1:2] + t2[0:8], t1[2:3] + t2[0:8], t1[3:4] + t2[0:8],
        mid1 + jnp.concatenate([rep2(0), rep2(1)], axis=0),
        t1[8:16] + t2[0:1],
        mid1 + jnp.concatenate([rep2(2), rep2(3)], axis=0)], axis=0)
    best = cand[0:1]
    _, cstep = _extract16(cand, pos, exact)
    chosen = (cstep < float(PEER_TOPK)).astype(F32)
    z = jnp.sum(chosen * jnp.exp(cand - best), axis=0, keepdims=True)
    per_a = jnp.dot(amat, chosen.astype(BF16), preferred_element_type=F32)
    cnt = jnp.zeros(s1.shape, F32)
    for a in range(PEER_TOPK):
        cnt = jnp.where(r1 == float(a), per_a[a:a + 1], cnt)
    e1n = jnp.exp(s1 - t1[0:1]) / z
    e2 = jnp.exp(s2 - t2[0:1])
    removed = lambda st: jnp.sum((st < float(PEER_TOPK)).astype(F32), axis=0, keepdims=True)
    ties = (jnp.abs(removed(r1) - PEER_TOPK) + jnp.abs(removed(r2) - PEER_TOPK)
            + jnp.abs(jnp.sum(chosen, axis=0, keepdims=True) - PEER_TOPK))
    return cnt, e1n, r2, e2, ties


def _peer_route_kernel(x_ref, g_ref, wq_ref, keys_ref, pos_ref, amat_ref,
                       hn_ref, cnt_ref, e1_ref, rb_ref, e2_ref, sc_ref):
    tm = x_ref.shape[0]
    hn = _rms(x_ref[...].astype(F32), g_ref[...]).astype(BF16)
    hn_ref[...] = hn
    qry = jnp.dot(hn, wq_ref[...], preferred_element_type=F32)
    for hp in range(2 * PEER_HEADS):
        sc_ref[hp] = _dot_nt(keys_ref[hp // 2, hp % 2], qry[:, hp * PEER_HALF:(hp + 1) * PEER_HALF].astype(BF16))

    def route(exact):
        ties = jnp.zeros((1, tm), F32)
        for h in range(PEER_HEADS):
            cnt, e1n, r2, e2, t = _route_head(sc_ref[2 * h], sc_ref[2 * h + 1], pos_ref[...], amat_ref[...], exact)
            cnt_ref[h] = cnt
            e1_ref[h] = e1n
            rb_ref[h] = pltpu.bitcast(r2.astype(BF16), jnp.uint32)
            e2_ref[h] = pltpu.bitcast(e2.astype(BF16), jnp.uint32)
            ties = ties + t
        return ties

    ties = route(exact=False)

    @pl.when(jnp.max(ties) > 0.0)
    def _():
        route(exact=True)


def _peer_route(x, norm_gain, w_query, sub_keys, tm=256):
    T, D = x.shape
    H = PEER_HEADS
    pos = np.broadcast_to((_CAND_A * PEER_TOPK + _CAND_B).astype(np.float32)[:, None], (N_CAND, tm))
    amat = (np.arange(PEER_TOPK)[:, None] == _CAND_A[None, :]).astype(np.float32)
    gate_shape = jax.ShapeDtypeStruct((H, PEER_N_KEYS, T), F32)
    packed_shape = jax.ShapeDtypeStruct((H, PEER_N_KEYS // 2, T), jnp.uint32)
    gate_spec = pl.BlockSpec((H, PEER_N_KEYS, tm), lambda i: (0, 0, i))
    packed_spec = pl.BlockSpec((H, PEER_N_KEYS // 2, tm), lambda i: (0, 0, i))
    return pl.pallas_call(
        _peer_route_kernel,
        grid=(T // tm,),
        in_specs=[
            pl.BlockSpec((tm, D), lambda i: (i, 0)),
            pl.BlockSpec((1, D), lambda i: (0, 0)),
            pl.BlockSpec(w_query.shape, lambda i: (0, 0)),
            pl.BlockSpec(sub_keys.shape, lambda i: (0, 0, 0, 0)),
            pl.BlockSpec((N_CAND, tm), lambda i: (0, 0)),
            pl.BlockSpec((PEER_TOPK, N_CAND), lambda i: (0, 0)),
        ],
        out_specs=[pl.BlockSpec((tm, D), lambda i: (i, 0)), gate_spec, gate_spec, packed_spec, packed_spec],
        out_shape=[jax.ShapeDtypeStruct((T, D), BF16), gate_shape, gate_shape, packed_shape, packed_shape],
        scratch_shapes=[pltpu.VMEM((2 * H, PEER_N_KEYS, tm), F32)],
        compiler_params=_cparams(("parallel",)),
        name="peer_route",
    )(x, norm_gain.reshape(1, D).astype(F32), w_query, sub_keys, jnp.asarray(pos), jnp.asarray(amat, BF16))


def _gelu(a):
    return 0.5 * a * (1.0 + lax.erf(a * (2.0 ** -0.5)))


def _gate_weights(cnt_ref, e1_ref, rb_ref, e2_ref, w_ref, step, slot, rows_per_step):
    tile = (PEER_N_KEYS, LANES)
    half = PEER_N_KEYS // 2
    for il in range(rows_per_step):
        i = step * rows_per_step + il
        cnt = [cnt_ref[pl.ds(h * PEER_N_KEYS + i, 1), :] for h in range(PEER_HEADS)]
        e1 = [e1_ref[pl.ds(h * PEER_N_KEYS + i, 1), :] for h in range(PEER_HEADS)]
        for st in range(w_ref.shape[2] // LANES):
            ls = slice(st * LANES, (st + 1) * LANES)
            w = jnp.zeros(tile, BF16)
            for h in range(PEER_HEADS):
                hs = slice(h * half, (h + 1) * half)
                rb = pltpu.bitcast(rb_ref[hs, ls], BF16)
                e2 = pltpu.bitcast(e2_ref[hs, ls], BF16)
                cnt_t = jnp.broadcast_to(cnt[h][:, ls], tile).astype(BF16)
                e1_t = jnp.broadcast_to(e1[h][:, ls], tile).astype(BF16)
                w = w + jnp.where(rb < cnt_t, e2, 0.0) * e1_t
            w_ref[slot, il * half:(il + 1) * half, ls] = pltpu.bitcast(w, jnp.uint32)


def _peer_expert_kernel(hn_ref, u_ref, vt_ref, cnt_ref, e1_ref, rb_ref, e2_ref, x_ref, o_ref, acc_ref, g_ref, w_ref,
                        *, rows_per_step):
    e = pl.program_id(1)
    gate = functools.partial(_gate_weights, cnt_ref, e1_ref, rb_ref, e2_ref, w_ref, rows_per_step=rows_per_step)

    @pl.when(e == 0)
    def _():
        acc_ref[...] = jnp.zeros_like(acc_ref)
        gate(step=0, slot=0)

    gate(step=jnp.minimum(e + 1, pl.num_programs(1) - 1), slot=(e + 1) % 2)
    act = _gelu(_dot_nt(u_ref[...], hn_ref[...]))
    g_ref[...] = (pltpu.bitcast(w_ref[e % 2], BF16).astype(F32) * act).astype(BF16)
    acc_ref[...] += jnp.dot(vt_ref[...], g_ref[...], preferred_element_type=F32)

    @pl.when(e == pl.num_programs(1) - 1)
    def _():
        o_ref[...] = x_ref[...] + acc_ref[...].T


def _peer_experts(x, hn, gates, u, vt, tm=512, te=1024):
    T, D = x.shape
    NE = u.shape[0]
    gates = [g.reshape(-1, T) for g in gates]
    once = pl.Buffered(1)
    gate_spec = pl.BlockSpec((PEER_HEADS * PEER_N_KEYS, tm), lambda i, e: (0, i), pipeline_mode=once)
    packed_spec = pl.BlockSpec((PEER_HEADS * PEER_N_KEYS // 2, tm), lambda i, e: (0, i), pipeline_mode=once)
    return pl.pallas_call(
        functools.partial(_peer_expert_kernel, rows_per_step=te // PEER_N_KEYS),
        grid=(T // tm, NE // te),
        in_specs=[
            pl.BlockSpec((tm, D), lambda i, e: (i, 0), pipeline_mode=once),
            pl.BlockSpec((te, D), lambda i, e: (e, 0)),
            pl.BlockSpec((D, te), lambda i, e: (0, e)),
            gate_spec, gate_spec, packed_spec, packed_spec,
            pl.BlockSpec((tm, D), lambda i, e: (i, 0), pipeline_mode=once),
        ],
        out_specs=pl.BlockSpec((tm, D), lambda i, e: (i, 0)),
        out_shape=jax.ShapeDtypeStruct((T, D), F32),
        scratch_shapes=[pltpu.VMEM((D, tm), F32), pltpu.VMEM((te, tm), BF16),
                        pltpu.VMEM((2, te // 2, tm), jnp.uint32)],
        compiler_params=_cparams(("parallel", "arbitrary")),
        name="peer_experts",
    )(hn, u, vt, *gates, x)


def _peer(x, norm_gain, w_query, sub_keys, u, v):
    hn, cnt, e1, rb, e2 = _peer_route(x, norm_gain, w_query.astype(BF16), sub_keys.astype(BF16))
    return _peer_experts(x, hn, (cnt, e1, rb, e2), u.astype(BF16), v.astype(BF16).T)


def _even_layer(x, groups, norm, w_in, na_q_gain, na_k_gain, na_rpb, lb, hg_out_gain, w_out):
    p = _matmul(x, w_in.astype(BF16), gain=norm, out_dtype=F32, tm=1024)
    mix_cols = NA_HEADS * NA_HEAD_DIM + HG_HEADS * HG_DIM
    mix = _neighbourhood_attention(p, groups, na_rpb, na_q_gain, na_k_gain, mix_cols)
    o_fw = _hgrn2(p, groups, lb[0], reverse=False, f_col=4 * NA_HEADS)
    mix = _hgrn2(p, groups, lb[1], reverse=True, f_col=5 * NA_HEADS, o_fw=o_fw, out_gain=hg_out_gain, mix=mix)
    return _matmul(mix, w_out.astype(BF16), residual=x, out_dtype=F32)


def _odd_layer(x, groups, norm, w_in, q_norm, kv_norm, w_uq, w_ukv, q_gain, k_gain, w_out):
    D = x.shape[1]
    pad_q = (-MLA_Q_RANK) % MLA_KV_RANK
    w_cq, w_ckv, w_kr = jnp.split(w_in, [MLA_Q_RANK, MLA_Q_RANK + MLA_KV_RANK], axis=1)
    w_in_p = jnp.concatenate(
        [w_cq, jnp.zeros((D, pad_q), w_in.dtype), w_ckv, w_kr, jnp.zeros((D, LANES - MLA_ROPE), w_in.dtype)],
        axis=1).astype(BF16)
    cc = _matmul(x, w_in_p, gain=norm, out_dtype=F32, tn=w_in_p.shape[1])
    w_uq_p = jnp.pad(w_uq.reshape(MLA_Q_RANK, MLA_HEADS, MLA_QK),
                     ((0, 0), (0, 0), (0, MLA_HEAD_PAD - MLA_QK))).reshape(MLA_Q_RANK, -1).astype(BF16)
    q = _matmul(cc, w_uq_p, gain=q_norm, out_dtype=BF16, x_col_block=0)
    kv = _matmul(cc, w_ukv.astype(BF16), gain=kv_norm, out_dtype=BF16,
                 x_col_block=(MLA_Q_RANK + pad_q) // MLA_KV_RANK)
    kr_block = (MLA_Q_RANK + pad_q + MLA_KV_RANK) // LANES
    qf, kf, vt = _mla_prep(q, kv, cc, kr_block, groups, q_gain, k_gain)
    o = _mla_attention(qf, kf, vt, groups)
    return _matmul(o, w_out.astype(BF16), residual=x, out_dtype=F32)


def _forward(xs, ev_norm, ev_w_in, ev_na_q_gain, ev_na_k_gain, ev_na_rpb, ev_hg_lb_logits, ev_hg_out_gain,
             ev_w_out, od_norm, od_w_in, od_q_norm, od_kv_norm, od_w_uq, od_w_ukv, od_q_gain, od_k_gain,
             od_w_out, ff_norm, peer_w_query, peer_sub_keys, peer_u, peer_v):
    groups = [(int(x.shape[0]), int(x.shape[1])) for x in xs]
    D = xs[0].shape[-1]
    depth = ff_norm.shape[0]
    lb_all = jnp.cumsum(jax.nn.softmax(ev_hg_lb_logits.astype(F32), axis=1), axis=1)
    x = jnp.concatenate([t.reshape(-1, D) for t in xs], axis=0)
    for layer in range(depth):
        j = layer // 2
        if layer % 2 == 0:
            x = _even_layer(x, groups, ev_norm[j], ev_w_in[j], ev_na_q_gain[j], ev_na_k_gain[j], ev_na_rpb[j],
                            lb_all[:, j], ev_hg_out_gain[j], ev_w_out[j])
        else:
            x = _odd_layer(x, groups, od_norm[j], od_w_in[j], od_q_norm[j], od_kv_norm[j], od_w_uq[j],
                           od_w_ukv[j], od_q_gain[j], od_k_gain[j], od_w_out[j])
        x = _peer(x, ff_norm[layer], peer_w_query[layer], peer_sub_keys[layer], peer_u[layer], peer_v[layer])
    outs = []
    tok0 = 0
    for (B, S) in groups:
        outs.append(x[tok0:tok0 + B * S].reshape(B, S, D))
        tok0 += B * S
    return tuple(outs)


def kernel(x_prompt, x_sample, ev_norm, ev_w_in, ev_na_q_gain, ev_na_k_gain, ev_na_rpb, ev_hg_lb_logits, ev_hg_out_gain, ev_w_out, od_norm, od_w_in, od_q_norm, od_kv_norm, od_w_uq, od_w_ukv, od_q_gain, od_k_gain, od_w_out, ff_norm, peer_w_query, peer_sub_keys, peer_u, peer_v):
    return _forward((x_prompt, x_sample), ev_norm, ev_w_in, ev_na_q_gain, ev_na_k_gain, ev_na_rpb, ev_hg_lb_logits,
                    ev_hg_out_gain, ev_w_out, od_norm, od_w_in, od_q_norm, od_kv_norm, od_w_uq, od_w_ukv, od_q_gain,
                    od_k_gain, od_w_out, ff_norm, peer_w_query, peer_sub_keys, peer_u, peer_v)
```

```python
import functools

import numpy as np
import jax
import jax.numpy as jnp
from jax import lax
from jax.experimental import pallas as pl
from jax.experimental.pallas import tpu as pltpu

F32 = jnp.float32
BF16 = jnp.bfloat16

EPS = 1e-6
LANES = 128
VMEM_LIMIT_BYTES = 56 * 1024 * 1024
NEG_BIG = -1e30

GRID_W = 64
NA_HEADS = 8
NA_HEAD_DIM = 128
NA_WIN_ROWS = 8
NA_WIN_COLS = 16
NA_ROW_BLOCK = 8
NA_KEY_ROWS = 16
HG_HEADS = 8
HG_DIM = 128
HG_CHUNK = 32
MLA_HEADS = 16
MLA_Q_RANK = 768
MLA_KV_RANK = 512
MLA_NOPE = 128
MLA_ROPE = 64
MLA_QK = MLA_NOPE + MLA_ROPE
MLA_HEAD_PAD = 256
ROPE_BASE = 10000.0
PEER_HEADS = 8
PEER_N_KEYS = 128
PEER_TOPK = 16
PEER_HALF = 128


def _cparams(sem):
    return pltpu.CompilerParams(dimension_semantics=sem, vmem_limit_bytes=VMEM_LIMIT_BYTES)


def _rms(x, gain):
    return x * lax.rsqrt(jnp.mean(x * x, axis=-1, keepdims=True) + EPS) * gain


def _dot_nt(a, b):
    return lax.dot_general(a, b, (((1,), (1,)), ((), ())), preferred_element_type=F32)


def _dot_tn(a, b):
    return lax.dot_general(a, b, (((0,), (0,)), ((), ())), preferred_element_type=F32)


def _mm_kernel(*refs, has_norm, has_res):
    it = iter(refs)
    x_ref = next(it)
    g_ref = next(it) if has_norm else None
    w_ref = next(it)
    r_ref = next(it) if has_res else None
    o_ref = next(it)
    if has_norm:
        hn_ref = next(it)

        @pl.when(pl.program_id(1) == 0)
        def _():
            hn_ref[...] = _rms(x_ref[...].astype(F32), g_ref[...]).astype(BF16)

        a = hn_ref[...]
    else:
        a = x_ref[...].astype(BF16)
    acc = jnp.dot(a, w_ref[...], preferred_element_type=F32)
    if has_res:
        acc = acc + r_ref[...]
    o_ref[...] = acc.astype(o_ref.dtype)


def _matmul(x, w, *, gain=None, residual=None, out_dtype=F32, x_col_block=0, tm=1024, tn=1024):
    M = x.shape[0]
    K, N = w.shape
    tn = min(tn, N)
    assert M % tm == 0 and N % tn == 0 and (x_col_block + 1) * K <= x.shape[1]
    has_norm, has_res = gain is not None, residual is not None
    in_specs = [pl.BlockSpec((tm, K), lambda i, j: (i, x_col_block))]
    args = [x]
    if has_norm:
        in_specs.append(pl.BlockSpec((1, K), lambda i, j: (0, 0)))
        args.append(gain.reshape(1, K).astype(F32))
    in_specs.append(pl.BlockSpec((K, tn), lambda i, j: (0, j)))
    args.append(w)
    if has_res:
        in_specs.append(pl.BlockSpec((tm, tn), lambda i, j: (i, j)))
        args.append(residual)
    return pl.pallas_call(
        functools.partial(_mm_kernel, has_norm=has_norm, has_res=has_res),
        grid=(M // tm, N // tn),
        in_specs=in_specs,
        out_specs=pl.BlockSpec((tm, tn), lambda i, j: (i, j)),
        out_shape=jax.ShapeDtypeStruct((M, N), out_dtype),
        scratch_shapes=[pltpu.VMEM((tm, K), BF16)] if has_norm else [],
        compiler_params=_cparams(("parallel", "arbitrary")),
        name="mm",
    )(*args)


NA_Q = NA_ROW_BLOCK * GRID_W
NA_KB = 256
NA_NKB = NA_KEY_ROWS * GRID_W // NA_KB


def _na_bias_tables(rpb):
    i = np.arange(NA_ROW_BLOCK)[:, None, None, None]
    c = np.arange(GRID_W)[None, :, None, None]
    j = np.arange(NA_KEY_ROWS)[None, None, :, None]
    kc = np.arange(GRID_W)[None, None, None, :]
    half = NA_WIN_ROWS // 2
    qc0 = np.clip(c - NA_WIN_COLS // 2, 0, GRID_W - NA_WIN_COLS)
    col_ok = (kc >= qc0) & (kc < qc0 + NA_WIN_COLS)
    dcol = np.clip(kc - c + NA_WIN_COLS - 1, 0, 2 * NA_WIN_COLS - 2)
    col_hot = (dcol[0, :, 0, :, None] == np.arange(2 * NA_WIN_COLS - 1)).astype(np.float32)
    by_col = jnp.einsum('hab,ckb->hack', rpb.astype(F32), col_hot, precision=lax.Precision.HIGHEST)
    tabs = []
    for var in range(3):
        if var == 0:
            r0 = np.maximum(i - half, 0)
            dr = j - i
        elif var == 1:
            r0 = i
            dr = j - half - i
        else:
            r0 = np.minimum(i + half, NA_KEY_ROWS - NA_WIN_ROWS)
            dr = j - (NA_KEY_ROWS - NA_ROW_BLOCK) - i
        ok = (j >= r0) & (j < r0 + NA_WIN_ROWS) & col_ok
        dri = np.clip(dr + NA_WIN_ROWS - 1, 0, 2 * NA_WIN_ROWS - 2)[:, 0, :, 0]
        row_hot = (dri[:, :, None] == np.arange(2 * NA_WIN_ROWS - 1)).astype(np.float32)
        vals = jnp.einsum('ija,hack->hicjk', row_hot, by_col, precision=lax.Precision.HIGHEST)
        tabs.append(jnp.where(jnp.asarray(ok)[None], vals, NEG_BIG).reshape(
            NA_HEADS, NA_Q, NA_KEY_ROWS * GRID_W))
    return jnp.stack(tabs)


def _na_steps(groups):
    steps = []
    tok0 = 0
    for (B, S) in groups:
        rows = S // GRID_W
        nrb = rows // NA_ROW_BLOCK
        assert nrb >= 3 and S % NA_Q == 0
        for b in range(B):
            for rb in range(nrb):
                var = 0 if rb == 0 else (2 if rb == nrb - 1 else 1)
                kr0 = min(max(NA_ROW_BLOCK * rb - NA_WIN_ROWS // 2, 0), rows - NA_KEY_ROWS)
                steps.append((var, (tok0 + b * S) // NA_Q + rb, (tok0 + b * S + kr0 * GRID_W) // NA_KB))
        tok0 += B * S
    steps.sort(key=lambda t: t[0])
    arr = np.asarray(steps, np.int32)
    return arr[:, 0], arr[:, 1], arr[:, 2]


def _na_kernel(var_ref, qb_ref, kb_ref, q_ref, *rest):
    k_refs = rest[:NA_NKB]
    v_refs = rest[NA_NKB:2 * NA_NKB]
    bias_ref, qg_ref, kg_ref, o_ref = rest[2 * NA_NKB:]
    scale = NA_HEAD_DIM ** -0.5
    q = _rms(q_ref[...].astype(F32), qg_ref[...]).astype(BF16)
    s = []
    for m in range(NA_NKB):
        k = _rms(k_refs[m][...].astype(F32), kg_ref[...]).astype(BF16)
        s.append(_dot_nt(q, k) * scale + bias_ref[0, 0, :, m * NA_KB:(m + 1) * NA_KB])
    mx = s[0].max(axis=-1, keepdims=True)
    for m in range(1, NA_NKB):
        mx = jnp.maximum(mx, s[m].max(axis=-1, keepdims=True))
    l = jnp.zeros_like(mx)
    acc = jnp.zeros((NA_Q, NA_HEAD_DIM), F32)
    for m in range(NA_NKB):
        p = jnp.exp(s[m] - mx)
        l = l + p.sum(axis=-1, keepdims=True)
        acc = acc + jnp.dot(p.astype(BF16), v_refs[m][...].astype(BF16), preferred_element_type=F32)
    o_ref[...] = (acc / l).astype(o_ref.dtype)


def _neighbourhood_attention(p, groups, rpb, q_gain, k_gain, out_cols):
    T = p.shape[0]
    var, qblk, kblk = _na_steps(groups)
    tabs = _na_bias_tables(rpb)
    H = NA_HEADS

    def qmap(h, n, var, qb, kb):
        return (qb[n], h)

    def kmap(m, off):
        return lambda h, n, var, qb, kb: (kb[n] + m, off + h)

    in_specs = [pl.BlockSpec((NA_Q, NA_HEAD_DIM), qmap)]
    in_specs += [pl.BlockSpec((NA_KB, NA_HEAD_DIM), kmap(m, H)) for m in range(NA_NKB)]
    in_specs += [pl.BlockSpec((NA_KB, NA_HEAD_DIM), kmap(m, 2 * H)) for m in range(NA_NKB)]
    in_specs += [
        pl.BlockSpec((1, 1, NA_Q, NA_KEY_ROWS * GRID_W), lambda h, n, var, qb, kb: (var[n], h, 0, 0)),
        pl.BlockSpec((1, NA_HEAD_DIM), lambda h, n, var, qb, kb: (0, 0)),
        pl.BlockSpec((1, NA_HEAD_DIM), lambda h, n, var, qb, kb: (0, 0)),
    ]
    return pl.pallas_call(
        _na_kernel,
        grid_spec=pltpu.PrefetchScalarGridSpec(
            num_scalar_prefetch=3,
            grid=(H, len(var)),
            in_specs=in_specs,
            out_specs=pl.BlockSpec((NA_Q, NA_HEAD_DIM), qmap),
        ),
        out_shape=jax.ShapeDtypeStruct((T, out_cols), BF16),
        compiler_params=_cparams(("parallel", "arbitrary")),
        name="natten",
    )(jnp.asarray(var), jnp.asarray(qblk), jnp.asarray(kblk), p, *([p] * (2 * NA_NKB)), tabs,
      q_gain.reshape(1, -1).astype(F32), k_gain.reshape(1, -1).astype(F32))


HG_BLOCK = 1024
HG_NCHUNK = HG_BLOCK // HG_CHUNK
HG_SUB = 8
assert HG_CHUNK == 4 * HG_SUB


def _hg_steps(groups, reverse):
    blk, first = [], []
    tok0 = 0
    for (B, S) in groups:
        nb = S // HG_BLOCK
        assert S % HG_BLOCK == 0
        for b in range(B):
            order = range(nb - 1, -1, -1) if reverse else range(nb)
            for n, sb in enumerate(order):
                blk.append((tok0 + b * S) // HG_BLOCK + sb)
                first.append(1 if n == 0 else 0)
        tok0 += B * S
    return np.asarray(blk, np.int32), np.asarray(first, np.int32)


def _hg_matrices(reverse):
    C, S = HG_CHUNK, HG_SUB
    t = np.arange(C)[:, None]
    u = np.arange(C)[None, :]
    upto = (u >= t) if reverse else (u <= t)
    after = (u < t) if reverse else (u > t)
    same = lambda m: (u // m) == (t // m)
    sums = np.concatenate([upto, upto & same(S), upto & same(2 * S), after & same(S), after & same(2 * S)], axis=0)
    query_side = 0 if reverse else 1
    masks = []
    for m in (S, 2 * S):
        bt, bs = t // m, u // m
        masks.append((bt % 2 == query_side) & (bs == (bt + 1 if reverse else bt - 1)))
    return sums.astype(np.float32), np.stack(masks).astype(np.float32)


def _hg_kernel(blk_ref, first_ref, q_ref, f_ref, i_ref, lb_ref, sums_ref, mask_ref, *rest, reverse, finish):
    if finish:
        ofw_ref, g_ref, gain_ref, o_ref, st_ref = rest
    else:
        o_ref, st_ref = rest
    h = pl.program_id(0)
    n = pl.program_id(1)
    C = HG_CHUNK

    @pl.when(first_ref[n] == 1)
    def _():
        st_ref[...] = jnp.zeros_like(st_ref)

    lb = lb_ref[pl.ds(h, 1), :]
    sums = sums_ref[...]
    mask0, mask1 = mask_ref[0], mask_ref[1]
    S = HG_SUB
    NC = HG_NCHUNK
    TB = HG_BLOCK
    blk3 = (TB // S, S, HG_DIM)
    row8 = lax.broadcasted_iota(jnp.int32, blk3, 1)
    ones = jnp.ones((HG_DIM, HG_DIM), BF16)
    rows = lambda x, c: x[c * C:(c + 1) * C]

    f = lb + (1.0 - lb) * jax.nn.sigmoid(f_ref[...].astype(F32))
    logf = jnp.log(f)
    kk = 1.0 - f
    q = q_ref[...].astype(F32)
    q = q * jax.nn.sigmoid(q)
    v = i_ref[...].astype(F32)
    logf_hi = logf.astype(BF16)
    logf_lo = (logf - logf_hi.astype(F32)).astype(BF16)
    e_cat = (jnp.dot(sums, jnp.concatenate([rows(logf_hi, c) for c in range(NC)], axis=1), preferred_element_type=F32)
             + jnp.dot(sums, jnp.concatenate([rows(logf_lo, c) for c in range(NC)], axis=1),
                       preferred_element_type=F32))
    part = lambda r: jnp.concatenate(
        [e_cat[r * C:(r + 1) * C, c * HG_DIM:(c + 1) * HG_DIM] for c in range(NC)], axis=0)
    b, bloc, eq1, ek0, ek1 = [part(r) for r in range(5)]
    b_end = jnp.concatenate(
        [jnp.broadcast_to(rows(b, c)[0:1] if reverse else rows(b, c)[C - 1:C], (C, HG_DIM)) for c in range(NC)], axis=0)
    qb = (q * jnp.exp(b)).astype(BF16)
    q0, k0 = (q * jnp.exp(bloc)).astype(BF16), (kk * jnp.exp(ek0)).astype(BF16)
    q1, k1 = (q * jnp.exp(eq1)).astype(BF16), (kk * jnp.exp(ek1)).astype(BF16)
    ks = (kk * jnp.exp(b_end - b)).astype(BF16)
    vb = v.astype(BF16)
    o_off, upd = [], []
    for c in range(NC):
        a_off = (_dot_nt(rows(q0, c), rows(k0, c)) * mask0 + _dot_nt(rows(q1, c), rows(k1, c)) * mask1)
        o_off.append(jnp.dot(a_off.astype(BF16), rows(vb, c), preferred_element_type=F32))
        upd.append(_dot_tn(rows(vb, c), rows(ks, c)))
    q3, k3, v3, b3 = q.reshape(blk3), kk.reshape(blk3), v.reshape(blk3), bloc.reshape(blk3)
    prods = []
    for ds in range(S):
        keep = (row8 <= ds) if reverse else (row8 >= ds)
        e = jnp.exp(jnp.where(keep, b3 - b3[:, ds:ds + 1, :], -jnp.inf))
        prods.append((q3 * e * k3[:, ds:ds + 1, :]).reshape(TB, HG_DIM).astype(BF16))
    a_bc = jnp.dot(jnp.concatenate(prods, axis=0), ones, preferred_element_type=F32)
    o_diag = jnp.zeros(blk3, F32)
    for ds in range(S):
        o_diag = o_diag + a_bc[ds * TB:(ds + 1) * TB, :].reshape(blk3) * v3[:, ds:ds + 1, :]
    st = st_ref[...]
    o_inter = [None] * NC
    for cc in range(NC):
        c = NC - 1 - cc if reverse else cc
        o_inter[c] = _dot_nt(rows(qb, c), st.astype(BF16))
        st = st * jnp.exp(rows(b_end, c)[0:1]) + upd[c]
    st_ref[...] = st
    o_blk = jnp.concatenate([o_inter[c] + o_off[c] for c in range(NC)], axis=0) + o_diag.reshape(TB, HG_DIM)

    if finish:
        o = o_blk + ofw_ref[...]
        g = g_ref[...].astype(F32)
        o_ref[...] = (_rms(o, gain_ref[...]) * (g * jax.nn.sigmoid(g))).astype(o_ref.dtype)
    else:
        o_ref[...] = o_blk


def _hgrn2(p, groups, lb, *, reverse, f_col, o_fw=None, out_gain=None, mix=None):
    T = p.shape[0]
    blk, first = _hg_steps(groups, reverse)
    H = HG_HEADS
    finish = o_fw is not None
    C = HG_CHUNK
    sums, masks = _hg_matrices(reverse)

    def col(off):
        return lambda h, n, blk, first: (blk[n], off + h)

    const = lambda h, n, blk, first: (0, 0)
    in_specs = [
        pl.BlockSpec((HG_BLOCK, HG_DIM), col(3 * NA_HEADS)),
        pl.BlockSpec((HG_BLOCK, HG_DIM), col(f_col)),
        pl.BlockSpec((HG_BLOCK, HG_DIM), col(6 * NA_HEADS)),
        pl.BlockSpec((H, HG_DIM), const),
        pl.BlockSpec(sums.shape, const),
        pl.BlockSpec(masks.shape, lambda h, n, blk, first: (0, 0, 0)),
    ]
    args = [p, p, p, lb.reshape(H, HG_DIM).astype(F32), jnp.asarray(sums, BF16), jnp.asarray(masks)]
    aliases = {}
    if finish:
        in_specs += [
            pl.BlockSpec((HG_BLOCK, HG_DIM), col(0)),
            pl.BlockSpec((HG_BLOCK, HG_DIM), col(7 * NA_HEADS)),
            pl.BlockSpec((1, HG_DIM), const),
            pl.BlockSpec(memory_space=pl.ANY),
        ]
        args += [o_fw, p, out_gain.reshape(1, HG_DIM).astype(F32), mix]
        aliases = {2 + len(args) - 1: 0}
        out_spec = pl.BlockSpec((HG_BLOCK, HG_DIM), col(NA_HEADS))
        out_shape = jax.ShapeDtypeStruct(mix.shape, mix.dtype)
    else:
        out_spec = pl.BlockSpec((HG_BLOCK, HG_DIM), col(0))
        out_shape = jax.ShapeDtypeStruct((T, H * HG_DIM), F32)

    def body(blk_ref, first_ref, *refs):
        if finish:
            refs = refs[:9] + refs[10:]
        _hg_kernel(blk_ref, first_ref, *refs, reverse=reverse, finish=finish)

    return pl.pallas_call(
        body,
        grid_spec=pltpu.PrefetchScalarGridSpec(
            num_scalar_prefetch=2,
            grid=(H, len(blk)),
            in_specs=in_specs,
            out_specs=out_spec,
            scratch_shapes=[
                pltpu.VMEM((HG_DIM, HG_DIM), F32),
            ],
        ),
        out_shape=out_shape,
        input_output_aliases=aliases,
        compiler_params=_cparams(("parallel", "arbitrary")),
        name="hgrn2_bw" if reverse else "hgrn2_fw",
    )(jnp.asarray(blk), jnp.asarray(first), *args)


def _rope_tables(groups):
    half = MLA_ROPE // 2
    smax = max(S for _, S in groups)
    pos = jnp.arange(smax, dtype=F32)
    inv = 1.0 / (ROPE_BASE ** (jnp.arange(0, MLA_ROPE, 2, dtype=F32) / MLA_ROPE))
    ang = pos[:, None] * inv[None, :]
    cos, sin = jnp.cos(ang), jnp.sin(ang)
    z = jnp.zeros((smax, LANES - 2 * half), F32)
    zh = jnp.zeros((smax, half), F32)
    c = jnp.concatenate([cos, cos, z], axis=1)
    s_from_x2 = jnp.concatenate([-sin, zh, z], axis=1)
    s_from_x1 = jnp.concatenate([zh, sin, z], axis=1)
    return c, s_from_x2, s_from_x1


def _rope_block(t, c, s2, s1):
    half = MLA_ROPE // 2
    return t * c + pltpu.roll(t, LANES - half, 1) * s2 + pltpu.roll(t, half, 1) * s1


def _qprep_kernel(q_ref, g_ref, c_ref, s2_ref, s1_ref, o_ref, *, scale):
    c, s2, s1 = c_ref[...], s2_ref[...], s1_ref[...]
    for h in range(MLA_HEADS):
        a = q_ref[:, h * MLA_HEAD_PAD:h * MLA_HEAD_PAD + LANES].astype(F32)
        r = q_ref[:, h * MLA_HEAD_PAD + LANES:(h + 1) * MLA_HEAD_PAD].astype(F32)
        ms = (jnp.sum(a * a, axis=-1, keepdims=True) + jnp.sum(r * r, axis=-1, keepdims=True)) / MLA_QK
        inv = lax.rsqrt(ms + EPS)
        o_ref[:, h * MLA_HEAD_PAD:h * MLA_HEAD_PAD + LANES] = (a * inv * g_ref[:, :LANES] * scale).astype(o_ref.dtype)
        rn = r * inv * g_ref[:, LANES:]
        o_ref[:, h * MLA_HEAD_PAD + LANES:(h + 1) * MLA_HEAD_PAD] = (_rope_block(rn, c, s2, s1) * scale).astype(o_ref.dtype)


def _kprep_kernel(kv_ref, kr_ref, g_ref, c_ref, s2_ref, s1_ref, o_ref, vt_ref):
    c, s2, s1 = c_ref[...], s2_ref[...], s1_ref[...]
    r = kr_ref[...].astype(F32)
    rs = jnp.sum(r * r, axis=-1, keepdims=True)
    for h in range(MLA_HEADS):
        a = kv_ref[:, h * MLA_HEAD_PAD:h * MLA_HEAD_PAD + LANES].astype(F32)
        inv = lax.rsqrt((jnp.sum(a * a, axis=-1, keepdims=True) + rs) / MLA_QK + EPS)
        o_ref[:, h * MLA_HEAD_PAD:h * MLA_HEAD_PAD + LANES] = (a * inv * g_ref[:, :LANES]).astype(o_ref.dtype)
        rn = r * inv * g_ref[:, LANES:]
        o_ref[:, h * MLA_HEAD_PAD + LANES:(h + 1) * MLA_HEAD_PAD] = _rope_block(rn, c, s2, s1).astype(o_ref.dtype)
        v = kv_ref[:, h * MLA_HEAD_PAD + LANES:(h + 1) * MLA_HEAD_PAD].astype(F32)
        vt_ref[h * LANES:(h + 1) * LANES, :] = v.T.astype(vt_ref.dtype)


def _pos_block_map(groups, tm):
    bounds = []
    tok0 = 0
    for (B, S) in groups:
        assert S % tm == 0
        bounds.append((tok0 // tm, S // tm))
        tok0 += B * S

    def imap(i):
        out = (i - bounds[0][0]) % bounds[0][1]
        for (b0, per) in bounds[1:]:
            out = jnp.where(i >= b0, (i - b0) % per, out)
        return (out, 0)

    return imap


def _pad_gain(g):
    return jnp.concatenate([g.astype(F32), jnp.zeros((MLA_HEAD_PAD - MLA_QK,), F32)]).reshape(1, MLA_HEAD_PAD)


def _mla_prep(q, kv, cc, kr_block, groups, q_gain, k_gain, tm=256):
    T = q.shape[0]
    W = MLA_HEADS * MLA_HEAD_PAD
    tabs = _rope_tables(groups)
    pmap = _pos_block_map(groups, tm)
    tab_specs = [pl.BlockSpec((tm, LANES), pmap)] * 3
    row = pl.BlockSpec((tm, W), lambda i: (i, 0))
    gspec = pl.BlockSpec((1, MLA_HEAD_PAD), lambda i: (0, 0))
    qf = pl.pallas_call(
        functools.partial(_qprep_kernel, scale=MLA_QK ** -0.5 * LOG2_E),
        grid=(T // tm,),
        in_specs=[row, gspec] + tab_specs,
        out_specs=row,
        out_shape=jax.ShapeDtypeStruct((T, W), BF16),
        compiler_params=_cparams(("parallel",)),
        name="mla_qprep",
    )(q, _pad_gain(q_gain), *tabs)
    kf, vt = pl.pallas_call(
        _kprep_kernel,
        grid=(T // tm,),
        in_specs=[row, pl.BlockSpec((tm, LANES), lambda i: (i, kr_block)), gspec] + tab_specs,
        out_specs=[row, pl.BlockSpec((MLA_HEADS * LANES, tm), lambda i: (0, i))],
        out_shape=[jax.ShapeDtypeStruct((T, W), BF16), jax.ShapeDtypeStruct((MLA_HEADS * LANES, T), BF16)],
        compiler_params=_cparams(("parallel",)),
        name="mla_kprep",
    )(kv, cc, _pad_gain(k_gain), *tabs)
    return qf, kf, vt


FLASH_SUB = 512
LOG2_E = 1.4426950408889634


def _attn_kernel(q_ref, k_ref, vt_ref, *rest):
    o_ref = rest[-1]
    q = q_ref[...]
    nsub = k_ref.shape[0] // FLASH_SUB
    s = [_dot_nt(k_ref[c * FLASH_SUB:(c + 1) * FLASH_SUB, :], q) for c in range(nsub)]
    m = s[0].max(axis=0, keepdims=True)
    for c in range(1, nsub):
        m = jnp.maximum(m, s[c].max(axis=0, keepdims=True))
    l = jnp.zeros_like(m)
    acc = jnp.zeros((vt_ref.shape[0], q.shape[0]), F32)
    for c in range(nsub):
        p = jnp.exp2(s[c] - m)
        l = l + p.sum(axis=0, keepdims=True)
        acc = acc + jnp.dot(vt_ref[:, c * FLASH_SUB:(c + 1) * FLASH_SUB], p.astype(BF16), preferred_element_type=F32)
    o_ref[...] = (acc / l).T.astype(o_ref.dtype)


MLA_MAX_SEQ = 16384


def _mla_attention(qf, kf, vt, groups, tq=512):
    T = qf.shape[0]
    H = MLA_HEADS
    out = None
    tok0 = 0
    for (B, S) in groups:
        assert S % tq == 0 and S % FLASH_SUB == 0 and tok0 % S == 0 and S <= MLA_MAX_SEQ
        nq = S // tq
        q0, s0 = tok0 // tq, tok0 // S
        once = pl.Buffered(1)
        in_specs = [
            pl.BlockSpec((tq, MLA_HEAD_PAD), lambda b, h, qi, q0=q0, nq=nq: (q0 + b * nq + qi, h)),
            pl.BlockSpec((S, MLA_HEAD_PAD), lambda b, h, qi, s0=s0: (s0 + b, h), pipeline_mode=once),
            pl.BlockSpec((LANES, S), lambda b, h, qi, s0=s0: (h, s0 + b), pipeline_mode=once),
        ]
        args = [qf, kf, vt]
        aliases = {}
        if out is not None:
            in_specs.append(pl.BlockSpec(memory_space=pl.ANY))
            args.append(out)
            aliases = {3: 0}
        out = pl.pallas_call(
            _attn_kernel,
            grid=(B, H, nq),
            in_specs=in_specs,
            out_specs=pl.BlockSpec((tq, LANES), lambda b, h, qi, q0=q0, nq=nq: (q0 + b * nq + qi, h)),
            out_shape=jax.ShapeDtypeStruct((T, H * LANES), BF16),
            input_output_aliases=aliases,
            compiler_params=_cparams(("parallel", "parallel", "parallel")),
            name="mla_attn",
        )(*args)
        tok0 += B * S
    return out


def _peer_candidates():
    a, b = [], []
    a += [0] * 16; b += list(range(16))
    for aa in (1, 2, 3):
        a += [aa] * 8; b += list(range(8))
    for bb in (0, 1):
        a += [4, 5, 6, 7]; b += [bb] * 4
    a += list(range(8, 16)); b += [0] * 8
    for bb in (2, 3):
        a += [4, 5, 6, 7]; b += [bb] * 4
    return np.asarray(a), np.asarray(b)


_CAND_A, _CAND_B = _peer_candidates()
N_CAND = len(_CAND_A)


def _extract16(val, order, exact):
    step = jnp.full(val.shape, float(PEER_TOPK), F32)
    tops = []
    for r in range(PEER_TOPK):
        m = val.max(axis=0, keepdims=True)
        if exact:
            first = jnp.where(val == m, order, 1e9).min(axis=0, keepdims=True)
            sel = order == first
        else:
            sel = val == m
        step = jnp.where(sel, float(r), step)
        val = jnp.where(sel, -jnp.inf, val)
        tops.append(m)
    return jnp.concatenate(tops, axis=0), step


def _route_head(s1, s2, pos, amat, exact):
    tm = s1.shape[1]
    idx = lax.broadcasted_iota(jnp.int32, s1.shape, 0).astype(F32)
    t1, r1 = _extract16(s1, idx, exact)
    t2, r2 = _extract16(s2, idx, exact)
    mid1 = jnp.concatenate([t1[4:8], t1[4:8]], axis=0)
    rep2 = lambda b: jnp.broadcast_to(t2[b:b + 1], (4, tm))
    cand = jnp.concatenate([
        t1[0:1] + t2, t1[1:2] + t2[0:8], t1[2:3] + t2[0:8], t1[3:4] + t2[0:8],
        mid1 + jnp.concatenate([rep2(0), rep2(1)], axis=0),
        t1[8:16] + t2[0:1],
        mid1 + jnp.concatenate([rep2(2), rep2(3)], axis=0)], axis=0)
    best = cand[0:1]
    _, cstep = _extract16(cand, pos, exact)
    chosen = (cstep < float(PEER_TOPK)).astype(F32)
    z = jnp.sum(chosen * jnp.exp(cand - best), axis=0, keepdims=True)
    per_a = jnp.dot(amat, chosen.astype(BF16), preferred_element_type=F32)
    cnt = jnp.zeros(s1.shape, F32)
    for a in range(PEER_TOPK):
        cnt = jnp.where(r1 == float(a), per_a[a:a + 1], cnt)
    e1n = jnp.exp(s1 - t1[0:1]) / z
    e2 = jnp.exp(s2 - t2[0:1])
    removed = lambda st: jnp.sum((st < float(PEER_TOPK)).astype(F32), axis=0, keepdims=True)
    ties = (jnp.abs(removed(r1) - PEER_TOPK) + jnp.abs(removed(r2) - PEER_TOPK)
            + jnp.abs(jnp.sum(chosen, axis=0, keepdims=True) - PEER_TOPK))
    return cnt, e1n, r2, e2, ties


def _peer_route_kernel(x_ref, g_ref, wq_ref, keys_ref, pos_ref, amat_ref,
                       hn_ref, cnt_ref, e1_ref, rb_ref, e2_ref, sc_ref):
    tm = x_ref.shape[0]
    hn = _rms(x_ref[...].astype(F32), g_ref[...]).astype(BF16)
    hn_ref[...] = hn
    qry = jnp.dot(hn, wq_ref[...], preferred_element_type=F32)
    for hp in range(2 * PEER_HEADS):
        sc_ref[hp] = _dot_nt(keys_ref[hp // 2, hp % 2], qry[:, hp * PEER_HALF:(hp + 1) * PEER_HALF].astype(BF16))

    def route(exact):
        ties = jnp.zeros((1, tm), F32)
        for h in range(PEER_HEADS):
            cnt, e1n, r2, e2, t = _route_head(sc_ref[2 * h], sc_ref[2 * h + 1], pos_ref[...], amat_ref[...], exact)
            cnt_ref[h] = cnt
            e1_ref[h] = e1n
            rb_ref[h] = pltpu.bitcast(r2.astype(BF16), jnp.uint32)
            e2_ref[h] = pltpu.bitcast(e2.astype(BF16), jnp.uint32)
            ties = ties + t
        return ties

    ties = route(exact=False)

    @pl.when(jnp.max(ties) > 0.0)
    def _():
        route(exact=True)


def _peer_route(x, norm_gain, w_query, sub_keys, tm=256):
    T, D = x.shape
    H = PEER_HEADS
    pos = np.broadcast_to((_CAND_A * PEER_TOPK + _CAND_B).astype(np.float32)[:, None], (N_CAND, tm))
    amat = (np.arange(PEER_TOPK)[:, None] == _CAND_A[None, :]).astype(np.float32)
    gate_shape = jax.ShapeDtypeStruct((H, PEER_N_KEYS, T), F32)
    packed_shape = jax.ShapeDtypeStruct((H, PEER_N_KEYS // 2, T), jnp.uint32)
    gate_spec = pl.BlockSpec((H, PEER_N_KEYS, tm), lambda i: (0, 0, i))
    packed_spec = pl.BlockSpec((H, PEER_N_KEYS // 2, tm), lambda i: (0, 0, i))
    return pl.pallas_call(
        _peer_route_kernel,
        grid=(T // tm,),
        in_specs=[
            pl.BlockSpec((tm, D), lambda i: (i, 0)),
            pl.BlockSpec((1, D), lambda i: (0, 0)),
            pl.BlockSpec(w_query.shape, lambda i: (0, 0)),
            pl.BlockSpec(sub_keys.shape, lambda i: (0, 0, 0, 0)),
            pl.BlockSpec((N_CAND, tm), lambda i: (0, 0)),
            pl.BlockSpec((PEER_TOPK, N_CAND), lambda i: (0, 0)),
        ],
        out_specs=[pl.BlockSpec((tm, D), lambda i: (i, 0)), gate_spec, gate_spec, packed_spec, packed_spec],
        out_shape=[jax.ShapeDtypeStruct((T, D), BF16), gate_shape, gate_shape, packed_shape, packed_shape],
        scratch_shapes=[pltpu.VMEM((2 * H, PEER_N_KEYS, tm), F32)],
        compiler_params=_cparams(("parallel",)),
        name="peer_route",
    )(x, norm_gain.reshape(1, D).astype(F32), w_query, sub_keys, jnp.asarray(pos), jnp.asarray(amat, BF16))


def _gelu(a):
    return 0.5 * a * (1.0 + lax.erf(a * (2.0 ** -0.5)))


def _gate_weights(cnt_ref, e1_ref, rb_ref, e2_ref, w_ref, step, slot, rows_per_step):
    tile = (PEER_N_KEYS, LANES)
    half = PEER_N_KEYS // 2
    for il in range(rows_per_step):
        i = step * rows_per_step + il
        cnt = [cnt_ref[pl.ds(h * PEER_N_KEYS + i, 1), :] for h in range(PEER_HEADS)]
        e1 = [e1_ref[pl.ds(h * PEER_N_KEYS + i, 1), :] for h in range(PEER_HEADS)]
        for st in range(w_ref.shape[2] // LANES):
            ls = slice(st * LANES, (st + 1) * LANES)
            w = jnp.zeros(tile, BF16)
            for h in range(PEER_HEADS):
                hs = slice(h * half, (h + 1) * half)
                rb = pltpu.bitcast(rb_ref[hs, ls], BF16)
                e2 = pltpu.bitcast(e2_ref[hs, ls], BF16)
                cnt_t = jnp.broadcast_to(cnt[h][:, ls], tile).astype(BF16)
                e1_t = jnp.broadcast_to(e1[h][:, ls], tile).astype(BF16)
                w = w + jnp.where(rb < cnt_t, e2, 0.0) * e1_t
            w_ref[slot, il * half:(il + 1) * half, ls] = pltpu.bitcast(w, jnp.uint32)


def _peer_expert_kernel(hn_ref, u_ref, vt_ref, cnt_ref, e1_ref, rb_ref, e2_ref, x_ref, o_ref, acc_ref, g_ref, w_ref,
                        *, rows_per_step):
    e = pl.program_id(1)
    gate = functools.partial(_gate_weights, cnt_ref, e1_ref, rb_ref, e2_ref, w_ref, rows_per_step=rows_per_step)

    @pl.when(e == 0)
    def _():
        acc_ref[...] = jnp.zeros_like(acc_ref)
        gate(step=0, slot=0)

    gate(step=jnp.minimum(e + 1, pl.num_programs(1) - 1), slot=(e + 1) % 2)
    act = _gelu(_dot_nt(u_ref[...], hn_ref[...]))
    g_ref[...] = (pltpu.bitcast(w_ref[e % 2], BF16).astype(F32) * act).astype(BF16)
    acc_ref[...] += jnp.dot(vt_ref[...], g_ref[...], preferred_element_type=F32)

    @pl.when(e == pl.num_programs(1) - 1)
    def _():
        o_ref[...] = x_ref[...] + acc_ref[...].T


def _peer_experts(x, hn, gates, u, vt, tm=512, te=1024):
    T, D = x.shape
    NE = u.shape[0]
    gates = [g.reshape(-1, T) for g in gates]
    once = pl.Buffered(1)
    gate_spec = pl.BlockSpec((PEER_HEADS * PEER_N_KEYS, tm), lambda i, e: (0, i), pipeline_mode=once)
    packed_spec = pl.BlockSpec((PEER_HEADS * PEER_N_KEYS // 2, tm), lambda i, e: (0, i), pipeline_mode=once)
    return pl.pallas_call(
        functools.partial(_peer_expert_kernel, rows_per_step=te // PEER_N_KEYS),
        grid=(T // tm, NE // te),
        in_specs=[
            pl.BlockSpec((tm, D), lambda i, e: (i, 0), pipeline_mode=once),
            pl.BlockSpec((te, D), lambda i, e: (e, 0)),
            pl.BlockSpec((D, te), lambda i, e: (0, e)),
            gate_spec, gate_spec, packed_spec, packed_spec,
            pl.BlockSpec((tm, D), lambda i, e: (i, 0), pipeline_mode=once),
        ],
        out_specs=pl.BlockSpec((tm, D), lambda i, e: (i, 0)),
        out_shape=jax.ShapeDtypeStruct((T, D), F32),
        scratch_shapes=[pltpu.VMEM((D, tm), F32), pltpu.VMEM((te, tm), BF16),
                        pltpu.VMEM((2, te // 2, tm), jnp.uint32)],
        compiler_params=_cparams(("parallel", "arbitrary")),
        name="peer_experts",
    )(hn, u, vt, *gates, x)


def _peer(x, norm_gain, w_query, sub_keys, u, v):
    hn, cnt, e1, rb, e2 = _peer_route(x, norm_gain, w_query.astype(BF16), sub_keys.astype(BF16))
    return _peer_experts(x, hn, (cnt, e1, rb, e2), u.astype(BF16), v.astype(BF16).T)


def _even_layer(x, groups, norm, w_in, na_q_gain, na_k_gain, na_rpb, lb, hg_out_gain, w_out):
    p = _matmul(x, w_in.astype(BF16), gain=norm, out_dtype=F32)
    mix_cols = NA_HEADS * NA_HEAD_DIM + HG_HEADS * HG_DIM
    mix = _neighbourhood_attention(p, groups, na_rpb, na_q_gain, na_k_gain, mix_cols)
    o_fw = _hgrn2(p, groups, lb[0], reverse=False, f_col=4 * NA_HEADS)
    mix = _hgrn2(p, groups, lb[1], reverse=True, f_col=5 * NA_HEADS, o_fw=o_fw, out_gain=hg_out_gain, mix=mix)
    return _matmul(mix, w_out.astype(BF16), residual=x, out_dtype=F32)


def _odd_layer(x, groups, norm, w_in, q_norm, kv_norm, w_uq, w_ukv, q_gain, k_gain, w_out):
    D = x.shape[1]
    pad_q = (-MLA_Q_RANK) % MLA_KV_RANK
    w_cq, w_ckv, w_kr = jnp.split(w_in, [MLA_Q_RANK, MLA_Q_RANK + MLA_KV_RANK], axis=1)
    w_in_p = jnp.concatenate(
        [w_cq, jnp.zeros((D, pad_q), w_in.dtype), w_ckv, w_kr, jnp.zeros((D, LANES - MLA_ROPE), w_in.dtype)],
        axis=1).astype(BF16)
    cc = _matmul(x, w_in_p, gain=norm, out_dtype=F32, tm=512, tn=w_in_p.shape[1])
    w_uq_p = jnp.pad(w_uq.reshape(MLA_Q_RANK, MLA_HEADS, MLA_QK),
                     ((0, 0), (0, 0), (0, MLA_HEAD_PAD - MLA_QK))).reshape(MLA_Q_RANK, -1).astype(BF16)
    q = _matmul(cc, w_uq_p, gain=q_norm, out_dtype=BF16, x_col_block=0)
    kv = _matmul(cc, w_ukv.astype(BF16), gain=kv_norm, out_dtype=BF16,
                 x_col_block=(MLA_Q_RANK + pad_q) // MLA_KV_RANK)
    kr_block = (MLA_Q_RANK + pad_q + MLA_KV_RANK) // LANES
    qf, kf, vt = _mla_prep(q, kv, cc, kr_block, groups, q_gain, k_gain)
    o = _mla_attention(qf, kf, vt, groups)
    return _matmul(o, w_out.astype(BF16), residual=x, out_dtype=F32)


def _forward(xs, ev_norm, ev_w_in, ev_na_q_gain, ev_na_k_gain, ev_na_rpb, ev_hg_lb_logits, ev_hg_out_gain,
             ev_w_out, od_norm, od_w_in, od_q_norm, od_kv_norm, od_w_uq, od_w_ukv, od_q_gain, od_k_gain,
             od_w_out, ff_norm, peer_w_query, peer_sub_keys, peer_u, peer_v):
    groups = [(int(x.shape[0]), int(x.shape[1])) for x in xs]
    D = xs[0].shape[-1]
    depth = ff_norm.shape[0]
    lb_all = jnp.cumsum(jax.nn.softmax(ev_hg_lb_logits.astype(F32), axis=1), axis=1)
    x = jnp.concatenate([t.reshape(-1, D) for t in xs], axis=0)
    for layer in range(depth):
        j = layer // 2
        if layer % 2 == 0:
            x = _even_layer(x, groups, ev_norm[j], ev_w_in[j], ev_na_q_gain[j], ev_na_k_gain[j], ev_na_rpb[j],
                            lb_all[:, j], ev_hg_out_gain[j], ev_w_out[j])
        else:
            x = _odd_layer(x, groups, od_norm[j], od_w_in[j], od_q_norm[j], od_kv_norm[j], od_w_uq[j],
                           od_w_ukv[j], od_q_gain[j], od_k_gain[j], od_w_out[j])
        x = _peer(x, ff_norm[layer], peer_w_query[layer], peer_sub_keys[layer], peer_u[layer], peer_v[layer])
    outs = []
    tok0 = 0
    for (B, S) in groups:
        outs.append(x[tok0:tok0 + B * S].reshape(B, S, D))
        tok0 += B * S
    return tuple(outs)


def kernel(x_prompt, x_sample, ev_norm, ev_w_in, ev_na_q_gain, ev_na_k_gain, ev_na_rpb, ev_hg_lb_logits, ev_hg_out_gain, ev_w_out, od_norm, od_w_in, od_q_norm, od_kv_norm, od_w_uq, od_w_ukv, od_q_gain, od_k_gain, od_w_out, ff_norm, peer_w_query, peer_sub_keys, peer_u, peer_v):
    return _forward((x_prompt, x_sample), ev_norm, ev_w_in, ev_na_q_gain, ev_na_k_gain, ev_na_rpb, ev_hg_lb_logits,
                    ev_hg_out_gain, ev_w_out, od_norm, od_w_in, od_q_norm, od_kv_norm, od_w_uq, od_w_ukv, od_q_gain,
                    od_k_gain, od_w_out, ff_norm, peer_w_query, peer_sub_keys, peer_u, peer_v)
```

```python
import functools

import numpy as np
import jax
import jax.numpy as jnp
from jax import lax
from jax.experimental import pallas as pl
from jax.experimental.pallas import tpu as pltpu

F32 = jnp.float32
BF16 = jnp.bfloat16

EPS = 1e-6
LANES = 128
VMEM_LIMIT_BYTES = 56 * 1024 * 1024
NEG_BIG = -1e30

GRID_W = 64
NA_HEADS = 8
NA_HEAD_DIM = 128
NA_WIN_ROWS = 8
NA_WIN_COLS = 16
NA_ROW_BLOCK = 8
NA_KEY_ROWS = 16
HG_HEADS = 8
HG_DIM = 128
HG_CHUNK = 32
MLA_HEADS = 16
MLA_Q_RANK = 768
MLA_KV_RANK = 512
MLA_NOPE = 128
MLA_ROPE = 64
MLA_QK = MLA_NOPE + MLA_ROPE
MLA_HEAD_PAD = 256
ROPE_BASE = 10000.0
PEER_HEADS = 8
PEER_N_KEYS = 128
PEER_TOPK = 16
PEER_HALF = 128


def _cparams(sem):
    return pltpu.CompilerParams(dimension_semantics=sem, vmem_limit_bytes=VMEM_LIMIT_BYTES)


def _rms(x, gain):
    return x * lax.rsqrt(jnp.mean(x * x, axis=-1, keepdims=True) + EPS) * gain


def _dot_nt(a, b):
    return lax.dot_general(a, b, (((1,), (1,)), ((), ())), preferred_element_type=F32)


def _dot_tn(a, b):
    return lax.dot_general(a, b, (((0,), (0,)), ((), ())), preferred_element_type=F32)


def _mm_kernel(*refs, has_norm, has_pair, has_res):
    it = iter(refs)
    x_ref = next(it)
    g_ref = next(it) if has_norm else None
    w_ref = next(it)
    x2_ref, w2_ref = (next(it), next(it)) if has_pair else (None, None)
    r_ref = next(it) if has_res else None
    o_ref = next(it)
    if has_norm:
        hn_ref = next(it)

        @pl.when(pl.program_id(1) == 0)
        def _():
            hn_ref[...] = _rms(x_ref[...].astype(F32), g_ref[...]).astype(BF16)

        a = hn_ref[...]
    else:
        a = x_ref[...].astype(BF16)
    acc = jnp.dot(a, w_ref[...], preferred_element_type=F32)
    if has_pair:
        acc = acc + jnp.dot(x2_ref[...].astype(BF16), w2_ref[...], preferred_element_type=F32)
    if has_res:
        acc = acc + r_ref[...]
    o_ref[...] = acc.astype(o_ref.dtype)


def _matmul(x, w, *, gain=None, pair=None, residual=None, out_dtype=F32, x_col_block=0, tm=1024, tn=1024):
    M = x.shape[0]
    K, N = w.shape
    tn = min(tn, N)
    assert M % tm == 0 and N % tn == 0 and (x_col_block + 1) * K <= x.shape[1]
    has_norm, has_pair, has_res = gain is not None, pair is not None, residual is not None
    in_specs = [pl.BlockSpec((tm, K), lambda i, j: (i, x_col_block))]
    args = [x]
    if has_norm:
        in_specs.append(pl.BlockSpec((1, K), lambda i, j: (0, 0)))
        args.append(gain.reshape(1, K).astype(F32))
    in_specs.append(pl.BlockSpec((K, tn), lambda i, j: (0, j)))
    args.append(w)
    if has_pair:
        x2, w2 = pair
        assert x2.shape == (M, w2.shape[0]) and w2.shape[1] == N
        in_specs += [pl.BlockSpec((tm, w2.shape[0]), lambda i, j: (i, 0)), pl.BlockSpec((w2.shape[0], tn), lambda i, j: (0, j))]
        args += [x2, w2]
    if has_res:
        in_specs.append(pl.BlockSpec((tm, tn), lambda i, j: (i, j)))
        args.append(residual)
    return pl.pallas_call(
        functools.partial(_mm_kernel, has_norm=has_norm, has_pair=has_pair, has_res=has_res),
        grid=(M // tm, N // tn),
        in_specs=in_specs,
        out_specs=pl.BlockSpec((tm, tn), lambda i, j: (i, j)),
        out_shape=jax.ShapeDtypeStruct((M, N), out_dtype),
        scratch_shapes=[pltpu.VMEM((tm, K), BF16)] if has_norm else [],
        compiler_params=_cparams(("parallel", "arbitrary")),
        name="mm",
    )(*args)


NA_Q = NA_ROW_BLOCK * GRID_W
NA_KB = 256
NA_NKB = NA_KEY_ROWS * GRID_W // NA_KB


def _na_bias_tables(rpb):
    i = np.arange(NA_ROW_BLOCK)[:, None, None, None]
    c = np.arange(GRID_W)[None, :, None, None]
    j = np.arange(NA_KEY_ROWS)[None, None, :, None]
    kc = np.arange(GRID_W)[None, None, None, :]
    half = NA_WIN_ROWS // 2
    qc0 = np.clip(c - NA_WIN_COLS // 2, 0, GRID_W - NA_WIN_COLS)
    col_ok = (kc >= qc0) & (kc < qc0 + NA_WIN_COLS)
    dcol = np.clip(kc - c + NA_WIN_COLS - 1, 0, 2 * NA_WIN_COLS - 2)
    col_hot = (dcol[0, :, 0, :, None] == np.arange(2 * NA_WIN_COLS - 1)).astype(np.float32)
    by_col = jnp.einsum('hab,ckb->hack', rpb.astype(F32), col_hot, precision=lax.Precision.HIGHEST)
    tabs = []
    for var in range(3):
        if var == 0:
            r0 = np.maximum(i - half, 0)
            dr = j - i
        elif var == 1:
            r0 = i
            dr = j - half - i
        else:
            r0 = np.minimum(i + half, NA_KEY_ROWS - NA_WIN_ROWS)
            dr = j - (NA_KEY_ROWS - NA_ROW_BLOCK) - i
        ok = (j >= r0) & (j < r0 + NA_WIN_ROWS) & col_ok
        dri = np.clip(dr + NA_WIN_ROWS - 1, 0, 2 * NA_WIN_ROWS - 2)[:, 0, :, 0]
        row_hot = (dri[:, :, None] == np.arange(2 * NA_WIN_ROWS - 1)).astype(np.float32)
        vals = jnp.einsum('ija,hack->hicjk', row_hot, by_col, precision=lax.Precision.HIGHEST)
        tabs.append(jnp.where(jnp.asarray(ok)[None], vals, NEG_BIG).reshape(
            NA_HEADS, NA_Q, NA_KEY_ROWS * GRID_W))
    return jnp.stack(tabs)


def _na_steps(groups):
    steps = []
    tok0 = 0
    for (B, S) in groups:
        rows = S // GRID_W
        nrb = rows // NA_ROW_BLOCK
        assert nrb >= 3 and S % NA_Q == 0
        for b in range(B):
            for rb in range(nrb):
                var = 0 if rb == 0 else (2 if rb == nrb - 1 else 1)
                kr0 = min(max(NA_ROW_BLOCK * rb - NA_WIN_ROWS // 2, 0), rows - NA_KEY_ROWS)
                steps.append((var, (tok0 + b * S) // NA_Q + rb, (tok0 + b * S + kr0 * GRID_W) // NA_KB))
        tok0 += B * S
    steps.sort(key=lambda t: t[0])
    arr = np.asarray(steps, np.int32)
    return arr[:, 0], arr[:, 1], arr[:, 2]


def _na_kernel(var_ref, qb_ref, kb_ref, q_ref, *rest):
    k_refs = rest[:NA_NKB]
    v_refs = rest[NA_NKB:2 * NA_NKB]
    bias_ref, qg_ref, kg_ref, o_ref = rest[2 * NA_NKB:]
    scale = NA_HEAD_DIM ** -0.5
    q = _rms(q_ref[...].astype(F32), qg_ref[...]).astype(BF16)
    s = []
    for m in range(NA_NKB):
        k = _rms(k_refs[m][...].astype(F32), kg_ref[...]).astype(BF16)
        s.append(_dot_nt(q, k) * scale + bias_ref[0, 0, :, m * NA_KB:(m + 1) * NA_KB])
    mx = s[0].max(axis=-1, keepdims=True)
    for m in range(1, NA_NKB):
        mx = jnp.maximum(mx, s[m].max(axis=-1, keepdims=True))
    l = jnp.zeros_like(mx)
    acc = jnp.zeros((NA_Q, NA_HEAD_DIM), F32)
    for m in range(NA_NKB):
        p = jnp.exp(s[m] - mx)
        l = l + p.sum(axis=-1, keepdims=True)
        acc = acc + jnp.dot(p.astype(BF16), v_refs[m][...].astype(BF16), preferred_element_type=F32)
    o_ref[...] = (acc / l).astype(o_ref.dtype)


def _neighbourhood_attention(p, groups, rpb, q_gain, k_gain):
    T = p.shape[0]
    out_cols = NA_HEADS * NA_HEAD_DIM
    var, qblk, kblk = _na_steps(groups)
    tabs = _na_bias_tables(rpb)
    H = NA_HEADS

    def qmap(h, n, var, qb, kb):
        return (qb[n], h)

    def kmap(m, off):
        return lambda h, n, var, qb, kb: (kb[n] + m, off + h)

    in_specs = [pl.BlockSpec((NA_Q, NA_HEAD_DIM), qmap)]
    in_specs += [pl.BlockSpec((NA_KB, NA_HEAD_DIM), kmap(m, H)) for m in range(NA_NKB)]
    in_specs += [pl.BlockSpec((NA_KB, NA_HEAD_DIM), kmap(m, 2 * H)) for m in range(NA_NKB)]
    in_specs += [
        pl.BlockSpec((1, 1, NA_Q, NA_KEY_ROWS * GRID_W), lambda h, n, var, qb, kb: (var[n], h, 0, 0)),
        pl.BlockSpec((1, NA_HEAD_DIM), lambda h, n, var, qb, kb: (0, 0)),
        pl.BlockSpec((1, NA_HEAD_DIM), lambda h, n, var, qb, kb: (0, 0)),
    ]
    return pl.pallas_call(
        _na_kernel,
        grid_spec=pltpu.PrefetchScalarGridSpec(
            num_scalar_prefetch=3,
            grid=(H, len(var)),
            in_specs=in_specs,
            out_specs=pl.BlockSpec((NA_Q, NA_HEAD_DIM), qmap),
        ),
        out_shape=jax.ShapeDtypeStruct((T, out_cols), BF16),
        compiler_params=_cparams(("parallel", "arbitrary")),
        name="natten",
    )(jnp.asarray(var), jnp.asarray(qblk), jnp.asarray(kblk), p, *([p] * (2 * NA_NKB)), tabs,
      q_gain.reshape(1, -1).astype(F32), k_gain.reshape(1, -1).astype(F32))


HG_BLOCK = 1024
HG_NCHUNK = HG_BLOCK // HG_CHUNK
HG_SUB = 8
assert HG_CHUNK == 4 * HG_SUB


def _hg_steps(groups, reverse):
    blk, first = [], []
    tok0 = 0
    for (B, S) in groups:
        nb = S // HG_BLOCK
        assert S % HG_BLOCK == 0
        for b in range(B):
            order = range(nb - 1, -1, -1) if reverse else range(nb)
            for n, sb in enumerate(order):
                blk.append((tok0 + b * S) // HG_BLOCK + sb)
                first.append(1 if n == 0 else 0)
        tok0 += B * S
    return np.asarray(blk, np.int32), np.asarray(first, np.int32)


def _hg_matrices(reverse):
    C, S = HG_CHUNK, HG_SUB
    t = np.arange(C)[:, None]
    u = np.arange(C)[None, :]
    upto = (u >= t) if reverse else (u <= t)
    after = (u < t) if reverse else (u > t)
    same = lambda m: (u // m) == (t // m)
    sums = np.concatenate([upto, upto & same(S), upto & same(2 * S), after & same(S), after & same(2 * S)], axis=0)
    query_side = 0 if reverse else 1
    masks = []
    for m in (S, 2 * S):
        bt, bs = t // m, u // m
        masks.append((bt % 2 == query_side) & (bs == (bt + 1 if reverse else bt - 1)))
    return sums.astype(np.float32), np.stack(masks).astype(np.float32)


def _hg_kernel(blk_ref, first_ref, q_ref, f_ref, i_ref, lb_ref, sums_ref, mask_ref, *rest, reverse, finish):
    if finish:
        ofw_ref, g_ref, gain_ref, o_ref, st_ref = rest
    else:
        o_ref, st_ref = rest
    h = pl.program_id(0)
    n = pl.program_id(1)
    C = HG_CHUNK

    @pl.when(first_ref[n] == 1)
    def _():
        st_ref[...] = jnp.zeros_like(st_ref)

    lb = lb_ref[pl.ds(h, 1), :]
    sums = sums_ref[...]
    mask0, mask1 = mask_ref[0], mask_ref[1]
    S = HG_SUB
    NC = HG_NCHUNK
    TB = HG_BLOCK
    blk3 = (TB // S, S, HG_DIM)
    row8 = lax.broadcasted_iota(jnp.int32, blk3, 1)
    ones = jnp.ones((HG_DIM, HG_DIM), BF16)
    rows = lambda x, c: x[c * C:(c + 1) * C]

    f = lb + (1.0 - lb) * jax.nn.sigmoid(f_ref[...].astype(F32))
    logf = jnp.log(f)
    kk = 1.0 - f
    q = q_ref[...].astype(F32)
    q = q * jax.nn.sigmoid(q)
    v = i_ref[...].astype(F32)
    logf_hi = logf.astype(BF16)
    logf_lo = (logf - logf_hi.astype(F32)).astype(BF16)
    e_cat = (jnp.dot(sums, jnp.concatenate([rows(logf_hi, c) for c in range(NC)], axis=1), preferred_element_type=F32)
             + jnp.dot(sums, jnp.concatenate([rows(logf_lo, c) for c in range(NC)], axis=1),
                       preferred_element_type=F32))
    part = lambda r: jnp.concatenate(
        [e_cat[r * C:(r + 1) * C, c * HG_DIM:(c + 1) * HG_DIM] for c in range(NC)], axis=0)
    b, bloc, eq1, ek0, ek1 = [part(r) for r in range(5)]
    b_end = jnp.concatenate(
        [jnp.broadcast_to(rows(b, c)[0:1] if reverse else rows(b, c)[C - 1:C], (C, HG_DIM)) for c in range(NC)], axis=0)
    qb = (q * jnp.exp(b)).astype(BF16)
    q0, k0 = (q * jnp.exp(bloc)).astype(BF16), (kk * jnp.exp(ek0)).astype(BF16)
    q1, k1 = (q * jnp.exp(eq1)).astype(BF16), (kk * jnp.exp(ek1)).astype(BF16)
    ks = (kk * jnp.exp(b_end - b)).astype(BF16)
    vb = v.astype(BF16)
    o_off, upd = [], []
    for c in range(NC):
        a_off = (_dot_nt(rows(q0, c), rows(k0, c)) * mask0 + _dot_nt(rows(q1, c), rows(k1, c)) * mask1)
        o_off.append(jnp.dot(a_off.astype(BF16), rows(vb, c), preferred_element_type=F32))
        upd.append(_dot_tn(rows(vb, c), rows(ks, c)))
    q3, k3, v3, b3 = q.reshape(blk3), kk.reshape(blk3), v.reshape(blk3), bloc.reshape(blk3)
    prods = []
    for ds in range(S):
        keep = (row8 <= ds) if reverse else (row8 >= ds)
        e = jnp.exp(jnp.where(keep, b3 - b3[:, ds:ds + 1, :], -jnp.inf))
        prods.append((q3 * e * k3[:, ds:ds + 1, :]).reshape(TB, HG_DIM).astype(BF16))
    a_bc = jnp.dot(jnp.concatenate(prods, axis=0), ones, preferred_element_type=F32)
    o_diag = jnp.zeros(blk3, F32)
    for ds in range(S):
        o_diag = o_diag + a_bc[ds * TB:(ds + 1) * TB, :].reshape(blk3) * v3[:, ds:ds + 1, :]
    st = st_ref[...]
    o_inter = [None] * NC
    for cc in range(NC):
        c = NC - 1 - cc if reverse else cc
        o_inter[c] = _dot_nt(rows(qb, c), st.astype(BF16))
        st = st * jnp.exp(rows(b_end, c)[0:1]) + upd[c]
    st_ref[...] = st
    o_blk = jnp.concatenate([o_inter[c] + o_off[c] for c in range(NC)], axis=0) + o_diag.reshape(TB, HG_DIM)

    if finish:
        o = o_blk + ofw_ref[...]
        g = g_ref[...].astype(F32)
        o_ref[...] = (_rms(o, gain_ref[...]) * (g * jax.nn.sigmoid(g))).astype(o_ref.dtype)
    else:
        o_ref[...] = o_blk


def _hgrn2(p, groups, lb, *, reverse, f_col, o_fw=None, out_gain=None):
    T = p.shape[0]
    blk, first = _hg_steps(groups, reverse)
    H = HG_HEADS
    finish = o_fw is not None
    C = HG_CHUNK
    sums, masks = _hg_matrices(reverse)

    def col(off):
        return lambda h, n, blk, first: (blk[n], off + h)

    const = lambda h, n, blk, first: (0, 0)
    in_specs = [
        pl.BlockSpec((HG_BLOCK, HG_DIM), col(3 * NA_HEADS)),
        pl.BlockSpec((HG_BLOCK, HG_DIM), col(f_col)),
        pl.BlockSpec((HG_BLOCK, HG_DIM), col(6 * NA_HEADS)),
        pl.BlockSpec((H, HG_DIM), const),
        pl.BlockSpec(sums.shape, const),
        pl.BlockSpec(masks.shape, lambda h, n, blk, first: (0, 0, 0)),
    ]
    args = [p, p, p, lb.reshape(H, HG_DIM).astype(F32), jnp.asarray(sums, BF16), jnp.asarray(masks)]
    if finish:
        in_specs += [
            pl.BlockSpec((HG_BLOCK, HG_DIM), col(0)),
            pl.BlockSpec((HG_BLOCK, HG_DIM), col(7 * NA_HEADS)),
            pl.BlockSpec((1, HG_DIM), const),
        ]
        args += [o_fw, p, out_gain.reshape(1, HG_DIM).astype(F32)]
    out_spec = pl.BlockSpec((HG_BLOCK, HG_DIM), col(0))
    out_shape = jax.ShapeDtypeStruct((T, H * HG_DIM), BF16 if finish else F32)

    return pl.pallas_call(
        functools.partial(_hg_kernel, reverse=reverse, finish=finish),
        grid_spec=pltpu.PrefetchScalarGridSpec(
            num_scalar_prefetch=2,
            grid=(H, len(blk)),
            in_specs=in_specs,
            out_specs=out_spec,
            scratch_shapes=[
                pltpu.VMEM((HG_DIM, HG_DIM), F32),
            ],
        ),
        out_shape=out_shape,
        compiler_params=_cparams(("parallel", "arbitrary")),
        name="hgrn2_bw" if reverse else "hgrn2_fw",
    )(jnp.asarray(blk), jnp.asarray(first), *args)


def _rope_tables(groups):
    half = MLA_ROPE // 2
    smax = max(S for _, S in groups)
    pos = jnp.arange(smax, dtype=F32)
    inv = 1.0 / (ROPE_BASE ** (jnp.arange(0, MLA_ROPE, 2, dtype=F32) / MLA_ROPE))
    ang = pos[:, None] * inv[None, :]
    cos, sin = jnp.cos(ang), jnp.sin(ang)
    z = jnp.zeros((smax, LANES - 2 * half), F32)
    zh = jnp.zeros((smax, half), F32)
    c = jnp.concatenate([cos, cos, z], axis=1)
    s_from_x2 = jnp.concatenate([-sin, zh, z], axis=1)
    s_from_x1 = jnp.concatenate([zh, sin, z], axis=1)
    return c, s_from_x2, s_from_x1


def _rope_block(t, c, s2, s1):
    half = MLA_ROPE // 2
    return t * c + pltpu.roll(t, LANES - half, 1) * s2 + pltpu.roll(t, half, 1) * s1


def _qprep_kernel(q_ref, g_ref, c_ref, s2_ref, s1_ref, o_ref, *, scale):
    c, s2, s1 = c_ref[...], s2_ref[...], s1_ref[...]
    for h in range(MLA_HEADS):
        a = q_ref[:, h * MLA_HEAD_PAD:h * MLA_HEAD_PAD + LANES].astype(F32)
        r = q_ref[:, h * MLA_HEAD_PAD + LANES:(h + 1) * MLA_HEAD_PAD].astype(F32)
        ms = (jnp.sum(a * a, axis=-1, keepdims=True) + jnp.sum(r * r, axis=-1, keepdims=True)) / MLA_QK
        inv = lax.rsqrt(ms + EPS)
        o_ref[:, h * MLA_HEAD_PAD:h * MLA_HEAD_PAD + LANES] = (a * inv * g_ref[:, :LANES] * scale).astype(o_ref.dtype)
        rn = r * inv * g_ref[:, LANES:]
        o_ref[:, h * MLA_HEAD_PAD + LANES:(h + 1) * MLA_HEAD_PAD] = (_rope_block(rn, c, s2, s1) * scale).astype(o_ref.dtype)


def _kprep_kernel(kv_ref, kr_ref, g_ref, c_ref, s2_ref, s1_ref, o_ref, vt_ref):
    c, s2, s1 = c_ref[...], s2_ref[...], s1_ref[...]
    r = kr_ref[...].astype(F32)
    rs = jnp.sum(r * r, axis=-1, keepdims=True)
    for h in range(MLA_HEADS):
        a = kv_ref[:, h * MLA_HEAD_PAD:h * MLA_HEAD_PAD + LANES].astype(F32)
        inv = lax.rsqrt((jnp.sum(a * a, axis=-1, keepdims=True) + rs) / MLA_QK + EPS)
        o_ref[:, h * MLA_HEAD_PAD:h * MLA_HEAD_PAD + LANES] = (a * inv * g_ref[:, :LANES]).astype(o_ref.dtype)
        rn = r * inv * g_ref[:, LANES:]
        o_ref[:, h * MLA_HEAD_PAD + LANES:(h + 1) * MLA_HEAD_PAD] = _rope_block(rn, c, s2, s1).astype(o_ref.dtype)
        v = kv_ref[:, h * MLA_HEAD_PAD + LANES:(h + 1) * MLA_HEAD_PAD].astype(F32)
        vt_ref[h * LANES:(h + 1) * LANES, :] = v.T.astype(vt_ref.dtype)


def _pos_block_map(groups, tm):
    bounds = []
    tok0 = 0
    for (B, S) in groups:
        assert S % tm == 0
        bounds.append((tok0 // tm, S // tm))
        tok0 += B * S

    def imap(i):
        out = (i - bounds[0][0]) % bounds[0][1]
        for (b0, per) in bounds[1:]:
            out = jnp.where(i >= b0, (i - b0) % per, out)
        return (out, 0)

    return imap


def _pad_gain(g):
    return jnp.concatenate([g.astype(F32), jnp.zeros((MLA_HEAD_PAD - MLA_QK,), F32)]).reshape(1, MLA_HEAD_PAD)


def _mla_prep(q, kv, cc, kr_block, groups, q_gain, k_gain, tm=256):
    T = q.shape[0]
    W = MLA_HEADS * MLA_HEAD_PAD
    tabs = _rope_tables(groups)
    pmap = _pos_block_map(groups, tm)
    tab_specs = [pl.BlockSpec((tm, LANES), pmap)] * 3
    row = pl.BlockSpec((tm, W), lambda i: (i, 0))
    gspec = pl.BlockSpec((1, MLA_HEAD_PAD), lambda i: (0, 0))
    qf = pl.pallas_call(
        functools.partial(_qprep_kernel, scale=MLA_QK ** -0.5 * LOG2_E),
        grid=(T // tm,),
        in_specs=[row, gspec] + tab_specs,
        out_specs=row,
        out_shape=jax.ShapeDtypeStruct((T, W), BF16),
        compiler_params=_cparams(("parallel",)),
        name="mla_qprep",
    )(q, _pad_gain(q_gain), *tabs)
    kf, vt = pl.pallas_call(
        _kprep_kernel,
        grid=(T // tm,),
        in_specs=[row, pl.BlockSpec((tm, LANES), lambda i: (i, kr_block)), gspec] + tab_specs,
        out_specs=[row, pl.BlockSpec((MLA_HEADS * LANES, tm), lambda i: (0, i))],
        out_shape=[jax.ShapeDtypeStruct((T, W), BF16), jax.ShapeDtypeStruct((MLA_HEADS * LANES, T), BF16)],
        compiler_params=_cparams(("parallel",)),
        name="mla_kprep",
    )(kv, cc, _pad_gain(k_gain), *tabs)
    return qf, kf, vt


FLASH_SUB = 512
LOG2_E = 1.4426950408889634


def _attn_kernel(q_ref, k_ref, vt_ref, o_ref):
    q = q_ref[...]
    nsub = k_ref.shape[0] // FLASH_SUB
    s = [_dot_nt(k_ref[c * FLASH_SUB:(c + 1) * FLASH_SUB, :], q) for c in range(nsub)]
    m = s[0].max(axis=0, keepdims=True)
    for c in range(1, nsub):
        m = jnp.maximum(m, s[c].max(axis=0, keepdims=True))
    l = jnp.zeros_like(m)
    acc = jnp.zeros((vt_ref.shape[0], q.shape[0]), F32)
    for c in range(nsub):
        p = jnp.exp2(s[c] - m)
        l = l + p.sum(axis=0, keepdims=True)
        acc = acc + jnp.dot(vt_ref[:, c * FLASH_SUB:(c + 1) * FLASH_SUB], p.astype(BF16), preferred_element_type=F32)
    o_ref[...] = (acc / l).T.astype(o_ref.dtype)


MLA_MAX_SEQ = 16384


def _mla_attention(qf, kf, vt, groups, tq=512):
    H = MLA_HEADS
    outs = []
    tok0 = 0
    for (B, S) in groups:
        assert S % tq == 0 and S % FLASH_SUB == 0 and tok0 % S == 0 and S <= MLA_MAX_SEQ
        nq = S // tq
        q0, s0 = tok0 // tq, tok0 // S
        once = pl.Buffered(1)
        outs.append(pl.pallas_call(
            _attn_kernel,
            grid=(B, H, nq),
            in_specs=[
                pl.BlockSpec((tq, MLA_HEAD_PAD), lambda b, h, qi, q0=q0, nq=nq: (q0 + b * nq + qi, h)),
                pl.BlockSpec((S, MLA_HEAD_PAD), lambda b, h, qi, s0=s0: (s0 + b, h), pipeline_mode=once),
                pl.BlockSpec((LANES, S), lambda b, h, qi, s0=s0: (h, s0 + b), pipeline_mode=once),
            ],
            out_specs=pl.BlockSpec((tq, LANES), lambda b, h, qi, nq=nq: (b * nq + qi, h)),
            out_shape=jax.ShapeDtypeStruct((B * S, H * LANES), BF16),
            compiler_params=_cparams(("parallel", "parallel", "parallel")),
            name="mla_attn",
        )(qf, kf, vt))
        tok0 += B * S
    return jnp.concatenate(outs, axis=0)


def _peer_candidates():
    a, b = [], []
    a += [0] * 16; b += list(range(16))
    for aa in (1, 2, 3):
        a += [aa] * 8; b += list(range(8))
    for bb in (0, 1):
        a += [4, 5, 6, 7]; b += [bb] * 4
    a += list(range(8, 16)); b += [0] * 8
    for bb in (2, 3):
        a += [4, 5, 6, 7]; b += [bb] * 4
    return np.asarray(a), np.asarray(b)


_CAND_A, _CAND_B = _peer_candidates()
N_CAND = len(_CAND_A)


def _extract16(val, order, exact):
    step = jnp.full(val.shape, float(PEER_TOPK), F32)
    tops = []
    for r in range(PEER_TOPK):
        m = val.max(axis=0, keepdims=True)
        if exact:
            first = jnp.where(val == m, order, 1e9).min(axis=0, keepdims=True)
            sel = order == first
        else:
            sel = val == m
        step = jnp.where(sel, float(r), step)
        val = jnp.where(sel, -jnp.inf, val)
        tops.append(m)
    return jnp.concatenate(tops, axis=0), step


def _route_head(s1, s2, pos, amat, exact):
    tm = s1.shape[1]
    idx = lax.broadcasted_iota(jnp.int32, s1.shape, 0).astype(F32)
    t1, r1 = _extract16(s1, idx, exact)
    t2, r2 = _extract16(s2, idx, exact)
    mid1 = jnp.concatenate([t1[4:8], t1[4:8]], axis=0)
    rep2 = lambda b: jnp.broadcast_to(t2[b:b + 1], (4, tm))
    cand = jnp.concatenate([
        t1[0:1] + t2, t1[1:2] + t2[0:8], t1[2:3] + t2[0:8], t1[3:4] + t2[0:8],
        mid1 + jnp.concatenate([rep2(0), rep2(1)], axis=0),
        t1[8:16] + t2[0:1],
        mid1 + jnp.concatenate([rep2(2), rep2(3)], axis=0)], axis=0)
    best = cand[0:1]
    _, cstep = _extract16(cand, pos, exact)
    chosen = (cstep < float(PEER_TOPK)).astype(F32)
    z = jnp.sum(chosen * jnp.exp(cand - best), axis=0, keepdims=True)
    per_a = jnp.dot(amat, chosen.astype(BF16), preferred_element_type=F32)
    cnt = jnp.zeros(s1.shape, F32)
    for a in range(PEER_TOPK):
        cnt = jnp.where(r1 == float(a), per_a[a:a + 1], cnt)
    e1n = jnp.exp(s1 - t1[0:1]) / z
    e2 = jnp.exp(s2 - t2[0:1])
    removed = lambda st: jnp.sum((st < float(PEER_TOPK)).astype(F32), axis=0, keepdims=True)
    ties = (jnp.abs(removed(r1) - PEER_TOPK) + jnp.abs(removed(r2) - PEER_TOPK)
            + jnp.abs(jnp.sum(chosen, axis=0, keepdims=True) - PEER_TOPK))
    return cnt, e1n, r2, e2, ties


def _peer_route_kernel(x_ref, g_ref, wq_ref, keys_ref, pos_ref, amat_ref,
                       hn_ref, cnt_ref, e1_ref, rb_ref, e2_ref, sc_ref):
    tm = x_ref.shape[0]
    hn = _rms(x_ref[...].astype(F32), g_ref[...]).astype(BF16)
    hn_ref[...] = hn
    qry = jnp.dot(hn, wq_ref[...], preferred_element_type=F32)
    for hp in range(2 * PEER_HEADS):
        sc_ref[hp] = _dot_nt(keys_ref[hp // 2, hp % 2], qry[:, hp * PEER_HALF:(hp + 1) * PEER_HALF].astype(BF16))

    def route(exact):
        ties = jnp.zeros((1, tm), F32)
        for h in range(PEER_HEADS):
            cnt, e1n, r2, e2, t = _route_head(sc_ref[2 * h], sc_ref[2 * h + 1], pos_ref[...], amat_ref[...], exact)
            cnt_ref[h] = cnt
            e1_ref[h] = e1n
            rb_ref[h] = pltpu.bitcast(r2.astype(BF16), jnp.uint32)
            e2_ref[h] = pltpu.bitcast(e2.astype(BF16), jnp.uint32)
            ties = ties + t
        return ties

    ties = route(exact=False)

    @pl.when(jnp.max(ties) > 0.0)
    def _():
        route(exact=True)


def _peer_route(x, norm_gain, w_query, sub_keys, tm=256):
    T, D = x.shape
    H = PEER_HEADS
    pos = np.broadcast_to((_CAND_A * PEER_TOPK + _CAND_B).astype(np.float32)[:, None], (N_CAND, tm))
    amat = (np.arange(PEER_TOPK)[:, None] == _CAND_A[None, :]).astype(np.float32)
    gate_shape = jax.ShapeDtypeStruct((H, PEER_N_KEYS, T), F32)
    packed_shape = jax.ShapeDtypeStruct((H, PEER_N_KEYS // 2, T), jnp.uint32)
    gate_spec = pl.BlockSpec((H, PEER_N_KEYS, tm), lambda i: (0, 0, i))
    packed_spec = pl.BlockSpec((H, PEER_N_KEYS // 2, tm), lambda i: (0, 0, i))
    return pl.pallas_call(
        _peer_route_kernel,
        grid=(T // tm,),
        in_specs=[
            pl.BlockSpec((tm, D), lambda i: (i, 0)),
            pl.BlockSpec((1, D), lambda i: (0, 0)),
            pl.BlockSpec(w_query.shape, lambda i: (0, 0)),
            pl.BlockSpec(sub_keys.shape, lambda i: (0, 0, 0, 0)),
            pl.BlockSpec((N_CAND, tm), lambda i: (0, 0)),
            pl.BlockSpec((PEER_TOPK, N_CAND), lambda i: (0, 0)),
        ],
        out_specs=[pl.BlockSpec((tm, D), lambda i: (i, 0)), gate_spec, gate_spec, packed_spec, packed_spec],
        out_shape=[jax.ShapeDtypeStruct((T, D), BF16), gate_shape, gate_shape, packed_shape, packed_shape],
        scratch_shapes=[pltpu.VMEM((2 * H, PEER_N_KEYS, tm), F32)],
        compiler_params=_cparams(("parallel",)),
        name="peer_route",
    )(x, norm_gain.reshape(1, D).astype(F32), w_query, sub_keys, jnp.asarray(pos), jnp.asarray(amat, BF16))


def _gelu(a):
    return 0.5 * a * (1.0 + lax.erf(a * (2.0 ** -0.5)))


def _gate_weights(cnt_ref, e1_ref, rb_ref, e2_ref, w_ref, step, slot, rows_per_step):
    tile = (PEER_N_KEYS, LANES)
    half = PEER_N_KEYS // 2
    for il in range(rows_per_step):
        i = step * rows_per_step + il
        cnt = [cnt_ref[pl.ds(h * PEER_N_KEYS + i, 1), :] for h in range(PEER_HEADS)]
        e1 = [e1_ref[pl.ds(h * PEER_N_KEYS + i, 1), :] for h in range(PEER_HEADS)]
        for st in range(w_ref.shape[2] // LANES):
            ls = slice(st * LANES, (st + 1) * LANES)
            w = jnp.zeros(tile, BF16)
            for h in range(PEER_HEADS):
                hs = slice(h * half, (h + 1) * half)
                rb = pltpu.bitcast(rb_ref[hs, ls], BF16)
                e2 = pltpu.bitcast(e2_ref[hs, ls], BF16)
                cnt_t = jnp.broadcast_to(cnt[h][:, ls], tile).astype(BF16)
                e1_t = jnp.broadcast_to(e1[h][:, ls], tile).astype(BF16)
                w = w + jnp.where(rb < cnt_t, e2, 0.0) * e1_t
            w_ref[slot, il * half:(il + 1) * half, ls] = pltpu.bitcast(w, jnp.uint32)


def _peer_expert_kernel(hn_ref, u_ref, vt_ref, cnt_ref, e1_ref, rb_ref, e2_ref, x_ref, o_ref, acc_ref, g_ref, w_ref,
                        *, rows_per_step):
    e = pl.program_id(1)
    gate = functools.partial(_gate_weights, cnt_ref, e1_ref, rb_ref, e2_ref, w_ref, rows_per_step=rows_per_step)

    @pl.when(e == 0)
    def _():
        acc_ref[...] = jnp.zeros_like(acc_ref)
        gate(step=0, slot=0)

    gate(step=jnp.minimum(e + 1, pl.num_programs(1) - 1), slot=(e + 1) % 2)
    act = _gelu(_dot_nt(u_ref[...], hn_ref[...]))
    g_ref[...] = (pltpu.bitcast(w_ref[e % 2], BF16).astype(F32) * act).astype(BF16)
    acc_ref[...] += jnp.dot(vt_ref[...], g_ref[...], preferred_element_type=F32)

    @pl.when(e == pl.num_programs(1) - 1)
    def _():
        o_ref[...] = x_ref[...] + acc_ref[...].T


def _peer_experts(x, hn, gates, u, vt, tm=512, te=1024):
    T, D = x.shape
    NE = u.shape[0]
    gates = [g.reshape(-1, T) for g in gates]
    once = pl.Buffered(1)
    gate_spec = pl.BlockSpec((PEER_HEADS * PEER_N_KEYS, tm), lambda i, e: (0, i), pipeline_mode=once)
    packed_spec = pl.BlockSpec((PEER_HEADS * PEER_N_KEYS // 2, tm), lambda i, e: (0, i), pipeline_mode=once)
    return pl.pallas_call(
        functools.partial(_peer_expert_kernel, rows_per_step=te // PEER_N_KEYS),
        grid=(T // tm, NE // te),
        in_specs=[
            pl.BlockSpec((tm, D), lambda i, e: (i, 0), pipeline_mode=once),
            pl.BlockSpec((te, D), lambda i, e: (e, 0)),
            pl.BlockSpec((D, te), lambda i, e: (0, e)),
            gate_spec, gate_spec, packed_spec, packed_spec,
            pl.BlockSpec((tm, D), lambda i, e: (i, 0), pipeline_mode=once),
        ],
        out_specs=pl.BlockSpec((tm, D), lambda i, e: (i, 0)),
        out_shape=jax.ShapeDtypeStruct((T, D), F32),
        scratch_shapes=[pltpu.VMEM((D, tm), F32), pltpu.VMEM((te, tm), BF16),
                        pltpu.VMEM((2, te // 2, tm), jnp.uint32)],
        compiler_params=_cparams(("parallel", "arbitrary")),
        name="peer_experts",
    )(hn, u, vt, *gates, x)


def _peer(x, norm_gain, w_query, sub_keys, u, v):
    hn, cnt, e1, rb, e2 = _peer_route(x, norm_gain, w_query.astype(BF16), sub_keys.astype(BF16))
    return _peer_experts(x, hn, (cnt, e1, rb, e2), u.astype(BF16), v.astype(BF16).T)


def _even_layer(x, groups, norm, w_in, na_q_gain, na_k_gain, na_rpb, lb, hg_out_gain, w_out):
    p = _matmul(x, w_in.astype(BF16), gain=norm, out_dtype=F32)
    na = _neighbourhood_attention(p, groups, na_rpb, na_q_gain, na_k_gain)
    o_fw = _hgrn2(p, groups, lb[0], reverse=False, f_col=4 * NA_HEADS)
    hg = _hgrn2(p, groups, lb[1], reverse=True, f_col=5 * NA_HEADS, o_fw=o_fw, out_gain=hg_out_gain)
    w_na, w_hg = jnp.split(w_out.astype(BF16), [na.shape[1]], axis=0)
    return _matmul(na, w_na, pair=(hg, w_hg), residual=x, out_dtype=F32)


def _odd_layer(x, groups, norm, w_in, q_norm, kv_norm, w_uq, w_ukv, q_gain, k_gain, w_out):
    D = x.shape[1]
    pad_q = (-MLA_Q_RANK) % MLA_KV_RANK
    w_cq, w_ckv, w_kr = jnp.split(w_in, [MLA_Q_RANK, MLA_Q_RANK + MLA_KV_RANK], axis=1)
    w_in_p = jnp.concatenate(
        [w_cq, jnp.zeros((D, pad_q), w_in.dtype), w_ckv, w_kr, jnp.zeros((D, LANES - MLA_ROPE), w_in.dtype)],
        axis=1).astype(BF16)
    cc = _matmul(x, w_in_p, gain=norm, out_dtype=F32, tm=512, tn=w_in_p.shape[1])
    w_uq_p = jnp.pad(w_uq.reshape(MLA_Q_RANK, MLA_HEADS, MLA_QK),
                     ((0, 0), (0, 0), (0, MLA_HEAD_PAD - MLA_QK))).reshape(MLA_Q_RANK, -1).astype(BF16)
    q = _matmul(cc, w_uq_p, gain=q_norm, out_dtype=BF16, x_col_block=0)
    kv = _matmul(cc, w_ukv.astype(BF16), gain=kv_norm, out_dtype=BF16,
                 x_col_block=(MLA_Q_RANK + pad_q) // MLA_KV_RANK)
    kr_block = (MLA_Q_RANK + pad_q + MLA_KV_RANK) // LANES
    qf, kf, vt = _mla_prep(q, kv, cc, kr_block, groups, q_gain, k_gain)
    o = _mla_attention(qf, kf, vt, groups)
    return _matmul(o, w_out.astype(BF16), residual=x, out_dtype=F32)


def _forward(xs, ev_norm, ev_w_in, ev_na_q_gain, ev_na_k_gain, ev_na_rpb, ev_hg_lb_logits, ev_hg_out_gain,
             ev_w_out, od_norm, od_w_in, od_q_norm, od_kv_norm, od_w_uq, od_w_ukv, od_q_gain, od_k_gain,
             od_w_out, ff_norm, peer_w_query, peer_sub_keys, peer_u, peer_v):
    groups = [(int(x.shape[0]), int(x.shape[1])) for x in xs]
    D = xs[0].shape[-1]
    depth = ff_norm.shape[0]
    lb_all = jnp.cumsum(jax.nn.softmax(ev_hg_lb_logits.astype(F32), axis=1), axis=1)
    x = jnp.concatenate([t.reshape(-1, D) for t in xs], axis=0)
    for layer in range(depth):
        j = layer // 2
        if layer % 2 == 0:
            x = _even_layer(x, groups, ev_norm[j], ev_w_in[j], ev_na_q_gain[j], ev_na_k_gain[j], ev_na_rpb[j],
                            lb_all[:, j], ev_hg_out_gain[j], ev_w_out[j])
        else:
            x = _odd_layer(x, groups, od_norm[j], od_w_in[j], od_q_norm[j], od_kv_norm[j], od_w_uq[j],
                           od_w_ukv[j], od_q_gain[j], od_k_gain[j], od_w_out[j])
        x = _peer(x, ff_norm[layer], peer_w_query[layer], peer_sub_keys[layer], peer_u[layer], peer_v[layer])
    outs = []
    tok0 = 0
    for (B, S) in groups:
        outs.append(x[tok0:tok0 + B * S].reshape(B, S, D))
        tok0 += B * S
    return tuple(outs)


def kernel(x_prompt, x_sample, ev_norm, ev_w_in, ev_na_q_gain, ev_na_k_gain, ev_na_rpb, ev_hg_lb_logits, ev_hg_out_gain, ev_w_out, od_norm, od_w_in, od_q_norm, od_kv_norm, od_w_uq, od_w_ukv, od_q_gain, od_k_gain, od_w_out, ff_norm, peer_w_query, peer_sub_keys, peer_u, peer_v):
    return _forward((x_prompt, x_sample), ev_norm, ev_w_in, ev_na_q_gain, ev_na_k_gain, ev_na_rpb, ev_hg_lb_logits,
                    ev_hg_out_gain, ev_w_out, od_norm, od_w_in, od_q_norm, od_kv_norm, od_w_uq, od_w_ukv, od_q_gain,
                    od_k_gain, od_w_out, ff_norm, peer_w_query, peer_sub_keys, peer_u, peer_v)
```

```python
import functools

import numpy as np
import jax
import jax.numpy as jnp
from jax import lax
from jax.experimental import pallas as pl
from jax.experimental.pallas import tpu as pltpu

F32 = jnp.float32
BF16 = jnp.bfloat16

EPS = 1e-6
LANES = 128
VMEM_LIMIT_BYTES = 56 * 1024 * 1024
NEG_BIG = -1e30

GRID_W = 64
NA_HEADS = 8
NA_HEAD_DIM = 128
NA_WIN_ROWS = 8
NA_WIN_COLS = 16
NA_ROW_BLOCK = 8
NA_KEY_ROWS = 16
HG_HEADS = 8
HG_DIM = 128
HG_CHUNK = 32
MLA_HEADS = 16
MLA_Q_RANK = 768
MLA_KV_RANK = 512
MLA_NOPE = 128
MLA_ROPE = 64
MLA_QK = MLA_NOPE + MLA_ROPE
MLA_HEAD_PAD = 256
ROPE_BASE = 10000.0
PEER_HEADS = 8
PEER_N_KEYS = 128
PEER_TOPK = 16
PEER_HALF = 128


def _cparams(sem):
    return pltpu.CompilerParams(dimension_semantics=sem, vmem_limit_bytes=VMEM_LIMIT_BYTES)


def _rms(x, gain):
    return x * lax.rsqrt(jnp.mean(x * x, axis=-1, keepdims=True) + EPS) * gain


def _dot_nt(a, b):
    return lax.dot_general(a, b, (((1,), (1,)), ((), ())), preferred_element_type=F32)


def _dot_tn(a, b):
    return lax.dot_general(a, b, (((0,), (0,)), ((), ())), preferred_element_type=F32)


def _mm_kernel(*refs, has_norm, has_pair, has_res):
    it = iter(refs)
    x_ref = next(it)
    g_ref = next(it) if has_norm else None
    w_ref = next(it)
    x2_ref, w2_ref = (next(it), next(it)) if has_pair else (None, None)
    r_ref = next(it) if has_res else None
    o_ref = next(it)
    if has_norm:
        hn_ref = next(it)

        @pl.when(pl.program_id(1) == 0)
        def _():
            hn_ref[...] = _rms(x_ref[...].astype(F32), g_ref[...]).astype(BF16)

        a = hn_ref[...]
    else:
        a = x_ref[...].astype(BF16)
    acc = jnp.dot(a, w_ref[...], preferred_element_type=F32)
    if has_pair:
        acc = acc + jnp.dot(x2_ref[...].astype(BF16), w2_ref[...], preferred_element_type=F32)
    if has_res:
        acc = acc + r_ref[...]
    o_ref[...] = acc.astype(o_ref.dtype)


def _matmul(x, w, *, gain=None, pair=None, residual=None, out_dtype=F32, x_col_block=0, tm=1024, tn=1024):
    M = x.shape[0]
    K, N = w.shape
    tn = min(tn, N)
    assert M % tm == 0 and N % tn == 0 and (x_col_block + 1) * K <= x.shape[1]
    has_norm, has_pair, has_res = gain is not None, pair is not None, residual is not None
    in_specs = [pl.BlockSpec((tm, K), lambda i, j: (i, x_col_block))]
    args = [x]
    if has_norm:
        in_specs.append(pl.BlockSpec((1, K), lambda i, j: (0, 0)))
        args.append(gain.reshape(1, K).astype(F32))
    in_specs.append(pl.BlockSpec((K, tn), lambda i, j: (0, j)))
    args.append(w)
    if has_pair:
        x2, w2 = pair
        assert x2.shape == (M, w2.shape[0]) and w2.shape[1] == N
        in_specs += [pl.BlockSpec((tm, w2.shape[0]), lambda i, j: (i, 0)), pl.BlockSpec((w2.shape[0], tn), lambda i, j: (0, j))]
        args += [x2, w2]
    if has_res:
        in_specs.append(pl.BlockSpec((tm, tn), lambda i, j: (i, j)))
        args.append(residual)
    return pl.pallas_call(
        functools.partial(_mm_kernel, has_norm=has_norm, has_pair=has_pair, has_res=has_res),
        grid=(M // tm, N // tn),
        in_specs=in_specs,
        out_specs=pl.BlockSpec((tm, tn), lambda i, j: (i, j)),
        out_shape=jax.ShapeDtypeStruct((M, N), out_dtype),
        scratch_shapes=[pltpu.VMEM((tm, K), BF16)] if has_norm else [],
        compiler_params=_cparams(("parallel", "arbitrary")),
        name="mm",
    )(*args)


NA_Q = NA_ROW_BLOCK * GRID_W
NA_KB = 256
NA_NKB = NA_KEY_ROWS * GRID_W // NA_KB


def _na_bias_tables(rpb):
    i = np.arange(NA_ROW_BLOCK)[:, None, None, None]
    c = np.arange(GRID_W)[None, :, None, None]
    j = np.arange(NA_KEY_ROWS)[None, None, :, None]
    kc = np.arange(GRID_W)[None, None, None, :]
    half = NA_WIN_ROWS // 2
    qc0 = np.clip(c - NA_WIN_COLS // 2, 0, GRID_W - NA_WIN_COLS)
    col_ok = (kc >= qc0) & (kc < qc0 + NA_WIN_COLS)
    dcol = np.clip(kc - c + NA_WIN_COLS - 1, 0, 2 * NA_WIN_COLS - 2)
    col_hot = (dcol[0, :, 0, :, None] == np.arange(2 * NA_WIN_COLS - 1)).astype(np.float32)
    by_col = jnp.einsum('hab,ckb->hack', rpb.astype(F32), col_hot, precision=lax.Precision.HIGHEST)
    tabs = []
    for var in range(3):
        if var == 0:
            r0 = np.maximum(i - half, 0)
            dr = j - i
        elif var == 1:
            r0 = i
            dr = j - half - i
        else:
            r0 = np.minimum(i + half, NA_KEY_ROWS - NA_WIN_ROWS)
            dr = j - (NA_KEY_ROWS - NA_ROW_BLOCK) - i
        ok = (j >= r0) & (j < r0 + NA_WIN_ROWS) & col_ok
        dri = np.clip(dr + NA_WIN_ROWS - 1, 0, 2 * NA_WIN_ROWS - 2)[:, 0, :, 0]
        row_hot = (dri[:, :, None] == np.arange(2 * NA_WIN_ROWS - 1)).astype(np.float32)
        vals = jnp.einsum('ija,hack->hicjk', row_hot, by_col, precision=lax.Precision.HIGHEST)
        tabs.append(jnp.where(jnp.asarray(ok)[None], vals, NEG_BIG).reshape(
            NA_HEADS, NA_Q, NA_KEY_ROWS * GRID_W))
    return jnp.stack(tabs)


def _na_steps(groups):
    steps = []
    tok0 = 0
    for (B, S) in groups:
        rows = S // GRID_W
        nrb = rows // NA_ROW_BLOCK
        assert nrb >= 3 and S % NA_Q == 0
        for b in range(B):
            for rb in range(nrb):
                var = 0 if rb == 0 else (2 if rb == nrb - 1 else 1)
                kr0 = min(max(NA_ROW_BLOCK * rb - NA_WIN_ROWS // 2, 0), rows - NA_KEY_ROWS)
                steps.append((var, (tok0 + b * S) // NA_Q + rb, (tok0 + b * S + kr0 * GRID_W) // NA_KB))
        tok0 += B * S
    steps.sort(key=lambda t: t[0])
    arr = np.asarray(steps, np.int32)
    return arr[:, 0], arr[:, 1], arr[:, 2]


def _na_kernel(var_ref, qb_ref, kb_ref, q_ref, *rest):
    k_refs = rest[:NA_NKB]
    v_refs = rest[NA_NKB:2 * NA_NKB]
    bias_ref, qg_ref, kg_ref, o_ref = rest[2 * NA_NKB:]
    scale = NA_HEAD_DIM ** -0.5
    q = _rms(q_ref[...].astype(F32), qg_ref[...]).astype(BF16)
    s = []
    for m in range(NA_NKB):
        k = _rms(k_refs[m][...].astype(F32), kg_ref[...]).astype(BF16)
        s.append(_dot_nt(q, k) * scale + bias_ref[0, 0, :, m * NA_KB:(m + 1) * NA_KB])
    mx = s[0].max(axis=-1, keepdims=True)
    for m in range(1, NA_NKB):
        mx = jnp.maximum(mx, s[m].max(axis=-1, keepdims=True))
    l = jnp.zeros_like(mx)
    acc = jnp.zeros((NA_Q, NA_HEAD_DIM), F32)
    for m in range(NA_NKB):
        p = jnp.exp(s[m] - mx)
        l = l + p.sum(axis=-1, keepdims=True)
        acc = acc + jnp.dot(p.astype(BF16), v_refs[m][...].astype(BF16), preferred_element_type=F32)
    o_ref[...] = (acc / l).astype(o_ref.dtype)


def _neighbourhood_attention(p, groups, rpb, q_gain, k_gain):
    T = p.shape[0]
    out_cols = NA_HEADS * NA_HEAD_DIM
    var, qblk, kblk = _na_steps(groups)
    tabs = _na_bias_tables(rpb)
    H = NA_HEADS

    def qmap(h, n, var, qb, kb):
        return (qb[n], h)

    def kmap(m, off):
        return lambda h, n, var, qb, kb: (kb[n] + m, off + h)

    in_specs = [pl.BlockSpec((NA_Q, NA_HEAD_DIM), qmap)]
    in_specs += [pl.BlockSpec((NA_KB, NA_HEAD_DIM), kmap(m, H)) for m in range(NA_NKB)]
    in_specs += [pl.BlockSpec((NA_KB, NA_HEAD_DIM), kmap(m, 2 * H)) for m in range(NA_NKB)]
    in_specs += [
        pl.BlockSpec((1, 1, NA_Q, NA_KEY_ROWS * GRID_W), lambda h, n, var, qb, kb: (var[n], h, 0, 0)),
        pl.BlockSpec((1, NA_HEAD_DIM), lambda h, n, var, qb, kb: (0, 0)),
        pl.BlockSpec((1, NA_HEAD_DIM), lambda h, n, var, qb, kb: (0, 0)),
    ]
    return pl.pallas_call(
        _na_kernel,
        grid_spec=pltpu.PrefetchScalarGridSpec(
            num_scalar_prefetch=3,
            grid=(H, len(var)),
            in_specs=in_specs,
            out_specs=pl.BlockSpec((NA_Q, NA_HEAD_DIM), qmap),
        ),
        out_shape=jax.ShapeDtypeStruct((T, out_cols), BF16),
        compiler_params=_cparams(("parallel", "arbitrary")),
        name="natten",
    )(jnp.asarray(var), jnp.asarray(qblk), jnp.asarray(kblk), p, *([p] * (2 * NA_NKB)), tabs,
      q_gain.reshape(1, -1).astype(F32), k_gain.reshape(1, -1).astype(F32))


HG_BLOCK = 1024
HG_NCHUNK = HG_BLOCK // HG_CHUNK
HG_SUB = 8
assert HG_CHUNK == 4 * HG_SUB


def _hg_steps(groups, reverse):
    blk, first = [], []
    tok0 = 0
    for (B, S) in groups:
        nb = S // HG_BLOCK
        assert S % HG_BLOCK == 0
        for b in range(B):
            order = range(nb - 1, -1, -1) if reverse else range(nb)
            for n, sb in enumerate(order):
                blk.append((tok0 + b * S) // HG_BLOCK + sb)
                first.append(1 if n == 0 else 0)
        tok0 += B * S
    return np.asarray(blk, np.int32), np.asarray(first, np.int32)


def _hg_matrices(reverse):
    C, S = HG_CHUNK, HG_SUB
    t = np.arange(C)[:, None]
    u = np.arange(C)[None, :]
    upto = (u >= t) if reverse else (u <= t)
    after = (u < t) if reverse else (u > t)
    same = lambda m: (u // m) == (t // m)
    sums = np.concatenate([upto, upto & same(S), upto & same(2 * S), after & same(S), after & same(2 * S)], axis=0)
    query_side = 0 if reverse else 1
    masks = []
    for m in (S, 2 * S):
        bt, bs = t // m, u // m
        masks.append((bt % 2 == query_side) & (bs == (bt + 1 if reverse else bt - 1)))
    return sums.astype(np.float32), np.stack(masks).astype(np.float32)


def _hg_kernel(blk_ref, first_ref, q_ref, f_ref, i_ref, lb_ref, sums_ref, mask_ref, *rest, reverse, finish):
    if finish:
        ofw_ref, g_ref, gain_ref, o_ref, st_ref = rest
    else:
        o_ref, st_ref = rest
    h = pl.program_id(0)
    n = pl.program_id(1)
    C = HG_CHUNK

    @pl.when(first_ref[n] == 1)
    def _():
        st_ref[...] = jnp.zeros_like(st_ref)

    lb = lb_ref[pl.ds(h, 1), :]
    sums = sums_ref[...]
    mask0, mask1 = mask_ref[0], mask_ref[1]
    S = HG_SUB
    NC = HG_NCHUNK
    TB = HG_BLOCK
    blk3 = (TB // S, S, HG_DIM)
    row8 = lax.broadcasted_iota(jnp.int32, blk3, 1)
    ones = jnp.ones((HG_DIM, HG_DIM), BF16)
    rows = lambda x, c: x[c * C:(c + 1) * C]

    f = lb + (1.0 - lb) * jax.nn.sigmoid(f_ref[...].astype(F32))
    logf = jnp.log(f)
    kk = 1.0 - f
    q = q_ref[...].astype(F32)
    q = q * jax.nn.sigmoid(q)
    v = i_ref[...].astype(F32)
    logf_hi = logf.astype(BF16)
    logf_lo = (logf - logf_hi.astype(F32)).astype(BF16)
    e_cat = (jnp.dot(sums, jnp.concatenate([rows(logf_hi, c) for c in range(NC)], axis=1), preferred_element_type=F32)
             + jnp.dot(sums, jnp.concatenate([rows(logf_lo, c) for c in range(NC)], axis=1),
                       preferred_element_type=F32))
    part = lambda r: jnp.concatenate(
        [e_cat[r * C:(r + 1) * C, c * HG_DIM:(c + 1) * HG_DIM] for c in range(NC)], axis=0)
    b, bloc, eq1, ek0, ek1 = [part(r) for r in range(5)]
    b_end = jnp.concatenate(
        [jnp.broadcast_to(rows(b, c)[0:1] if reverse else rows(b, c)[C - 1:C], (C, HG_DIM)) for c in range(NC)], axis=0)
    qb = (q * jnp.exp(b)).astype(BF16)
    q0, k0 = (q * jnp.exp(bloc)).astype(BF16), (kk * jnp.exp(ek0)).astype(BF16)
    q1, k1 = (q * jnp.exp(eq1)).astype(BF16), (kk * jnp.exp(ek1)).astype(BF16)
    ks = (kk * jnp.exp(b_end - b)).astype(BF16)
    vb = v.astype(BF16)
    o_off, upd = [], []
    for c in range(NC):
        a_off = (_dot_nt(rows(q0, c), rows(k0, c)) * mask0 + _dot_nt(rows(q1, c), rows(k1, c)) * mask1)
        o_off.append(jnp.dot(a_off.astype(BF16), rows(vb, c), preferred_element_type=F32))
        upd.append(_dot_tn(rows(vb, c), rows(ks, c)))
    q3, k3, v3, b3 = q.reshape(blk3), kk.reshape(blk3), v.reshape(blk3), bloc.reshape(blk3)
    prods = []
    for ds in range(S):
        keep = (row8 <= ds) if reverse else (row8 >= ds)
        e = jnp.exp(jnp.where(keep, b3 - b3[:, ds:ds + 1, :], -jnp.inf))
        prods.append((q3 * e * k3[:, ds:ds + 1, :]).reshape(TB, HG_DIM).astype(BF16))
    a_bc = jnp.dot(jnp.concatenate(prods, axis=0), ones, preferred_element_type=F32)
    o_diag = jnp.zeros(blk3, F32)
    for ds in range(S):
        o_diag = o_diag + a_bc[ds * TB:(ds + 1) * TB, :].reshape(blk3) * v3[:, ds:ds + 1, :]
    st = st_ref[...]
    o_inter = [None] * NC
    for cc in range(NC):
        c = NC - 1 - cc if reverse else cc
        o_inter[c] = _dot_nt(rows(qb, c), st.astype(BF16))
        st = st * jnp.exp(rows(b_end, c)[0:1]) + upd[c]
    st_ref[...] = st
    o_blk = jnp.concatenate([o_inter[c] + o_off[c] for c in range(NC)], axis=0) + o_diag.reshape(TB, HG_DIM)

    if finish:
        o = o_blk + ofw_ref[...]
        g = g_ref[...].astype(F32)
        o_ref[...] = (_rms(o, gain_ref[...]) * (g * jax.nn.sigmoid(g))).astype(o_ref.dtype)
    else:
        o_ref[...] = o_blk


def _hgrn2(p, groups, lb, *, reverse, f_col, o_fw=None, out_gain=None):
    T = p.shape[0]
    blk, first = _hg_steps(groups, reverse)
    H = HG_HEADS
    finish = o_fw is not None
    C = HG_CHUNK
    sums, masks = _hg_matrices(reverse)

    def col(off):
        return lambda h, n, blk, first: (blk[n], off + h)

    const = lambda h, n, blk, first: (0, 0)
    in_specs = [
        pl.BlockSpec((HG_BLOCK, HG_DIM), col(3 * NA_HEADS)),
        pl.BlockSpec((HG_BLOCK, HG_DIM), col(f_col)),
        pl.BlockSpec((HG_BLOCK, HG_DIM), col(6 * NA_HEADS)),
        pl.BlockSpec((H, HG_DIM), const),
        pl.BlockSpec(sums.shape, const),
        pl.BlockSpec(masks.shape, lambda h, n, blk, first: (0, 0, 0)),
    ]
    args = [p, p, p, lb.reshape(H, HG_DIM).astype(F32), jnp.asarray(sums, BF16), jnp.asarray(masks)]
    if finish:
        in_specs += [
            pl.BlockSpec((HG_BLOCK, HG_DIM), col(0)),
            pl.BlockSpec((HG_BLOCK, HG_DIM), col(7 * NA_HEADS)),
            pl.BlockSpec((1, HG_DIM), const),
        ]
        args += [o_fw, p, out_gain.reshape(1, HG_DIM).astype(F32)]
    out_spec = pl.BlockSpec((HG_BLOCK, HG_DIM), col(0))
    out_shape = jax.ShapeDtypeStruct((T, H * HG_DIM), BF16 if finish else F32)

    return pl.pallas_call(
        functools.partial(_hg_kernel, reverse=reverse, finish=finish),
        grid_spec=pltpu.PrefetchScalarGridSpec(
            num_scalar_prefetch=2,
            grid=(H, len(blk)),
            in_specs=in_specs,
            out_specs=out_spec,
            scratch_shapes=[
                pltpu.VMEM((HG_DIM, HG_DIM), F32),
            ],
        ),
        out_shape=out_shape,
        compiler_params=_cparams(("parallel", "arbitrary")),
        name="hgrn2_bw" if reverse else "hgrn2_fw",
    )(jnp.asarray(blk), jnp.asarray(first), *args)


def _rope_tables(groups):
    half = MLA_ROPE // 2
    smax = max(S for _, S in groups)
    pos = jnp.arange(smax, dtype=F32)
    inv = 1.0 / (ROPE_BASE ** (jnp.arange(0, MLA_ROPE, 2, dtype=F32) / MLA_ROPE))
    ang = pos[:, None] * inv[None, :]
    cos, sin = jnp.cos(ang), jnp.sin(ang)
    z = jnp.zeros((smax, LANES - 2 * half), F32)
    zh = jnp.zeros((smax, half), F32)
    c = jnp.concatenate([cos, cos, z], axis=1)
    s_from_x2 = jnp.concatenate([-sin, zh, z], axis=1)
    s_from_x1 = jnp.concatenate([zh, sin, z], axis=1)
    return c, s_from_x2, s_from_x1


def _rope_block(t, c, s2, s1):
    half = MLA_ROPE // 2
    return t * c + pltpu.roll(t, LANES - half, 1) * s2 + pltpu.roll(t, half, 1) * s1


def _qprep_kernel(q_ref, g_ref, c_ref, s2_ref, s1_ref, o_ref, *, scale):
    c, s2, s1 = c_ref[...], s2_ref[...], s1_ref[...]
    for h in range(MLA_HEADS):
        a = q_ref[:, h * MLA_HEAD_PAD:h * MLA_HEAD_PAD + LANES].astype(F32)
        r = q_ref[:, h * MLA_HEAD_PAD + LANES:(h + 1) * MLA_HEAD_PAD].astype(F32)
        ms = (jnp.sum(a * a, axis=-1, keepdims=True) + jnp.sum(r * r, axis=-1, keepdims=True)) / MLA_QK
        inv = lax.rsqrt(ms + EPS)
        o_ref[:, h * MLA_HEAD_PAD:h * MLA_HEAD_PAD + LANES] = (a * inv * g_ref[:, :LANES] * scale).astype(o_ref.dtype)
        rn = r * inv * g_ref[:, LANES:]
        o_ref[:, h * MLA_HEAD_PAD + LANES:(h + 1) * MLA_HEAD_PAD] = (_rope_block(rn, c, s2, s1) * scale).astype(o_ref.dtype)


def _kprep_kernel(kv_ref, kr_ref, g_ref, c_ref, s2_ref, s1_ref, o_ref, vt_ref):
    c, s2, s1 = c_ref[...], s2_ref[...], s1_ref[...]
    r = kr_ref[...].astype(F32)
    rs = jnp.sum(r * r, axis=-1, keepdims=True)
    for h in range(MLA_HEADS):
        a = kv_ref[:, h * MLA_HEAD_PAD:h * MLA_HEAD_PAD + LANES].astype(F32)
        inv = lax.rsqrt((jnp.sum(a * a, axis=-1, keepdims=True) + rs) / MLA_QK + EPS)
        o_ref[:, h * MLA_HEAD_PAD:h * MLA_HEAD_PAD + LANES] = (a * inv * g_ref[:, :LANES]).astype(o_ref.dtype)
        rn = r * inv * g_ref[:, LANES:]
        o_ref[:, h * MLA_HEAD_PAD + LANES:(h + 1) * MLA_HEAD_PAD] = _rope_block(rn, c, s2, s1).astype(o_ref.dtype)
        v = kv_ref[:, h * MLA_HEAD_PAD + LANES:(h + 1) * MLA_HEAD_PAD].astype(F32)
        vt_ref[h * LANES:(h + 1) * LANES, :] = v.T.astype(vt_ref.dtype)


def _pos_block_map(groups, tm):
    bounds = []
    tok0 = 0
    for (B, S) in groups:
        assert S % tm == 0
        bounds.append((tok0 // tm, S // tm))
        tok0 += B * S

    def imap(i):
        out = (i - bounds[0][0]) % bounds[0][1]
        for (b0, per) in bounds[1:]:
            out = jnp.where(i >= b0, (i - b0) % per, out)
        return (out, 0)

    return imap


def _pad_gain(g):
    return jnp.concatenate([g.astype(F32), jnp.zeros((MLA_HEAD_PAD - MLA_QK,), F32)]).reshape(1, MLA_HEAD_PAD)


def _mla_prep(q, kv, cc, kr_block, groups, q_gain, k_gain, tm=256):
    T = q.shape[0]
    W = MLA_HEADS * MLA_HEAD_PAD
    tabs = _rope_tables(groups)
    pmap = _pos_block_map(groups, tm)
    tab_specs = [pl.BlockSpec((tm, LANES), pmap)] * 3
    row = pl.BlockSpec((tm, W), lambda i: (i, 0))
    gspec = pl.BlockSpec((1, MLA_HEAD_PAD), lambda i: (0, 0))
    qf = pl.pallas_call(
        functools.partial(_qprep_kernel, scale=MLA_QK ** -0.5 * LOG2_E),
        grid=(T // tm,),
        in_specs=[row, gspec] + tab_specs,
        out_specs=row,
        out_shape=jax.ShapeDtypeStruct((T, W), BF16),
        compiler_params=_cparams(("parallel",)),
        name="mla_qprep",
    )(q, _pad_gain(q_gain), *tabs)
    kf, vt = pl.pallas_call(
        _kprep_kernel,
        grid=(T // tm,),
        in_specs=[row, pl.BlockSpec((tm, LANES), lambda i: (i, kr_block)), gspec] + tab_specs,
        out_specs=[row, pl.BlockSpec((MLA_HEADS * LANES, tm), lambda i: (0, i))],
        out_shape=[jax.ShapeDtypeStruct((T, W), BF16), jax.ShapeDtypeStruct((MLA_HEADS * LANES, T), BF16)],
        compiler_params=_cparams(("parallel",)),
        name="mla_kprep",
    )(kv, cc, _pad_gain(k_gain), *tabs)
    return qf, kf, vt


FLASH_SUB = 512
LOG2_E = 1.4426950408889634


def _attn_kernel(q_ref, k_ref, vt_ref, o_ref):
    q = q_ref[...]
    nsub = k_ref.shape[0] // FLASH_SUB
    s = [_dot_nt(k_ref[c * FLASH_SUB:(c + 1) * FLASH_SUB, :], q) for c in range(nsub)]
    m = s[0].max(axis=0, keepdims=True)
    for c in range(1, nsub):
        m = jnp.maximum(m, s[c].max(axis=0, keepdims=True))
    l = jnp.zeros_like(m)
    acc = jnp.zeros((vt_ref.shape[0], q.shape[0]), F32)
    for c in range(nsub):
        p = jnp.exp2(s[c] - m)
        l = l + p.sum(axis=0, keepdims=True)
        acc = acc + jnp.dot(vt_ref[:, c * FLASH_SUB:(c + 1) * FLASH_SUB], p.astype(BF16), preferred_element_type=F32)
    o_ref[...] = (acc / l).T.astype(o_ref.dtype)


MLA_MAX_SEQ = 16384


def _mla_attention(qf, kf, vt, groups, tq=512):
    H = MLA_HEADS
    outs = []
    tok0 = 0
    for (B, S) in groups:
        assert S % tq == 0 and S % FLASH_SUB == 0 and tok0 % S == 0 and S <= MLA_MAX_SEQ
        nq = S // tq
        q0, s0 = tok0 // tq, tok0 // S
        once = pl.Buffered(1)
        outs.append(pl.pallas_call(
            _attn_kernel,
            grid=(B, H, nq),
            in_specs=[
                pl.BlockSpec((tq, MLA_HEAD_PAD), lambda b, h, qi, q0=q0, nq=nq: (q0 + b * nq + qi, h)),
                pl.BlockSpec((S, MLA_HEAD_PAD), lambda b, h, qi, s0=s0: (s0 + b, h), pipeline_mode=once),
                pl.BlockSpec((LANES, S), lambda b, h, qi, s0=s0: (h, s0 + b), pipeline_mode=once),
            ],
            out_specs=pl.BlockSpec((tq, LANES), lambda b, h, qi, nq=nq: (b * nq + qi, h)),
            out_shape=jax.ShapeDtypeStruct((B * S, H * LANES), BF16),
            compiler_params=_cparams(("parallel", "parallel", "parallel")),
            name="mla_attn",
        )(qf, kf, vt))
        tok0 += B * S
    return jnp.concatenate(outs, axis=0)


def _peer_candidates():
    a, b = [], []
    a += [0] * 16; b += list(range(16))
    for aa in (1, 2, 3):
        a += [aa] * 8; b += list(range(8))
    for bb in (0, 1):
        a += [4, 5, 6, 7]; b += [bb] * 4
    a += list(range(8, 16)); b += [0] * 8
    for bb in (2, 3):
        a += [4, 5, 6, 7]; b += [bb] * 4
    return np.asarray(a), np.asarray(b)


_CAND_A, _CAND_B = _peer_candidates()
N_CAND = len(_CAND_A)


def _extract16(val, order, exact):
    step = jnp.full(val.shape, float(PEER_TOPK), F32)
    tops = []
    for r in range(PEER_TOPK):
        m = val.max(axis=0, keepdims=True)
        if exact:
            first = jnp.where(val == m, order, 1e9).min(axis=0, keepdims=True)
            sel = order == first
        else:
            sel = val == m
        step = jnp.where(sel, float(r), step)
        val = jnp.where(sel, -jnp.inf, val)
        tops.append(m)
    return jnp.concatenate(tops, axis=0), step


def _route_head(s1, s2, pos, amat, exact):
    tm = s1.shape[1]
    idx = lax.broadcasted_iota(jnp.int32, s1.shape, 0).astype(F32)
    t1, r1 = _extract16(s1, idx, exact)
    t2, r2 = _extract16(s2, idx, exact)
    mid1 = jnp.concatenate([t1[4:8], t1[4:8]], axis=0)
    rep2 = lambda b: jnp.broadcast_to(t2[b:b + 1], (4, tm))
    cand = jnp.concatenate([
        t1[0:1] + t2, t1[1:2] + t2[0:8], t1[2:3] + t2[0:8], t1[3:4] + t2[0:8],
        mid1 + jnp.concatenate([rep2(0), rep2(1)], axis=0),
        t1[8:16] + t2[0:1],
        mid1 + jnp.concatenate([rep2(2), rep2(3)], axis=0)], axis=0)
    best = cand[0:1]
    _, cstep = _extract16(cand, pos, exact)
    chosen = (cstep < float(PEER_TOPK)).astype(F32)
    z = jnp.sum(chosen * jnp.exp(cand - best), axis=0, keepdims=True)
    per_a = jnp.dot(amat, chosen.astype(BF16), preferred_element_type=F32)
    cnt = jnp.zeros(s1.shape, F32)
    for a in range(PEER_TOPK):
        cnt = jnp.where(r1 == float(a), per_a[a:a + 1], cnt)
    e1n = jnp.exp(s1 - t1[0:1]) / z
    e2 = jnp.exp(s2 - t2[0:1])
    removed = lambda st: jnp.sum((st < float(PEER_TOPK)).astype(F32), axis=0, keepdims=True)
    ties = (jnp.abs(removed(r1) - PEER_TOPK) + jnp.abs(removed(r2) - PEER_TOPK)
            + jnp.abs(jnp.sum(chosen, axis=0, keepdims=True) - PEER_TOPK))
    return cnt, e1n, r2, e2, ties


def _peer_route_kernel(x_ref, g_ref, wq_ref, keys_ref, pos_ref, amat_ref,
                       hn_ref, cnt_ref, e1_ref, rb_ref, e2_ref, sc_ref):
    tm = x_ref.shape[0]
    hn = _rms(x_ref[...].astype(F32), g_ref[...]).astype(BF16)
    hn_ref[...] = hn
    qry = jnp.dot(hn, wq_ref[...], preferred_element_type=F32)
    for hp in range(2 * PEER_HEADS):
        sc_ref[hp] = _dot_nt(keys_ref[hp // 2, hp % 2], qry[:, hp * PEER_HALF:(hp + 1) * PEER_HALF].astype(BF16))

    def route(h, exact):
        cnt, e1n, r2, e2, ties = _route_head(sc_ref[2 * h], sc_ref[2 * h + 1], pos_ref[...], amat_ref[...], exact)
        cnt_ref[h] = cnt
        e1_ref[h] = e1n
        rb_ref[h] = pltpu.bitcast(r2.astype(BF16), jnp.uint32)
        e2_ref[h] = pltpu.bitcast(e2.astype(BF16), jnp.uint32)
        return ties

    for h in range(PEER_HEADS):
        ties = route(h, exact=False)

        @pl.when(jnp.max(ties) > 0.0)
        def _():
            route(h, exact=True)


def _peer_route(x, norm_gain, w_query, sub_keys, tm=256):
    T, D = x.shape
    H = PEER_HEADS
    pos = np.broadcast_to((_CAND_A * PEER_TOPK + _CAND_B).astype(np.float32)[:, None], (N_CAND, tm))
    amat = (np.arange(PEER_TOPK)[:, None] == _CAND_A[None, :]).astype(np.float32)
    gate_shape = jax.ShapeDtypeStruct((H, PEER_N_KEYS, T), F32)
    packed_shape = jax.ShapeDtypeStruct((H, PEER_N_KEYS // 2, T), jnp.uint32)
    gate_spec = pl.BlockSpec((H, PEER_N_KEYS, tm), lambda i: (0, 0, i))
    packed_spec = pl.BlockSpec((H, PEER_N_KEYS // 2, tm), lambda i: (0, 0, i))
    return pl.pallas_call(
        _peer_route_kernel,
        grid=(T // tm,),
        in_specs=[
            pl.BlockSpec((tm, D), lambda i: (i, 0)),
            pl.BlockSpec((1, D), lambda i: (0, 0)),
            pl.BlockSpec(w_query.shape, lambda i: (0, 0)),
            pl.BlockSpec(sub_keys.shape, lambda i: (0, 0, 0, 0)),
            pl.BlockSpec((N_CAND, tm), lambda i: (0, 0)),
            pl.BlockSpec((PEER_TOPK, N_CAND), lambda i: (0, 0)),
        ],
        out_specs=[pl.BlockSpec((tm, D), lambda i: (i, 0)), gate_spec, gate_spec, packed_spec, packed_spec],
        out_shape=[jax.ShapeDtypeStruct((T, D), BF16), gate_shape, gate_shape, packed_shape, packed_shape],
        scratch_shapes=[pltpu.VMEM((2 * H, PEER_N_KEYS, tm), F32)],
        compiler_params=_cparams(("parallel",)),
        name="peer_route",
    )(x, norm_gain.reshape(1, D).astype(F32), w_query, sub_keys, jnp.asarray(pos), jnp.asarray(amat, BF16))


def _gelu(a):
    return 0.5 * a * (1.0 + lax.erf(a * (2.0 ** -0.5)))


def _gate_weights(cnt_ref, e1_ref, rb_ref, e2_ref, w_ref, step, slot, rows_per_step):
    tile = (PEER_N_KEYS, LANES)
    half = PEER_N_KEYS // 2
    for il in range(rows_per_step):
        i = step * rows_per_step + il
        cnt = [cnt_ref[pl.ds(h * PEER_N_KEYS + i, 1), :] for h in range(PEER_HEADS)]
        e1 = [e1_ref[pl.ds(h * PEER_N_KEYS + i, 1), :] for h in range(PEER_HEADS)]
        for st in range(w_ref.shape[2] // LANES):
            ls = slice(st * LANES, (st + 1) * LANES)
            w = jnp.zeros(tile, BF16)
            for h in range(PEER_HEADS):
                hs = slice(h * half, (h + 1) * half)
                rb = pltpu.bitcast(rb_ref[hs, ls], BF16)
                e2 = pltpu.bitcast(e2_ref[hs, ls], BF16)
                cnt_t = jnp.broadcast_to(cnt[h][:, ls], tile).astype(BF16)
                e1_t = jnp.broadcast_to(e1[h][:, ls], tile).astype(BF16)
                w = w + jnp.where(rb < cnt_t, e2, 0.0) * e1_t
            w_ref[slot, il * half:(il + 1) * half, ls] = pltpu.bitcast(w, jnp.uint32)


def _peer_expert_kernel(hn_ref, u_ref, vt_ref, cnt_ref, e1_ref, rb_ref, e2_ref, x_ref, o_ref, acc_ref, g_ref, w_ref,
                        *, rows_per_step):
    e = pl.program_id(1)
    gate = functools.partial(_gate_weights, cnt_ref, e1_ref, rb_ref, e2_ref, w_ref, rows_per_step=rows_per_step)

    @pl.when(e == 0)
    def _():
        acc_ref[...] = jnp.zeros_like(acc_ref)
        gate(step=0, slot=0)

    gate(step=jnp.minimum(e + 1, pl.num_programs(1) - 1), slot=(e + 1) % 2)
    act = _gelu(_dot_nt(u_ref[...], hn_ref[...]))
    g_ref[...] = (pltpu.bitcast(w_ref[e % 2], BF16).astype(F32) * act).astype(BF16)
    acc_ref[...] += jnp.dot(vt_ref[...], g_ref[...], preferred_element_type=F32)

    @pl.when(e == pl.num_programs(1) - 1)
    def _():
        o_ref[...] = x_ref[...] + acc_ref[...].T


def _peer_experts(x, hn, gates, u, vt, tm=512, te=1024):
    T, D = x.shape
    NE = u.shape[0]
    gates = [g.reshape(-1, T) for g in gates]
    once = pl.Buffered(1)
    gate_spec = pl.BlockSpec((PEER_HEADS * PEER_N_KEYS, tm), lambda i, e: (0, i), pipeline_mode=once)
    packed_spec = pl.BlockSpec((PEER_HEADS * PEER_N_KEYS // 2, tm), lambda i, e: (0, i), pipeline_mode=once)
    return pl.pallas_call(
        functools.partial(_peer_expert_kernel, rows_per_step=te // PEER_N_KEYS),
        grid=(T // tm, NE // te),
        in_specs=[
            pl.BlockSpec((tm, D), lambda i, e: (i, 0), pipeline_mode=once),
            pl.BlockSpec((te, D), lambda i, e: (e, 0)),
            pl.BlockSpec((D, te), lambda i, e: (0, e)),
            gate_spec, gate_spec, packed_spec, packed_spec,
            pl.BlockSpec((tm, D), lambda i, e: (i, 0), pipeline_mode=once),
        ],
        out_specs=pl.BlockSpec((tm, D), lambda i, e: (i, 0)),
        out_shape=jax.ShapeDtypeStruct((T, D), F32),
        scratch_shapes=[pltpu.VMEM((D, tm), F32), pltpu.VMEM((te, tm), BF16),
                        pltpu.VMEM((2, te // 2, tm), jnp.uint32)],
        compiler_params=_cparams(("parallel", "arbitrary")),
        name="peer_experts",
    )(hn, u, vt, *gates, x)


def _peer(x, norm_gain, w_query, sub_keys, u, v):
    hn, cnt, e1, rb, e2 = _peer_route(x, norm_gain, w_query.astype(BF16), sub_keys.astype(BF16))
    return _peer_experts(x, hn, (cnt, e1, rb, e2), u.astype(BF16), v.astype(BF16).T)


def _even_layer(x, groups, norm, w_in, na_q_gain, na_k_gain, na_rpb, lb, hg_out_gain, w_out):
    p = _matmul(x, w_in.astype(BF16), gain=norm, out_dtype=F32)
    na = _neighbourhood_attention(p, groups, na_rpb, na_q_gain, na_k_gain)
    o_fw = _hgrn2(p, groups, lb[0], reverse=False, f_col=4 * NA_HEADS)
    hg = _hgrn2(p, groups, lb[1], reverse=True, f_col=5 * NA_HEADS, o_fw=o_fw, out_gain=hg_out_gain)
    w_na, w_hg = jnp.split(w_out.astype(BF16), [na.shape[1]], axis=0)
    return _matmul(na, w_na, pair=(hg, w_hg), residual=x, out_dtype=F32)


def _odd_layer(x, groups, norm, w_in, q_norm, kv_norm, w_uq, w_ukv, q_gain, k_gain, w_out):
    D = x.shape[1]
    pad_q = (-MLA_Q_RANK) % MLA_KV_RANK
    w_cq, w_ckv, w_kr = jnp.split(w_in, [MLA_Q_RANK, MLA_Q_RANK + MLA_KV_RANK], axis=1)
    w_in_p = jnp.concatenate(
        [w_cq, jnp.zeros((D, pad_q), w_in.dtype), w_ckv, w_kr, jnp.zeros((D, LANES - MLA_ROPE), w_in.dtype)],
        axis=1).astype(BF16)
    cc = _matmul(x, w_in_p, gain=norm, out_dtype=F32, tm=512, tn=w_in_p.shape[1])
    w_uq_p = jnp.pad(w_uq.reshape(MLA_Q_RANK, MLA_HEADS, MLA_QK),
                     ((0, 0), (0, 0), (0, MLA_HEAD_PAD - MLA_QK))).reshape(MLA_Q_RANK, -1).astype(BF16)
    q = _matmul(cc, w_uq_p, gain=q_norm, out_dtype=BF16, x_col_block=0)
    kv = _matmul(cc, w_ukv.astype(BF16), gain=kv_norm, out_dtype=BF16,
                 x_col_block=(MLA_Q_RANK + pad_q) // MLA_KV_RANK)
    kr_block = (MLA_Q_RANK + pad_q + MLA_KV_RANK) // LANES
    qf, kf, vt = _mla_prep(q, kv, cc, kr_block, groups, q_gain, k_gain)
    o = _mla_attention(qf, kf, vt, groups)
    return _matmul(o, w_out.astype(BF16), residual=x, out_dtype=F32)


def _forward(xs, ev_norm, ev_w_in, ev_na_q_gain, ev_na_k_gain, ev_na_rpb, ev_hg_lb_logits, ev_hg_out_gain,
             ev_w_out, od_norm, od_w_in, od_q_norm, od_kv_norm, od_w_uq, od_w_ukv, od_q_gain, od_k_gain,
             od_w_out, ff_norm, peer_w_query, peer_sub_keys, peer_u, peer_v):
    groups = [(int(x.shape[0]), int(x.shape[1])) for x in xs]
    D = xs[0].shape[-1]
    depth = ff_norm.shape[0]
    lb_all = jnp.cumsum(jax.nn.softmax(ev_hg_lb_logits.astype(F32), axis=1), axis=1)
    x = jnp.concatenate([t.reshape(-1, D) for t in xs], axis=0)
    for layer in range(depth):
        j = layer // 2
        if layer % 2 == 0:
            x = _even_layer(x, groups, ev_norm[j], ev_w_in[j], ev_na_q_gain[j], ev_na_k_gain[j], ev_na_rpb[j],
                            lb_all[:, j], ev_hg_out_gain[j], ev_w_out[j])
        else:
            x = _odd_layer(x, groups, od_norm[j], od_w_in[j], od_q_norm[j], od_kv_norm[j], od_w_uq[j],
                           od_w_ukv[j], od_q_gain[j], od_k_gain[j], od_w_out[j])
        x = _peer(x, ff_norm[layer], peer_w_query[layer], peer_sub_keys[layer], peer_u[layer], peer_v[layer])
    outs = []
    tok0 = 0
    for (B, S) in groups:
        outs.append(x[tok0:tok0 + B * S].reshape(B, S, D))
        tok0 += B * S
    return tuple(outs)


def kernel(x_prompt, x_sample, ev_norm, ev_w_in, ev_na_q_gain, ev_na_k_gain, ev_na_rpb, ev_hg_lb_logits, ev_hg_out_gain, ev_w_out, od_norm, od_w_in, od_q_norm, od_kv_norm, od_w_uq, od_w_ukv, od_q_gain, od_k_gain, od_w_out, ff_norm, peer_w_query, peer_sub_keys, peer_u, peer_v):
    return _forward((x_prompt, x_sample), ev_norm, ev_w_in, ev_na_q_gain, ev_na_k_gain, ev_na_rpb, ev_hg_lb_logits,
                    ev_hg_out_gain, ev_w_out, od_norm, od_w_in, od_q_norm, od_kv_norm, od_w_uq, od_w_ukv, od_q_gain,
                    od_k_gain, od_w_out, ff_norm, peer_w_query, peer_sub_keys, peer_u, peer_v)
```
